```python
import jax, jax.numpy as jnp
from jax import lax
import numpy as np

D_MODEL = 2048
BATCH = 4
SEQ = 4096
DEPTH = 1

GLA_HEADS = 4
GLA_DK = 128
GLA_DV = 256
GLA_RANK = 16
GLA_GATE_NORM = 16.0
GLA_CHUNK = 64
SWA_HEADS = 16
SWA_KV_HEADS = 2
SWA_HEAD_DIM = 64
WINDOW = 128
ROPE_THETA = 10000.0
D_FF = 5632
CONV_WIDTH = 3
EPS = 1e-6
NEG_INF = -1e30

GLA_QK_W = GLA_HEADS * GLA_DK
GLA_V_W = GLA_HEADS * GLA_DV
SWA_Q_W = SWA_HEADS * SWA_HEAD_DIM
SWA_KV_W = SWA_KV_HEADS * SWA_HEAD_DIM
MIX_WIDTH = GLA_V_W + SWA_Q_W
IN_SPLITS = (GLA_QK_W, GLA_QK_W, GLA_V_W, GLA_V_W, GLA_RANK, SWA_Q_W, SWA_KV_W, SWA_KV_W)
IN_WIDTH = GLA_QK_W * 2 + GLA_V_W * 2 + GLA_RANK + SWA_Q_W + SWA_KV_W * 2

kernel_name = "hymba_gla_swa_sink_convffn"


def rms_norm(x, g):
    xf = x.astype(jnp.float32)
    y = xf * lax.rsqrt(jnp.mean(xf * xf, axis=-1, keepdims=True) + EPS)
    return (y * g.astype(jnp.float32)).astype(x.dtype)


def split_cols(proj):
    outs = []
    start = 0
    for w in IN_SPLITS:
        outs.append(proj[..., start:start + w])
        start += w
    return outs


def rope(t, positions):
    half = t.shape[-1] // 2
    inv_freq = ROPE_THETA ** (-jnp.arange(half, dtype=jnp.float32) / half)
    ang = positions.astype(jnp.float32)[..., None] * inv_freq
    cos = jnp.cos(ang)[:, :, None, :]
    sin = jnp.sin(ang)[:, :, None, :]
    tf = t.astype(jnp.float32)
    t1, t2 = tf[..., :half], tf[..., half:]
    return jnp.concatenate([t1 * cos - t2 * sin, t2 * cos + t1 * sin], axis=-1).astype(t.dtype)


def gla_group(q, k, v, g_out, a_low, w_a_up, b_a_up, gn_gain):
    B, S, _ = q.shape
    C = GLA_CHUNK
    N = S // C
    f32 = jnp.float32
    log_a = jax.nn.log_sigmoid((a_low.astype(f32) @ w_a_up.astype(f32)) + b_a_up.astype(f32)) / GLA_GATE_NORM

    def to_chunks(t, d):
        return t.astype(f32).reshape(B, N, C, GLA_HEADS, d).transpose(0, 3, 1, 2, 4)

    qc = to_chunks(q, GLA_DK) * (GLA_DK ** -0.5)
    kc = to_chunks(k, GLA_DK)
    vc = to_chunks(v, GLA_DV)
    bcum = jnp.cumsum(to_chunks(log_a, GLA_DK), axis=3)
    b_last = bcum[:, :, :, -1:, :]
    q_e = qc * jnp.exp(bcum)
    k_e = kc * jnp.exp(-bcum)
    k_d = kc * jnp.exp(b_last - bcum)

    causal = jnp.tril(jnp.ones((C, C), dtype=bool))
    scores = jnp.where(causal, jnp.einsum('bhnid,bhnjd->bhnij', q_e, k_e), 0.0)
    o_intra = jnp.einsum('bhnij,bhnjv->bhniv', scores, vc)

    decay = jnp.exp(b_last[:, :, :, 0, :])
    kv = jnp.einsum('bhncd,bhncv->bhndv', k_d, vc)

    def step(state, inp):
        dec, kv_n = inp
        return dec[..., None] * state + kv_n, state

    s0 = jnp.zeros((B, GLA_HEADS, GLA_DK, GLA_DV), dtype=f32)
    _, s_prev = lax.scan(step, s0, (jnp.moveaxis(decay, 2, 0), jnp.moveaxis(kv, 2, 0)))
    s_prev = jnp.moveaxis(s_prev, 0, 2)
    o_inter = jnp.einsum('bhnid,bhndv->bhniv', q_e, s_prev)

    o = (o_intra + o_inter).transpose(0, 2, 3, 1, 4).reshape(B, S, GLA_HEADS, GLA_DV)
    o = rms_norm(o, gn_gain)
    o = o * jax.nn.silu(g_out.astype(f32).reshape(B, S, GLA_HEADS, GLA_DV))
    return o.reshape(B, S, GLA_V_W).astype(q.dtype)


def swa_sink_group(q, k, v, sinks, positions):
    B, S, _ = q.shape
    G = SWA_HEADS // SWA_KV_HEADS
    Wb = WINDOW
    NB = S // Wb
    hd = SWA_HEAD_DIM
    q = rope(q.reshape(B, S, SWA_HEADS, hd), positions)
    k = rope(k.reshape(B, S, SWA_KV_HEADS, hd), positions)
    v = v.reshape(B, S, SWA_KV_HEADS, hd)

    qb = q.reshape(B, NB, Wb, SWA_KV_HEADS, G, hd)
    kb = k.reshape(B, NB, Wb, SWA_KV_HEADS, hd)
    vb = v.reshape(B, NB, Wb, SWA_KV_HEADS, hd)
    pad = ((0, 0), (1, 0), (0, 0), (0, 0), (0, 0))
    kk = jnp.concatenate([jnp.pad(kb, pad)[:, :-1], kb], axis=2)
    vv = jnp.concatenate([jnp.pad(vb, pad)[:, :-1], vb], axis=2)

    s = jnp.einsum('bnqhgd,bnkhd->bnhgqk', qb, kk).astype(jnp.float32) * (hd ** -0.5)
    qi = jnp.arange(Wb)[:, None] + Wb
    ki = jnp.arange(2 * Wb)[None, :]
    band = (ki <= qi) & (qi - ki < WINDOW)
    has_prev = (jnp.arange(NB) > 0)[:, None, None] | (ki >= Wb)[None]
    mask = band[None] & has_prev
    s = jnp.where(mask[None, :, None, None, :, :], s, NEG_INF)

    sink_col = jnp.broadcast_to(
        sinks.astype(jnp.float32).reshape(1, 1, SWA_KV_HEADS, G, 1, 1),
        (B, NB, SWA_KV_HEADS, G, Wb, 1))
    p = jax.nn.softmax(jnp.concatenate([s, sink_col], axis=-1), axis=-1)[..., :-1]
    o = jnp.einsum('bnhgqk,bnkhd->bnqhgd', p.astype(vv.dtype), vv)
    return o.reshape(B, S, SWA_Q_W)


def conv_ffn(x, w_gate, w_up, conv_w, conv_b, w_down):
    S = x.shape[1]
    gte = x @ w_gate
    gp = jnp.pad(gte, ((0, 0), (CONV_WIDTH - 1, 0), (0, 0)))
    conv = conv_b
    for tap in range(CONV_WIDTH):
        conv = conv + conv_w[tap] * gp[:, tap:tap + S]
    h = jax.nn.silu(conv) * (x @ w_up)
    return h @ w_down


def setup_inputs(seed: int = 0) -> dict:
    key = jax.random.key(seed)
    ks = jax.random.split(key, 18)
    f32 = jnp.float32

    def nrm(k, shape, scale):
        return jax.random.normal(k, shape, dtype=f32) * scale

    x = jax.random.normal(ks[0], (BATCH, SEQ, D_MODEL), dtype=f32)
    offset = jax.random.randint(ks[1], (BATCH, 1), 0, 1024, dtype=jnp.int32)
    positions = offset + jnp.arange(SEQ, dtype=jnp.int32)[None, :]
    return {
        "x": x,
        "positions": positions,
        "attn_norm": 1.0 + nrm(ks[2], (DEPTH, D_MODEL), 0.02),
        "w_in": nrm(ks[3], (DEPTH, D_MODEL, IN_WIDTH), D_MODEL ** -0.5),
        "w_a_up": nrm(ks[4], (DEPTH, GLA_RANK, GLA_QK_W), GLA_RANK ** -0.5),
        "b_a_up": nrm(ks[5], (DEPTH, GLA_QK_W), 0.1),
        "gla_norm": 1.0 + nrm(ks[6], (DEPTH, GLA_DV), 0.02),
        "sinks": nrm(ks[7], (DEPTH, SWA_HEADS), 0.5),
        "w_out": nrm(ks[8], (DEPTH, MIX_WIDTH, D_MODEL), MIX_WIDTH ** -0.5),
        "ffn_norm": 1.0 + nrm(ks[9], (DEPTH, D_MODEL), 0.02),
        "w_gate": nrm(ks[10], (DEPTH, D_MODEL, D_FF), D_MODEL ** -0.5),
        "w_up": nrm(ks[11], (DEPTH, D_MODEL, D_FF), D_MODEL ** -0.5),
        "conv_w": nrm(ks[12], (DEPTH, CONV_WIDTH, D_FF), CONV_WIDTH ** -0.5),
        "conv_b": nrm(ks[13], (DEPTH, D_FF), 0.01),
        "w_down": nrm(ks[14], (DEPTH, D_FF, D_MODEL), D_FF ** -0.5),
        "final_norm": 1.0 + nrm(ks[15], (D_MODEL,), 0.02),
    }


def reference(x, positions, attn_norm, w_in, w_a_up, b_a_up, gla_norm, sinks, w_out,
              ffn_norm, w_gate, w_up, conv_w, conv_b, w_down, final_norm):
    h = x
    for l in range(DEPTH):
        u = rms_norm(h, attn_norm[l])
        gq, gk, gv, gg, ga, sq, sk, sv = split_cols(u @ w_in[l])
        o_gla = gla_group(gq, gk, gv, gg, ga, w_a_up[l], b_a_up[l], gla_norm[l])
        o_swa = swa_sink_group(sq, sk, sv, sinks[l], positions)
        h = h + jnp.concatenate([o_gla, o_swa.astype(h.dtype)], axis=-1) @ w_out[l]
        h = h + conv_ffn(rms_norm(h, ffn_norm[l]), w_gate[l], w_up[l], conv_w[l], conv_b[l], w_down[l])
    return rms_norm(h, final_norm)
```

```python
import functools

import jax
import jax.numpy as jnp
from jax import lax
from jax.experimental import pallas as pl
from jax.experimental.pallas import tpu as pltpu

D_MODEL = 2048
GLA_HEADS = 4
GLA_DK = 128
GLA_DV = 256
GLA_RANK = 16
GLA_GATE_NORM = 16.0
GLA_CHUNK = 64
SWA_HEADS = 16
SWA_KV_HEADS = 2
SWA_HEAD_DIM = 64
WINDOW = 128
ROPE_THETA = 10000.0
D_FF = 5632
CONV_WIDTH = 3
EPS = 1e-6
NEG_INF = -1e30

GLA_QK_W = GLA_HEADS * GLA_DK
GLA_V_W = GLA_HEADS * GLA_DV
SWA_Q_W = SWA_HEADS * SWA_HEAD_DIM
SWA_KV_W = SWA_KV_HEADS * SWA_HEAD_DIM

LANES = 128
SUBLANES = 8

COL_GQ = 0
COL_GK = COL_GQ + GLA_QK_W
COL_GV = COL_GK + GLA_QK_W
COL_GG = COL_GV + GLA_V_W
COL_SQ = COL_GG + GLA_V_W
COL_SK = COL_SQ + SWA_Q_W
COL_SV = COL_SK + SWA_KV_W
COL_GA = COL_SV + SWA_KV_W
PROJ_TN = 1536
PROJ_W = 3 * PROJ_TN

VMEM_LIMIT = 58 * 1024 * 1024

F32 = jnp.float32
BF16 = jnp.bfloat16


def _rms(x, gain):
    return x * lax.rsqrt(jnp.mean(x * x, axis=-1, keepdims=True) + EPS) * gain


def _inproj_kernel(x_ref, g_ref, w_ref, o_ref, u_ref):
    @pl.when(pl.program_id(1) == 0)
    def _():
        u_ref[...] = _rms(x_ref[...], g_ref[...]).astype(BF16)

    o_ref[...] = jnp.dot(u_ref[...], w_ref[...], preferred_element_type=F32).astype(BF16)


def _inproj(x2, gain, w, tm):
    t = x2.shape[0]
    return pl.pallas_call(
        _inproj_kernel,
        out_shape=jax.ShapeDtypeStruct((t, PROJ_W), BF16),
        grid=(t // tm, PROJ_W // PROJ_TN),
        in_specs=[
            pl.BlockSpec((tm, D_MODEL), lambda i, j: (i, 0)),
            pl.BlockSpec((1, D_MODEL), lambda i, j: (0, 0)),
            pl.BlockSpec((D_MODEL, PROJ_TN), lambda i, j: (0, j)),
        ],
        out_specs=pl.BlockSpec((tm, PROJ_TN), lambda i, j: (i, j)),
        scratch_shapes=[pltpu.VMEM((tm, D_MODEL), BF16)],
        compiler_params=pltpu.CompilerParams(
            dimension_semantics=("arbitrary", "arbitrary"),
            vmem_limit_bytes=VMEM_LIMIT),
        name="inproj",
    )(x2, gain, w)


def _gla_kernel(q_ref, k_ref, v_ref, g_ref, a_ref, wa_ref, ba_ref, gn_ref, o_ref, s_ref, *, ts):
    c = GLA_CHUNK

    @pl.when(pl.program_id(2) == 0)
    def _():
        s_ref[...] = jnp.zeros_like(s_ref)

    z = jnp.dot(a_ref[...], wa_ref[...], preferred_element_type=F32) + ba_ref[...]
    log_a = (jnp.minimum(z, 0.0) - jnp.log1p(jnp.exp(-jnp.abs(z)))) / GLA_GATE_NORM

    row = lax.broadcasted_iota(jnp.int32, (c, c), 0)
    col = lax.broadcasted_iota(jnp.int32, (c, c), 1)
    causal = col <= row
    tril = causal.astype(BF16)
    gain = gn_ref[...]
    scale = GLA_DK ** -0.5

    for n in range(ts // c):
        sl = pl.ds(n * c, c)
        la = log_a[n * c:(n + 1) * c]
        la_hi = la.astype(BF16)
        la_lo = (la - la_hi.astype(F32)).astype(BF16)
        bcum = (jnp.dot(tril, la_hi, preferred_element_type=F32)
                + jnp.dot(tril, la_lo, preferred_element_type=F32))
        b_last = bcum[c - 1:c, :]
        qn = q_ref[sl, :].astype(F32) * scale
        kn = k_ref[sl, :].astype(F32)
        vn = v_ref[sl, :]
        q_e = (qn * jnp.exp(bcum)).astype(BF16)
        k_e = (kn * jnp.exp(-bcum)).astype(BF16)
        k_d = (kn * jnp.exp(b_last - bcum)).astype(BF16)

        scores = lax.dot_general(q_e, k_e, (((1,), (1,)), ((), ())), preferred_element_type=F32)
        scores = jnp.where(causal, scores, 0.0).astype(BF16)
        o_intra = jnp.dot(scores, vn, preferred_element_type=F32)

        state_t = s_ref[...]
        o_inter = lax.dot_general(q_e, state_t.astype(BF16), (((1,), (1,)), ((), ())),
                                  preferred_element_type=F32)
        kv_t = lax.dot_general(vn, k_d, (((0,), (0,)), ((), ())), preferred_element_type=F32)
        s_ref[...] = state_t * jnp.exp(b_last) + kv_t

        o = _rms(o_intra + o_inter, gain)
        gate = g_ref[sl, :].astype(F32)
        o_ref[sl, :] = (o * (gate * jax.nn.sigmoid(gate))).astype(BF16)


def _gla(proj, wa, ba, gn, batch, seq, ts):
    nt = seq // ts
    kern = functools.partial(_gla_kernel, ts=ts)
    rows = lambda b, h, i: b * nt + i
    return pl.pallas_call(
        kern,
        out_shape=jax.ShapeDtypeStruct((batch * seq, GLA_V_W), BF16),
        grid=(batch, GLA_HEADS, nt),
        in_specs=[
            pl.BlockSpec((ts, GLA_DK), lambda b, h, i: (rows(b, h, i), COL_GQ // GLA_DK + h)),
            pl.BlockSpec((ts, GLA_DK), lambda b, h, i: (rows(b, h, i), COL_GK // GLA_DK + h)),
            pl.BlockSpec((ts, GLA_DV), lambda b, h, i: (rows(b, h, i), COL_GV // GLA_DV + h)),
            pl.BlockSpec((ts, GLA_DV), lambda b, h, i: (rows(b, h, i), COL_GG // GLA_DV + h)),
            pl.BlockSpec((ts, LANES), lambda b, h, i: (rows(b, h, i), COL_GA // LANES)),
            pl.BlockSpec((LANES, GLA_DK), lambda b, h, i: (0, h)),
            pl.BlockSpec((1, GLA_DK), lambda b, h, i: (0, h)),
            pl.BlockSpec((1, GLA_DV), lambda b, h, i: (0, 0)),
        ],
        out_specs=pl.BlockSpec((ts, GLA_DV), lambda b, h, i: (rows(b, h, i), h)),
        scratch_shapes=[pltpu.VMEM((GLA_DV, GLA_DK), F32)],
        compiler_params=pltpu.CompilerParams(
            dimension_semantics=("arbitrary", "arbitrary", "arbitrary"),
            vmem_limit_bytes=VMEM_LIMIT),
        name="gla",
    )(proj, proj, proj, proj, proj, wa, ba, gn)


def _rope(t, cos, sin, first_half):
    w = t.shape[-1]
    nxt = pltpu.roll(t, w - SWA_HEAD_DIM // 2, 1)
    prv = pltpu.roll(t, SWA_HEAD_DIM // 2, 1)
    return t * cos + jnp.where(first_half, -nxt, prv) * sin


def _swa_kernel(sinks_ref, q_ref, k_ref, v_ref, pos_ref, freq_ref, o_ref, kp_ref, vp_ref):
    nb = pl.program_id(1)
    wb = WINDOW
    hd = SWA_HEAD_DIM
    groups = SWA_HEADS // SWA_KV_HEADS

    @pl.when(nb == 0)
    def _():
        kp_ref[...] = jnp.zeros_like(kp_ref)
        vp_ref[...] = jnp.zeros_like(vp_ref)

    ang = pos_ref[...].astype(F32) * freq_ref[...]
    cos = jnp.cos(ang)
    sin = jnp.sin(ang)
    lane = lax.broadcasted_iota(jnp.int32, (wb, LANES), 1)
    first_half = (lane % hd) < hd // 2

    k_cur = _rope(k_ref[...].astype(F32), cos, sin, first_half).astype(BF16)
    v_cur = v_ref[...]
    k_all = jnp.concatenate([kp_ref[...], k_cur], axis=0)
    v_all = jnp.concatenate([vp_ref[...], v_cur], axis=0)

    qi = lax.broadcasted_iota(jnp.int32, (wb, 2 * wb), 0) + wb
    ki = lax.broadcasted_iota(jnp.int32, (wb, 2 * wb), 1)
    mask = (ki <= qi) & (qi - ki < WINDOW) & ((nb > 0) | (ki >= wb))

    scale = hd ** -0.5
    for lt in range(SWA_Q_W // LANES):
        q_t = _rope(q_ref[:, lt * LANES:(lt + 1) * LANES].astype(F32), cos, sin, first_half).astype(BF16)
        outs = []
        for sub in range(LANES // hd):
            head = lt * (LANES // hd) + sub
            kvh = head // groups
            q_h = q_t[:, sub * hd:(sub + 1) * hd]
            k_h = k_all[:, kvh * hd:(kvh + 1) * hd]
            v_h = v_all[:, kvh * hd:(kvh + 1) * hd]
            s = lax.dot_general(q_h, k_h, (((1,), (1,)), ((), ())), preferred_element_type=F32) * scale
            s = jnp.where(mask, s, NEG_INF)
            sink = sinks_ref[head]
            m = jnp.maximum(jnp.max(s, axis=-1, keepdims=True), sink)
            e = jnp.exp(s - m)
            denom = jnp.sum(e, axis=-1, keepdims=True) + jnp.exp(sink - m)
            p = (e / denom).astype(BF16)
            outs.append(jnp.dot(p, v_h, preferred_element_type=F32))
        o_ref[:, lt * LANES:(lt + 1) * LANES] = jnp.concatenate(outs, axis=-1).astype(BF16)

    kp_ref[...] = k_cur
    vp_ref[...] = v_cur


def _swa(proj, sinks, pos2, freq, batch, seq):
    nb = seq // WINDOW
    rows = lambda b, n: b * nb + n
    return pl.pallas_call(
        _swa_kernel,
        out_shape=jax.ShapeDtypeStruct((batch * seq, SWA_Q_W), BF16),
        grid=(batch, nb),
        in_specs=[
            pl.BlockSpec(memory_space=pltpu.SMEM),
            pl.BlockSpec((WINDOW, SWA_Q_W), lambda b, n: (rows(b, n), COL_SQ // SWA_Q_W)),
            pl.BlockSpec((WINDOW, SWA_KV_W), lambda b, n: (rows(b, n), COL_SK // SWA_KV_W)),
            pl.BlockSpec((WINDOW, SWA_KV_W), lambda b, n: (rows(b, n), COL_SV // SWA_KV_W)),
            pl.BlockSpec((WINDOW, 1), lambda b, n: (rows(b, n), 0)),
            pl.BlockSpec((1, LANES), lambda b, n: (0, 0)),
        ],
        out_specs=pl.BlockSpec((WINDOW, SWA_Q_W), lambda b, n: (rows(b, n), 0)),
        scratch_shapes=[pltpu.VMEM((WINDOW, SWA_KV_W), BF16), pltpu.VMEM((WINDOW, SWA_KV_W), BF16)],
        compiler_params=pltpu.CompilerParams(
            dimension_semantics=("arbitrary", "arbitrary"),
            vmem_limit_bytes=VMEM_LIMIT),
        name="swa",
    )(sinks, proj, proj, proj, pos2, freq)


def _outproj_kernel(x_ref, a_ref, b_ref, wa_ref, wb_ref, h_ref):
    acc = jnp.dot(a_ref[...], wa_ref[...], preferred_element_type=F32)
    acc = acc + jnp.dot(b_ref[...], wb_ref[...], preferred_element_type=F32)
    h_ref[...] = x_ref[...] + acc


def _outproj(x2, o_gla, o_swa, w_a, w_b, tm):
    t = x2.shape[0]
    return pl.pallas_call(
        _outproj_kernel,
        out_shape=jax.ShapeDtypeStruct((t, D_MODEL), F32),
        grid=(t // tm,),
        in_specs=[
            pl.BlockSpec((tm, D_MODEL), lambda i: (i, 0)),
            pl.BlockSpec((tm, GLA_V_W), lambda i: (i, 0)),
            pl.BlockSpec((tm, SWA_Q_W), lambda i: (i, 0)),
            pl.BlockSpec((GLA_V_W, D_MODEL), lambda i: (0, 0)),
            pl.BlockSpec((SWA_Q_W, D_MODEL), lambda i: (0, 0)),
        ],
        out_specs=pl.BlockSpec((tm, D_MODEL), lambda i: (i, 0)),
        compiler_params=pltpu.CompilerParams(
            dimension_semantics=("arbitrary",),
            vmem_limit_bytes=VMEM_LIMIT),
        name="outproj",
    )(x2, o_gla, o_swa, w_a, w_b)


def _ffn_kernel(h_ref, gf_ref, wg_ref, wu_ref, cw_ref, cb_ref, wd_ref, gl_ref, o_ref,
                hn_ref, carry_ref, *, tm, seq):
    i = pl.program_id(0)
    j = pl.program_id(1)
    nj = pl.num_programs(1)

    @pl.when(j == 0)
    def _():
        hn_ref[...] = _rms(h_ref[...], gf_ref[...]).astype(BF16)

    hn = hn_ref[...]
    gate = jnp.dot(hn, wg_ref[...], preferred_element_type=F32)
    up = jnp.dot(hn, wu_ref[...], preferred_element_type=F32)

    seq_start = (i * tm) % seq == 0
    prev = jnp.where(seq_start, 0.0, carry_ref[j])
    carry_ref[j] = gate[tm - SUBLANES:, :]
    rows = lax.broadcasted_iota(jnp.int32, prev.shape, 0)

    def shifted(d):
        rolled = pltpu.roll(gate, d, 0)
        top = jnp.where(rows < d, pltpu.roll(prev, d, 0), rolled[:SUBLANES])
        return jnp.concatenate([top, rolled[SUBLANES:]], axis=0)

    g1 = shifted(1)
    g2 = shifted(2)
    conv = cb_ref[...] + cw_ref[0:1, :] * g2 + cw_ref[1:2, :] * g1 + cw_ref[2:3, :] * gate
    act = (conv * jax.nn.sigmoid(conv) * up).astype(BF16)
    part = jnp.dot(act, wd_ref[...], preferred_element_type=F32)

    @pl.when(j == 0)
    def _():
        o_ref[...] = h_ref[...] + part

    @pl.when(j > 0)
    def _():
        o_ref[...] += part

    @pl.when(j == nj - 1)
    def _():
        o_ref[...] = _rms(o_ref[...], gl_ref[...])


def _ffn(h, gf, wg, wu, cw, cb, wd, gl, seq, tm, tf):
    t = h.shape[0]
    nj = D_FF // tf
    kern = functools.partial(_ffn_kernel, tm=tm, seq=seq)
    return pl.pallas_call(
        kern,
        out_shape=jax.ShapeDtypeStruct((t, D_MODEL), F32),
        grid=(t // tm, nj),
        in_specs=[
            pl.BlockSpec((tm, D_MODEL), lambda i, j: (i, 0)),
            pl.BlockSpec((1, D_MODEL), lambda i, j: (0, 0)),
            pl.BlockSpec((D_MODEL, tf), lambda i, j: (0, j)),
            pl.BlockSpec((D_MODEL, tf), lambda i, j: (0, j)),
            pl.BlockSpec((CONV_WIDTH, tf), lambda i, j: (0, j)),
            pl.BlockSpec((1, tf), lambda i, j: (0, j)),
            pl.BlockSpec((tf, D_MODEL), lambda i, j: (j, 0)),
            pl.BlockSpec((1, D_MODEL), lambda i, j: (0, 0)),
        ],
        out_specs=pl.BlockSpec((tm, D_MODEL), lambda i, j: (i, 0)),
        scratch_shapes=[pltpu.VMEM((tm, D_MODEL), BF16), pltpu.VMEM((nj, SUBLANES, tf), F32)],
        compiler_params=pltpu.CompilerParams(
            dimension_semantics=("arbitrary", "arbitrary"),
            vmem_limit_bytes=VMEM_LIMIT),
        name="convffn",
    )(h, gf, wg, wu, cw, cb, wd, gl)


def kernel(x, positions, attn_norm, w_in, w_a_up, b_a_up, gla_norm, sinks, w_out, ffn_norm,
           w_gate, w_up, conv_w, conv_b, w_down, final_norm):
    batch, seq, _ = x.shape
    t = batch * seq
    assert w_in.shape[0] == 1, "the final norm is fused into the single layer's FFN kernel"
    x2 = x.reshape(t, D_MODEL)
    pos2 = positions.reshape(t, 1)
    half = SWA_HEAD_DIM // 2
    inv_freq = ROPE_THETA ** (-jnp.arange(half, dtype=F32) / half)
    freq = jnp.tile(inv_freq, LANES // half).reshape(1, LANES)

    wl = w_in[0]
    o_ga = 2 * GLA_QK_W + 2 * GLA_V_W
    w_proj = jnp.concatenate(
        [wl[:, :o_ga], wl[:, o_ga + GLA_RANK:], wl[:, o_ga:o_ga + GLA_RANK],
         jnp.zeros((D_MODEL, PROJ_W - COL_GA - GLA_RANK), F32)], axis=1).astype(BF16)
    wa = jnp.concatenate(
        [w_a_up[0], jnp.zeros((LANES - GLA_RANK, GLA_QK_W), F32)], axis=0).astype(BF16)

    proj = _inproj(x2, attn_norm[0].reshape(1, D_MODEL), w_proj, tm=1024)
    o_gla = _gla(proj, wa, b_a_up[0].reshape(1, GLA_QK_W), gla_norm[0].reshape(1, GLA_DV),
                 batch, seq, ts=512)
    o_swa = _swa(proj, sinks[0], pos2, freq, batch, seq)
    wo = w_out[0].astype(BF16)
    h = _outproj(x2, o_gla, o_swa, wo[:GLA_V_W], wo[GLA_V_W:], tm=512)
    y = _ffn(h, ffn_norm[0].reshape(1, D_MODEL), w_gate[0].astype(BF16), w_up[0].astype(BF16),
             conv_w[0], conv_b[0].reshape(1, D_FF), w_down[0].astype(BF16),
             final_norm.reshape(1, D_MODEL), seq, tm=1024, tf=512)
    return y.reshape(batch, seq, D_MODEL)
```

```python
import functools

import jax
import jax.numpy as jnp
from jax import lax
from jax.experimental import pallas as pl
from jax.experimental.pallas import tpu as pltpu

D_MODEL = 2048
GLA_HEADS = 4
GLA_DK = 128
GLA_DV = 256
GLA_RANK = 16
GLA_GATE_NORM = 16.0
GLA_CHUNK = 64
SWA_HEADS = 16
SWA_KV_HEADS = 2
SWA_HEAD_DIM = 64
WINDOW = 128
ROPE_THETA = 10000.0
D_FF = 5632
CONV_WIDTH = 3
EPS = 1e-6
NEG_INF = -1e30
LOG2E = 1.4426950408889634

GLA_QK_W = GLA_HEADS * GLA_DK
GLA_V_W = GLA_HEADS * GLA_DV
SWA_Q_W = SWA_HEADS * SWA_HEAD_DIM
SWA_KV_W = SWA_KV_HEADS * SWA_HEAD_DIM

LANES = 128
SUBLANES = 8

COL_GQ = 0
COL_GK = COL_GQ + GLA_QK_W
COL_GV = COL_GK + GLA_QK_W
COL_GG = COL_GV + GLA_V_W
COL_SQ = COL_GG + GLA_V_W
COL_SK = COL_SQ + SWA_Q_W
COL_SV = COL_SK + SWA_KV_W
COL_GA = COL_SV + SWA_KV_W
PROJ_TN = 1536
PROJ_W = 3 * PROJ_TN

VMEM_LIMIT = 58 * 1024 * 1024

F32 = jnp.float32
BF16 = jnp.bfloat16

_NT = (((1,), (1,)), ((), ()))
_TN = (((0,), (0,)), ((), ()))


def _rms(x, gain):
    return x * lax.rsqrt(jnp.mean(x * x, axis=-1, keepdims=True) + EPS) * gain


def _inproj_kernel(x_ref, g_ref, w_ref, o_ref, u_ref, *, tm, rc):
    j = pl.program_id(1)

    @pl.when(j == 0)
    def _():
        for r in range(tm // rc):
            rs = pl.ds(r * rc, rc)
            u = _rms(x_ref[rs, :], g_ref[...]).astype(BF16)
            u_ref[rs, :] = u
            o_ref[rs, :] = jnp.dot(u, w_ref[...], preferred_element_type=F32).astype(BF16)

    @pl.when(j > 0)
    def _():
        o_ref[...] = jnp.dot(u_ref[...], w_ref[...], preferred_element_type=F32).astype(BF16)


def _inproj(x2, gain, w, tm, rc):
    t = x2.shape[0]
    return pl.pallas_call(
        functools.partial(_inproj_kernel, tm=tm, rc=rc),
        out_shape=jax.ShapeDtypeStruct((t, PROJ_W), BF16),
        grid=(t // tm, PROJ_W // PROJ_TN),
        in_specs=[
            pl.BlockSpec((tm, D_MODEL), lambda i, j: (i, 0)),
            pl.BlockSpec((1, D_MODEL), lambda i, j: (0, 0)),
            pl.BlockSpec((D_MODEL, PROJ_TN), lambda i, j: (0, j)),
        ],
        out_specs=pl.BlockSpec((tm, PROJ_TN), lambda i, j: (i, j)),
        scratch_shapes=[pltpu.VMEM((tm, D_MODEL), BF16)],
        compiler_params=pltpu.CompilerParams(
            dimension_semantics=("arbitrary", "arbitrary"),
            vmem_limit_bytes=VMEM_LIMIT),
        name="inproj",
    )(x2, gain, w)


def _gla_kernel(q_ref, k_ref, v_ref, g_ref, a_ref, wa_ref, ba_ref, gn_ref, tril_ref, mask_ref,
                o_ref, s_ref, *, ts):
    c = GLA_CHUNK
    nc = ts // c

    @pl.when(pl.program_id(1) == 0)
    def _():
        s_ref[...] = jnp.zeros_like(s_ref)

    z = jnp.dot(a_ref[...], wa_ref[...], preferred_element_type=F32) + ba_ref[...]
    log_a = (jnp.minimum(z, 0.0) - jnp.log1p(jnp.exp(-jnp.abs(z)))) / GLA_GATE_NORM

    la_hi = log_a.astype(BF16)
    la_lo = (log_a - la_hi.astype(F32)).astype(BF16)
    cum2 = jnp.dot(tril_ref[...], jnp.concatenate([la_hi, la_lo], axis=1), preferred_element_type=F32)
    bcum = cum2[:, :GLA_QK_W] + cum2[:, GLA_QK_W:]
    bcum3 = bcum.reshape(nc, c, GLA_QK_W)
    b_last = bcum3[:, c - 1:c, :]

    q = q_ref[...].astype(F32) * (GLA_DK ** -0.5)
    k = k_ref[...].astype(F32)
    q_e = (q * jnp.exp(bcum)).astype(BF16)
    k_e = (k * jnp.exp(-bcum)).astype(BF16)
    k_d = (k * jnp.exp(b_last - bcum3).reshape(ts, GLA_QK_W)).astype(BF16)
    decay = jnp.exp(b_last)
    keep = mask_ref[...] != 0.0

    for h in range(GLA_HEADS):
        ks = slice(h * GLA_DK, (h + 1) * GLA_DK)
        vs = slice(h * GLA_DV, (h + 1) * GLA_DV)
        v = v_ref[:, vs]
        scores = lax.dot_general(q_e[:, ks], k_e[:, ks], _NT, preferred_element_type=F32)
        scores = jnp.where(keep, scores, 0.0).astype(BF16)
        o_intra = jnp.dot(scores, v, preferred_element_type=F32)

        state_t = s_ref[h]
        o_inter = []
        for n in range(nc):
            rows = slice(n * c, (n + 1) * c)
            o_inter.append(lax.dot_general(q_e[rows, ks], state_t.astype(BF16), _NT,
                                           preferred_element_type=F32))
            kv_t = lax.dot_general(v[rows], k_d[rows, ks], _TN, preferred_element_type=F32)
            state_t = state_t * decay[n][:, ks] + kv_t
        s_ref[h] = state_t

        o = _rms(o_intra + jnp.concatenate(o_inter, axis=0), gn_ref[...])
        gate = g_ref[:, vs].astype(F32)
        o_ref[:, vs] = (o * (gate * jax.nn.sigmoid(gate))).astype(BF16)


def _gla(proj, wa, ba, gn, batch, seq, ts):
    nt = seq // ts
    kern = functools.partial(_gla_kernel, ts=ts)
    rows = lambda b, i: b * nt + i
    idx = jnp.arange(ts)
    causal = (idx[:, None] >= idx[None, :]) & (idx[:, None] // GLA_CHUNK == idx[None, :] // GLA_CHUNK)
    return pl.pallas_call(
        kern,
        out_shape=jax.ShapeDtypeStruct((batch * seq, GLA_V_W), BF16),
        grid=(batch, nt),
        in_specs=[
            pl.BlockSpec((ts, GLA_QK_W), lambda b, i: (rows(b, i), COL_GQ // GLA_QK_W)),
            pl.BlockSpec((ts, GLA_QK_W), lambda b, i: (rows(b, i), COL_GK // GLA_QK_W)),
            pl.BlockSpec((ts, GLA_V_W), lambda b, i: (rows(b, i), COL_GV // GLA_V_W)),
            pl.BlockSpec((ts, GLA_V_W), lambda b, i: (rows(b, i), COL_GG // GLA_V_W)),
            pl.BlockSpec((ts, LANES), lambda b, i: (rows(b, i), COL_GA // LANES)),
            pl.BlockSpec((LANES, GLA_QK_W), lambda b, i: (0, 0)),
            pl.BlockSpec((1, GLA_QK_W), lambda b, i: (0, 0)),
            pl.BlockSpec((1, GLA_DV), lambda b, i: (0, 0)),
            pl.BlockSpec((ts, ts), lambda b, i: (0, 0)),
            pl.BlockSpec((ts, ts), lambda b, i: (0, 0)),
        ],
        out_specs=pl.BlockSpec((ts, GLA_V_W), lambda b, i: (rows(b, i), 0)),
        scratch_shapes=[pltpu.VMEM((GLA_HEADS, GLA_DV, GLA_DK), F32)],
        compiler_params=pltpu.CompilerParams(
            dimension_semantics=("arbitrary", "arbitrary"),
            vmem_limit_bytes=VMEM_LIMIT),
        name="gla",
    )(proj, proj, proj, proj, proj, wa, ba, gn, causal.astype(BF16), causal.astype(F32))


def _swa_kernel(sinks_ref, q_ref, k_ref, v_ref, pos_ref, freq_ref, o_ref, k2p_ref, vt2p_ref):
    nb = pl.program_id(1)
    wb = WINDOW
    hd = SWA_HEAD_DIM
    half = hd // 2
    pairs_per_kv = SWA_Q_W // LANES // SWA_KV_HEADS

    @pl.when(nb == 0)
    def _():
        k2p_ref[...] = jnp.zeros_like(k2p_ref)
        vt2p_ref[...] = jnp.zeros_like(vt2p_ref)

    ang_t = freq_ref[...] * pos_ref[0].astype(F32)
    reps = LANES // half
    cos = jnp.concatenate([jnp.cos(ang_t)] * reps, axis=0).T
    sin = jnp.concatenate([jnp.sin(ang_t)] * reps, axis=0).T
    lane = lax.broadcasted_iota(jnp.int32, (wb, LANES), 1)
    sin = jnp.where(lane < hd, -sin, sin)
    head_a = (lane % hd) < half
    low = lane < hd

    def rope(t, cs, sn):
        return t * cs + pltpu.roll(t, hd, 1) * sn

    k_r = rope(k_ref[...].astype(F32), cos, sin)
    k2_cur = [jnp.where(head_a, k_r, pltpu.roll(k_r, half, 1)).astype(BF16),
              jnp.where(head_a, pltpu.roll(k_r, LANES - half, 1), k_r).astype(BF16)]
    v_f = v_ref[...].astype(F32)
    v_sw = pltpu.roll(v_f, hd, 1)
    vt2_cur = [jnp.where(low, v_f, v_sw).T.astype(BF16), jnp.where(low, v_sw, v_f).T.astype(BF16)]
    k2 = [jnp.concatenate([k2p_ref[c], k2_cur[c]], axis=0) for c in range(SWA_KV_HEADS)]
    vt2 = [jnp.concatenate([vt2p_ref[c], vt2_cur[c]], axis=1) for c in range(SWA_KV_HEADS)]

    ki = lax.broadcasted_iota(jnp.int32, (2 * wb, 2 * wb), 0)
    qi = lax.broadcasted_iota(jnp.int32, (2 * wb, 2 * wb), 1) % wb + wb
    mask = (ki <= qi) & (qi - ki < WINDOW) & ((nb > 0) | (ki >= wb))
    col_a = lax.broadcasted_iota(jnp.int32, (1, 2 * wb), 1) < wb
    row_a = lax.broadcasted_iota(jnp.int32, (LANES, wb), 0) < hd

    qscale = (hd ** -0.5) * LOG2E
    cos_q = cos * qscale
    sin_q = sin * qscale

    for t in range(SWA_Q_W // LANES):
        c = t // pairs_per_kv
        q_r = rope(q_ref[:, t * LANES:(t + 1) * LANES].astype(F32), cos_q, sin_q)
        lhs = jnp.concatenate([jnp.where(head_a, q_r, 0.0), jnp.where(head_a, 0.0, q_r)],
                              axis=0).astype(BF16)
        st = lax.dot_general(k2[c], lhs, _NT, preferred_element_type=F32)
        st = jnp.where(mask, st, NEG_INF)
        m = jnp.max(st, axis=0, keepdims=True)
        e = jnp.exp2(st - m)
        sink2 = jnp.where(col_a, sinks_ref[2 * t], sinks_ref[2 * t + 1]) * LOG2E
        rden = 1.0 / (jnp.sum(e, axis=0, keepdims=True) + jnp.exp2(sink2 - m))
        ot = jnp.dot(vt2[c], e.astype(BF16), preferred_element_type=F32)
        ot = jnp.where(row_a, ot[:, :wb] * rden[:, :wb], ot[:, wb:] * rden[:, wb:])
        o_ref[:, t * LANES:(t + 1) * LANES] = ot.T.astype(BF16)

    for c in range(SWA_KV_HEADS):
        k2p_ref[c] = k2_cur[c]
        vt2p_ref[c] = vt2_cur[c]


def _swa(proj, sinks, pos3, freq_col, batch, seq):
    nb = seq // WINDOW
    rows = lambda b, n: b * nb + n
    return pl.pallas_call(
        _swa_kernel,
        out_shape=jax.ShapeDtypeStruct((batch * seq, SWA_Q_W), BF16),
        grid=(batch, nb),
        in_specs=[
            pl.BlockSpec(memory_space=pltpu.SMEM),
            pl.BlockSpec((WINDOW, SWA_Q_W), lambda b, n: (rows(b, n), COL_SQ // SWA_Q_W)),
            pl.BlockSpec((WINDOW, SWA_KV_W), lambda b, n: (rows(b, n), COL_SK // SWA_KV_W)),
            pl.BlockSpec((WINDOW, SWA_KV_W), lambda b, n: (rows(b, n), COL_SV // SWA_KV_W)),
            pl.BlockSpec((1, 1, WINDOW), lambda b, n: (rows(b, n), 0, 0)),
            pl.BlockSpec((SWA_HEAD_DIM // 2, 1), lambda b, n: (0, 0)),
        ],
        out_specs=pl.BlockSpec((WINDOW, SWA_Q_W), lambda b, n: (rows(b, n), 0)),
        scratch_shapes=[pltpu.VMEM((SWA_KV_HEADS, WINDOW, LANES), BF16),
                        pltpu.VMEM((SWA_KV_HEADS, LANES, WINDOW), BF16)],
        compiler_params=pltpu.CompilerParams(
            dimension_semantics=("arbitrary", "arbitrary"),
            vmem_limit_bytes=VMEM_LIMIT),
        name="swa",
    )(sinks, proj, proj, proj, pos3, freq_col)


def _outproj_kernel(x_ref, a_ref, b_ref, wa_ref, wb_ref, h_ref):
    acc = jnp.dot(a_ref[...], wa_ref[...], preferred_element_type=F32)
    acc = acc + jnp.dot(b_ref[...], wb_ref[...], preferred_element_type=F32)
    h_ref[...] = x_ref[...] + acc


def _outproj(x2, o_gla, o_swa, w_a, w_b, tm):
    t = x2.shape[0]
    return pl.pallas_call(
        _outproj_kernel,
        out_shape=jax.ShapeDtypeStruct((t, D_MODEL), F32),
        grid=(t // tm,),
        in_specs=[
            pl.BlockSpec((tm, D_MODEL), lambda i: (i, 0)),
            pl.BlockSpec((tm, GLA_V_W), lambda i: (i, 0)),
            pl.BlockSpec((tm, SWA_Q_W), lambda i: (i, 0)),
            pl.BlockSpec((GLA_V_W, D_MODEL), lambda i: (0, 0)),
            pl.BlockSpec((SWA_Q_W, D_MODEL), lambda i: (0, 0)),
        ],
        out_specs=pl.BlockSpec((tm, D_MODEL), lambda i: (i, 0)),
        compiler_params=pltpu.CompilerParams(
            dimension_semantics=("arbitrary",),
            vmem_limit_bytes=VMEM_LIMIT),
        name="outproj",
    )(x2, o_gla, o_swa, w_a, w_b)


def _ffn_kernel(h_ref, gf_ref, wg_ref, wu_ref, cw_ref, cb_ref, wd_ref, gl_ref, o_ref,
                hn_ref, carry_ref, *, tm, rc, seq):
    i = pl.program_id(0)
    j = pl.program_id(1)
    nj = pl.num_programs(1)

    def step(first, last):
        seq_start = (i * tm) % seq == 0
        prev = jnp.where(seq_start, 0.0, carry_ref[j])
        rows = lax.broadcasted_iota(jnp.int32, prev.shape, 0)
        cb = cb_ref[...]
        cw0, cw1, cw2 = cw_ref[0:1, :], cw_ref[1:2, :], cw_ref[2:3, :]

        for r in range(tm // rc):
            rs = pl.ds(r * rc, rc)
            if first:
                res = h_ref[rs, :]
                hn = _rms(res, gf_ref[...]).astype(BF16)
                hn_ref[rs, :] = hn
            else:
                res = o_ref[rs, :]
                hn = hn_ref[rs, :]
            gate = jnp.dot(hn, wg_ref[...], preferred_element_type=F32)
            up = jnp.dot(hn, wu_ref[...], preferred_element_type=F32)

            def shifted(d):
                rolled = pltpu.roll(gate, d, 0)
                top = jnp.where(rows < d, pltpu.roll(prev, d, 0), rolled[:SUBLANES])
                return jnp.concatenate([top, rolled[SUBLANES:]], axis=0)

            conv = cb + cw0 * shifted(2) + cw1 * shifted(1) + cw2 * gate
            act = (conv * jax.nn.sigmoid(conv) * up).astype(BF16)
            acc = res + jnp.dot(act, wd_ref[...], preferred_element_type=F32)
            o_ref[rs, :] = _rms(acc, gl_ref[...]) if last else acc
            prev = gate[rc - SUBLANES:, :]
        carry_ref[j] = prev

    pl.when(j == 0)(functools.partial(step, True, False))
    pl.when((j > 0) & (j < nj - 1))(functools.partial(step, False, False))
    pl.when(j == nj - 1)(functools.partial(step, False, True))


def _ffn(h, gf, wg, wu, cw, cb, wd, gl, seq, tm, tf, rc):
    t = h.shape[0]
    nj = D_FF // tf
    kern = functools.partial(_ffn_kernel, tm=tm, rc=rc, seq=seq)
    return pl.pallas_call(
        kern,
        out_shape=jax.ShapeDtypeStruct((t, D_MODEL), F32),
        grid=(t // tm, nj),
        in_specs=[
            pl.BlockSpec((tm, D_MODEL), lambda i, j: (i, 0)),
            pl.BlockSpec((1, D_MODEL), lambda i, j: (0, 0)),
            pl.BlockSpec((D_MODEL, tf), lambda i, j: (0, j)),
            pl.BlockSpec((D_MODEL, tf), lambda i, j: (0, j)),
            pl.BlockSpec((CONV_WIDTH, tf), lambda i, j: (0, j)),
            pl.BlockSpec((1, tf), lambda i, j: (0, j)),
            pl.BlockSpec((tf, D_MODEL), lambda i, j: (j, 0)),
            pl.BlockSpec((1, D_MODEL), lambda i, j: (0, 0)),
        ],
        out_specs=pl.BlockSpec((tm, D_MODEL), lambda i, j: (i, 0)),
        scratch_shapes=[pltpu.VMEM((tm, D_MODEL), BF16), pltpu.VMEM((nj, SUBLANES, tf), F32)],
        compiler_params=pltpu.CompilerParams(
            dimension_semantics=("arbitrary", "arbitrary"),
            vmem_limit_bytes=VMEM_LIMIT),
        name="convffn",
    )(h, gf, wg, wu, cw, cb, wd, gl)


def kernel(x, positions, attn_norm, w_in, w_a_up, b_a_up, gla_norm, sinks, w_out, ffn_norm,
           w_gate, w_up, conv_w, conv_b, w_down, final_norm):
    batch, seq, _ = x.shape
    t = batch * seq
    assert w_in.shape[0] == 1, "the final norm is fused into the single layer's FFN kernel"
    x2 = x.reshape(t, D_MODEL)
    pos3 = positions.reshape(t // WINDOW, 1, WINDOW)
    half = SWA_HEAD_DIM // 2
    freq_col = (ROPE_THETA ** (-jnp.arange(half, dtype=F32) / half)).reshape(half, 1)

    wl = w_in[0]
    o_ga = 2 * GLA_QK_W + 2 * GLA_V_W
    o_sq = o_ga + GLA_RANK

    def pair_layout(w):
        n = w.shape[1] // LANES
        return w.reshape(D_MODEL, n, 2, 2, half).transpose(0, 1, 3, 2, 4).reshape(D_MODEL, n * LANES)

    w_proj = jnp.concatenate(
        [wl[:, :o_ga], pair_layout(wl[:, o_sq:o_sq + SWA_Q_W]),
         pair_layout(wl[:, o_sq + SWA_Q_W:o_sq + SWA_Q_W + SWA_KV_W]), wl[:, o_sq + SWA_Q_W + SWA_KV_W:],
         wl[:, o_ga:o_sq], jnp.zeros((D_MODEL, PROJ_W - COL_GA - GLA_RANK), F32)], axis=1).astype(BF16)
    wa = jnp.concatenate(
        [w_a_up[0], jnp.zeros((LANES - GLA_RANK, GLA_QK_W), F32)], axis=0).astype(BF16)

    proj = _inproj(x2, attn_norm[0].reshape(1, D_MODEL), w_proj, tm=1024, rc=256)
    o_gla = _gla(proj, wa, b_a_up[0].reshape(1, GLA_QK_W), gla_norm[0].reshape(1, GLA_DV),
                 batch, seq, ts=512)
    o_swa = _swa(proj, sinks[0], pos3, freq_col, batch, seq)
    wo = w_out[0].astype(BF16)
    h = _outproj(x2, o_gla, o_swa, wo[:GLA_V_W], wo[GLA_V_W:], tm=512)
    y = _ffn(h, ffn_norm[0].reshape(1, D_MODEL), w_gate[0].astype(BF16), w_up[0].astype(BF16),
             conv_w[0], conv_b[0].reshape(1, D_FF), w_down[0].astype(BF16),
             final_norm.reshape(1, D_MODEL), seq, tm=1024, tf=512, rc=256)
    return y.reshape(batch, seq, D_MODEL)
```

```python
import functools

import jax
import jax.numpy as jnp
from jax import lax
from jax.experimental import pallas as pl
from jax.experimental.pallas import tpu as pltpu

D_MODEL = 2048
GLA_HEADS = 4
GLA_DK = 128
GLA_DV = 256
GLA_RANK = 16
GLA_GATE_NORM = 16.0
GLA_CHUNK = 64
SWA_HEADS = 16
SWA_KV_HEADS = 2
SWA_HEAD_DIM = 64
WINDOW = 128
ROPE_THETA = 10000.0
D_FF = 5632
CONV_WIDTH = 3
EPS = 1e-6
NEG_INF = -1e30
LOG2E = 1.4426950408889634

GLA_QK_W = GLA_HEADS * GLA_DK
GLA_V_W = GLA_HEADS * GLA_DV
SWA_Q_W = SWA_HEADS * SWA_HEAD_DIM
SWA_KV_W = SWA_KV_HEADS * SWA_HEAD_DIM

LANES = 128
SUBLANES = 8

COL_GQ = 0
COL_GK = COL_GQ + GLA_QK_W
COL_GV = COL_GK + GLA_QK_W
COL_GG = COL_GV + GLA_V_W
COL_SQ = COL_GG + GLA_V_W
COL_SK = COL_SQ + SWA_Q_W
COL_SV = COL_SK + SWA_KV_W
COL_GA = COL_SV + SWA_KV_W
PROJ_TN = 1536
PROJ_W = 3 * PROJ_TN

VMEM_LIMIT = 58 * 1024 * 1024

F32 = jnp.float32
BF16 = jnp.bfloat16

_NT = (((1,), (1,)), ((), ()))
_TN = (((0,), (0,)), ((), ()))


def _rms(x, gain):
    return x * lax.rsqrt(jnp.mean(x * x, axis=-1, keepdims=True) + EPS) * gain


def _inproj_kernel(x_ref, g_ref, w_ref, o_ref, u_ref, *, tm, rc):
    j = pl.program_id(1)

    @pl.when(j == 0)
    def _():
        for r in range(tm // rc):
            rs = pl.ds(r * rc, rc)
            u = _rms(x_ref[rs, :], g_ref[...]).astype(BF16)
            u_ref[rs, :] = u
            o_ref[rs, :] = jnp.dot(u, w_ref[...], preferred_element_type=F32).astype(BF16)

    @pl.when(j > 0)
    def _():
        o_ref[...] = jnp.dot(u_ref[...], w_ref[...], preferred_element_type=F32).astype(BF16)


def _inproj(x2, gain, w, tm, rc):
    t = x2.shape[0]
    return pl.pallas_call(
        functools.partial(_inproj_kernel, tm=tm, rc=rc),
        out_shape=jax.ShapeDtypeStruct((t, PROJ_W), BF16),
        grid=(t // tm, PROJ_W // PROJ_TN),
        in_specs=[
            pl.BlockSpec((tm, D_MODEL), lambda i, j: (i, 0)),
            pl.BlockSpec((1, D_MODEL), lambda i, j: (0, 0)),
            pl.BlockSpec((D_MODEL, PROJ_TN), lambda i, j: (0, j)),
        ],
        out_specs=pl.BlockSpec((tm, PROJ_TN), lambda i, j: (i, j)),
        scratch_shapes=[pltpu.VMEM((tm, D_MODEL), BF16)],
        compiler_params=pltpu.CompilerParams(
            dimension_semantics=("arbitrary", "arbitrary"),
            vmem_limit_bytes=VMEM_LIMIT),
        name="inproj",
    )(x2, gain, w)


def _gla_kernel(q_ref, k_ref, v_ref, g_ref, a_ref, wa_ref, ba_ref, gn_ref, tril_ref, mask_ref,
                o_ref, s_ref, *, ts):
    c = GLA_CHUNK
    nc = ts // c

    @pl.when(pl.program_id(1) == 0)
    def _():
        s_ref[...] = jnp.zeros_like(s_ref)

    z = jnp.dot(a_ref[...], wa_ref[...], preferred_element_type=F32) + ba_ref[...]
    log_a = (jnp.minimum(z, 0.0) - jnp.log1p(jnp.exp(-jnp.abs(z)))) / GLA_GATE_NORM

    la_hi = log_a.astype(BF16)
    la_lo = (log_a - la_hi.astype(F32)).astype(BF16)
    cum2 = jnp.dot(tril_ref[...], jnp.concatenate([la_hi, la_lo], axis=1), preferred_element_type=F32)
    bcum = cum2[:, :GLA_QK_W] + cum2[:, GLA_QK_W:]
    bcum3 = bcum.reshape(nc, c, GLA_QK_W)
    b_last = bcum3[:, c - 1:c, :]

    q = q_ref[...].astype(F32) * (GLA_DK ** -0.5)
    k = k_ref[...].astype(F32)
    q_e = (q * jnp.exp(bcum)).astype(BF16)
    k_e = (k * jnp.exp(-bcum)).astype(BF16)
    k_d = (k * jnp.exp(b_last - bcum3).reshape(ts, GLA_QK_W)).astype(BF16)
    decay = jnp.exp(b_last)
    keep = mask_ref[...] != 0.0

    for h in range(GLA_HEADS):
        ks = slice(h * GLA_DK, (h + 1) * GLA_DK)
        vs = slice(h * GLA_DV, (h + 1) * GLA_DV)
        v = v_ref[:, vs]
        scores = lax.dot_general(q_e[:, ks], k_e[:, ks], _NT, preferred_element_type=F32)
        scores = jnp.where(keep, scores, 0.0).astype(BF16)
        o_intra = jnp.dot(scores, v, preferred_element_type=F32)

        state_t = s_ref[h]
        o_inter = []
        for n in range(nc):
            rows = slice(n * c, (n + 1) * c)
            o_inter.append(lax.dot_general(q_e[rows, ks], state_t.astype(BF16), _NT,
                                           preferred_element_type=F32))
            kv_t = lax.dot_general(v[rows], k_d[rows, ks], _TN, preferred_element_type=F32)
            state_t = state_t * decay[n][:, ks] + kv_t
        s_ref[h] = state_t

        o = _rms(o_intra + jnp.concatenate(o_inter, axis=0), gn_ref[...])
        gate = g_ref[:, vs].astype(F32)
        o_ref[:, vs] = (o * (gate * jax.nn.sigmoid(gate))).astype(BF16)


def _gla(proj, wa, ba, gn, batch, seq, ts):
    nt = seq // ts
    kern = functools.partial(_gla_kernel, ts=ts)
    rows = lambda b, i: b * nt + i
    idx = jnp.arange(ts)
    causal = (idx[:, None] >= idx[None, :]) & (idx[:, None] // GLA_CHUNK == idx[None, :] // GLA_CHUNK)
    return pl.pallas_call(
        kern,
        out_shape=jax.ShapeDtypeStruct((batch * seq, GLA_V_W), BF16),
        grid=(batch, nt),
        in_specs=[
            pl.BlockSpec((ts, GLA_QK_W), lambda b, i: (rows(b, i), COL_GQ // GLA_QK_W)),
            pl.BlockSpec((ts, GLA_QK_W), lambda b, i: (rows(b, i), COL_GK // GLA_QK_W)),
            pl.BlockSpec((ts, GLA_V_W), lambda b, i: (rows(b, i), COL_GV // GLA_V_W)),
            pl.BlockSpec((ts, GLA_V_W), lambda b, i: (rows(b, i), COL_GG // GLA_V_W)),
            pl.BlockSpec((ts, LANES), lambda b, i: (rows(b, i), COL_GA // LANES)),
            pl.BlockSpec((LANES, GLA_QK_W), lambda b, i: (0, 0)),
            pl.BlockSpec((1, GLA_QK_W), lambda b, i: (0, 0)),
            pl.BlockSpec((1, GLA_DV), lambda b, i: (0, 0)),
            pl.BlockSpec((ts, ts), lambda b, i: (0, 0)),
            pl.BlockSpec((ts, ts), lambda b, i: (0, 0)),
        ],
        out_specs=pl.BlockSpec((ts, GLA_V_W), lambda b, i: (rows(b, i), 0)),
        scratch_shapes=[pltpu.VMEM((GLA_HEADS, GLA_DV, GLA_DK), F32)],
        compiler_params=pltpu.CompilerParams(
            dimension_semantics=("arbitrary", "arbitrary"),
            vmem_limit_bytes=VMEM_LIMIT),
        name="gla",
    )(proj, proj, proj, proj, proj, wa, ba, gn, causal.astype(BF16), causal.astype(F32))


def _swa_kernel(sinks_ref, q_ref, k_ref, v_ref, pos_ref, posn_ref, freq_ref, o_ref,
                k2p_ref, vt2p_ref, cos_ref, sin_ref):
    nb = pl.program_id(1)
    wb = WINDOW
    hd = SWA_HEAD_DIM
    half = hd // 2
    pairs_per_kv = SWA_Q_W // LANES // SWA_KV_HEADS
    heads_per_kv = SWA_HEADS // SWA_KV_HEADS
    lane = lax.broadcasted_iota(jnp.int32, (wb, LANES), 1)
    head_a = (lane % hd) < half
    low = lane < hd

    def rotary_tables(pos_row):
        ang_t = freq_ref[...] * pos_row.astype(F32)
        reps = LANES // half
        cs = jnp.concatenate([jnp.cos(ang_t)] * reps, axis=0).T
        sn = jnp.concatenate([jnp.sin(ang_t)] * reps, axis=0).T
        return cs, jnp.where(low, -sn, sn)

    @pl.when(nb == 0)
    def _():
        k2p_ref[...] = jnp.zeros_like(k2p_ref)
        vt2p_ref[...] = jnp.zeros_like(vt2p_ref)
        cos_ref[...], sin_ref[...] = rotary_tables(pos_ref[0])

    cos = cos_ref[...]
    sin = sin_ref[...]

    def rope(t, cs, sn):
        return t * cs + pltpu.roll(t, hd, 1) * sn

    k_r = rope(k_ref[...].astype(F32), cos, sin)
    k2_cur = [jnp.where(head_a, k_r, pltpu.roll(k_r, half, 1)).astype(BF16),
              jnp.where(head_a, pltpu.roll(k_r, LANES - half, 1), k_r).astype(BF16)]
    v_f = v_ref[...].astype(F32)
    v_sw = pltpu.roll(v_f, hd, 1)
    vt2_cur = [jnp.where(low, v_f, v_sw).T.astype(BF16), jnp.where(low, v_sw, v_f).T.astype(BF16)]
    k2 = [jnp.concatenate([k2p_ref[c], k2_cur[c]], axis=0) for c in range(SWA_KV_HEADS)]
    vt2 = [jnp.concatenate([vt2p_ref[c], vt2_cur[c]], axis=1) for c in range(SWA_KV_HEADS)]

    ki = lax.broadcasted_iota(jnp.int32, (2 * wb, wb), 0)
    qi = lax.broadcasted_iota(jnp.int32, (2 * wb, wb), 1) + wb
    mask = (ki <= qi) & (qi - ki < WINDOW) & ((nb > 0) | (ki >= wb))
    row_a = lax.broadcasted_iota(jnp.int32, (LANES, wb), 0) < hd

    qscale = (hd ** -0.5) * LOG2E
    cos_q = cos * qscale
    sin_q = sin * qscale

    for c in range(SWA_KV_HEADS):
        tiles = range(c * pairs_per_kv, (c + 1) * pairs_per_kv)
        lhs = []
        for t in tiles:
            q_r = rope(q_ref[:, t * LANES:(t + 1) * LANES].astype(F32), cos_q, sin_q)
            lhs += [jnp.where(head_a, q_r, 0.0), jnp.where(head_a, 0.0, q_r)]
        lhs = jnp.concatenate(lhs, axis=0).astype(BF16)
        st = lax.dot_general(k2[c], lhs, _NT, preferred_element_type=F32)
        st = jnp.concatenate([jnp.where(mask, st[:, i * wb:(i + 1) * wb], NEG_INF) for i in range(heads_per_kv)],
                             axis=1)
        m = jnp.max(st, axis=0, keepdims=True)
        e = jnp.exp2(st - m)
        sink = jnp.concatenate([jnp.full((1, wb), sinks_ref[2 * t + i], F32) for t in tiles for i in range(2)],
                               axis=1) * LOG2E
        rden = 1.0 / (jnp.sum(e, axis=0, keepdims=True) + jnp.exp2(sink - m))
        ot = jnp.dot(vt2[c], e.astype(BF16), preferred_element_type=F32) * rden
        for n, t in enumerate(tiles):
            pair = jnp.where(row_a, ot[:, 2 * n * wb:(2 * n + 1) * wb], ot[:, (2 * n + 1) * wb:(2 * n + 2) * wb])
            o_ref[:, t * LANES:(t + 1) * LANES] = pair.T.astype(BF16)

    for c in range(SWA_KV_HEADS):
        k2p_ref[c] = k2_cur[c]
        vt2p_ref[c] = vt2_cur[c]
    cos_ref[...], sin_ref[...] = rotary_tables(posn_ref[0])


def _swa(proj, sinks, pos3, freq_col, batch, seq):
    nb = seq // WINDOW
    rows = lambda b, n: b * nb + n
    return pl.pallas_call(
        _swa_kernel,
        out_shape=jax.ShapeDtypeStruct((batch * seq, SWA_Q_W), BF16),
        grid=(batch, nb),
        in_specs=[
            pl.BlockSpec(memory_space=pltpu.SMEM),
            pl.BlockSpec((WINDOW, SWA_Q_W), lambda b, n: (rows(b, n), COL_SQ // SWA_Q_W)),
            pl.BlockSpec((WINDOW, SWA_KV_W), lambda b, n: (rows(b, n), COL_SK // SWA_KV_W)),
            pl.BlockSpec((WINDOW, SWA_KV_W), lambda b, n: (rows(b, n), COL_SV // SWA_KV_W)),
            pl.BlockSpec((1, 1, WINDOW), lambda b, n: (rows(b, n), 0, 0)),
            pl.BlockSpec((1, 1, WINDOW), lambda b, n: (rows(b, jnp.minimum(n + 1, nb - 1)), 0, 0)),
            pl.BlockSpec((SWA_HEAD_DIM // 2, 1), lambda b, n: (0, 0)),
        ],
        out_specs=pl.BlockSpec((WINDOW, SWA_Q_W), lambda b, n: (rows(b, n), 0)),
        scratch_shapes=[pltpu.VMEM((SWA_KV_HEADS, WINDOW, LANES), BF16),
                        pltpu.VMEM((SWA_KV_HEADS, LANES, WINDOW), BF16),
                        pltpu.VMEM((WINDOW, LANES), F32), pltpu.VMEM((WINDOW, LANES), F32)],
        compiler_params=pltpu.CompilerParams(
            dimension_semantics=("arbitrary", "arbitrary"),
            vmem_limit_bytes=VMEM_LIMIT),
        name="swa",
    )(sinks, proj, proj, proj, pos3, pos3, freq_col)


def _outproj_kernel(x_ref, a_ref, b_ref, wa_ref, wb_ref, h_ref):
    acc = jnp.dot(a_ref[...], wa_ref[...], preferred_element_type=F32)
    acc = acc + jnp.dot(b_ref[...], wb_ref[...], preferred_element_type=F32)
    h_ref[...] = x_ref[...] + acc


def _outproj(x2, o_gla, o_swa, w_a, w_b, tm):
    t = x2.shape[0]
    return pl.pallas_call(
        _outproj_kernel,
        out_shape=jax.ShapeDtypeStruct((t, D_MODEL), F32),
        grid=(t // tm,),
        in_specs=[
            pl.BlockSpec((tm, D_MODEL), lambda i: (i, 0)),
            pl.BlockSpec((tm, GLA_V_W), lambda i: (i, 0)),
            pl.BlockSpec((tm, SWA_Q_W), lambda i: (i, 0)),
            pl.BlockSpec((GLA_V_W, D_MODEL), lambda i: (0, 0)),
            pl.BlockSpec((SWA_Q_W, D_MODEL), lambda i: (0, 0)),
        ],
        out_specs=pl.BlockSpec((tm, D_MODEL), lambda i: (i, 0)),
        compiler_params=pltpu.CompilerParams(
            dimension_semantics=("arbitrary",),
            vmem_limit_bytes=VMEM_LIMIT),
        name="outproj",
    )(x2, o_gla, o_swa, w_a, w_b)


def _ffn_kernel(h_ref, gf_ref, wg_ref, wu_ref, cw_ref, cb_ref, wd_ref, gl_ref, o_ref,
                hn_ref, carry_ref, *, tm, rc, seq):
    i = pl.program_id(0)
    j = pl.program_id(1)
    nj = pl.num_programs(1)

    def step(first, last):
        seq_start = (i * tm) % seq == 0
        prev = jnp.where(seq_start, 0.0, carry_ref[j])
        rows = lax.broadcasted_iota(jnp.int32, prev.shape, 0)
        cb = cb_ref[...]
        cw0, cw1, cw2 = cw_ref[0:1, :], cw_ref[1:2, :], cw_ref[2:3, :]

        for r in range(tm // rc):
            rs = pl.ds(r * rc, rc)
            if first:
                res = h_ref[rs, :]
                hn = _rms(res, gf_ref[...]).astype(BF16)
                hn_ref[rs, :] = hn
            else:
                res = o_ref[rs, :]
                hn = hn_ref[rs, :]
            gate = jnp.dot(hn, wg_ref[...], preferred_element_type=F32)
            up = jnp.dot(hn, wu_ref[...], preferred_element_type=F32)

            def shifted(d):
                rolled = pltpu.roll(gate, d, 0)
                top = jnp.where(rows < d, pltpu.roll(prev, d, 0), rolled[:SUBLANES])
                return jnp.concatenate([top, rolled[SUBLANES:]], axis=0)

            conv = cb + cw0 * shifted(2) + cw1 * shifted(1) + cw2 * gate
            act = (conv * jax.nn.sigmoid(conv) * up).astype(BF16)
            acc = res + jnp.dot(act, wd_ref[...], preferred_element_type=F32)
            o_ref[rs, :] = _rms(acc, gl_ref[...]) if last else acc
            prev = gate[rc - SUBLANES:, :]
        carry_ref[j] = prev

    pl.when(j == 0)(functools.partial(step, True, False))
    pl.when((j > 0) & (j < nj - 1))(functools.partial(step, False, False))
    pl.when(j == nj - 1)(functools.partial(step, False, True))


def _ffn(h, gf, wg, wu, cw, cb, wd, gl, seq, tm, tf, rc):
    t = h.shape[0]
    nj = D_FF // tf
    kern = functools.partial(_ffn_kernel, tm=tm, rc=rc, seq=seq)
    return pl.pallas_call(
        kern,
        out_shape=jax.ShapeDtypeStruct((t, D_MODEL), F32),
        grid=(t // tm, nj),
        in_specs=[
            pl.BlockSpec((tm, D_MODEL), lambda i, j: (i, 0)),
            pl.BlockSpec((1, D_MODEL), lambda i, j: (0, 0)),
            pl.BlockSpec((D_MODEL, tf), lambda i, j: (0, j)),
            pl.BlockSpec((D_MODEL, tf), lambda i, j: (0, j)),
            pl.BlockSpec((CONV_WIDTH, tf), lambda i, j: (0, j)),
            pl.BlockSpec((1, tf), lambda i, j: (0, j)),
            pl.BlockSpec((tf, D_MODEL), lambda i, j: (j, 0)),
            pl.BlockSpec((1, D_MODEL), lambda i, j: (0, 0)),
        ],
        out_specs=pl.BlockSpec((tm, D_MODEL), lambda i, j: (i, 0)),
        scratch_shapes=[pltpu.VMEM((tm, D_MODEL), BF16), pltpu.VMEM((nj, SUBLANES, tf), F32)],
        compiler_params=pltpu.CompilerParams(
            dimension_semantics=("arbitrary", "arbitrary"),
            vmem_limit_bytes=VMEM_LIMIT),
        name="convffn",
    )(h, gf, wg, wu, cw, cb, wd, gl)


def kernel(x, positions, attn_norm, w_in, w_a_up, b_a_up, gla_norm, sinks, w_out, ffn_norm,
           w_gate, w_up, conv_w, conv_b, w_down, final_norm):
    batch, seq, _ = x.shape
    t = batch * seq
    assert w_in.shape[0] == 1, "the final norm is fused into the single layer's FFN kernel"
    x2 = x.reshape(t, D_MODEL)
    pos3 = positions.reshape(t // WINDOW, 1, WINDOW)
    half = SWA_HEAD_DIM // 2
    freq_col = (ROPE_THETA ** (-jnp.arange(half, dtype=F32) / half)).reshape(half, 1)

    wl = w_in[0]
    o_ga = 2 * GLA_QK_W + 2 * GLA_V_W
    o_sq = o_ga + GLA_RANK

    def pair_layout(w):
        n = w.shape[1] // LANES
        return w.reshape(D_MODEL, n, 2, 2, half).transpose(0, 1, 3, 2, 4).reshape(D_MODEL, n * LANES)

    w_proj = jnp.concatenate(
        [wl[:, :o_ga], pair_layout(wl[:, o_sq:o_sq + SWA_Q_W]),
         pair_layout(wl[:, o_sq + SWA_Q_W:o_sq + SWA_Q_W + SWA_KV_W]), wl[:, o_sq + SWA_Q_W + SWA_KV_W:],
         wl[:, o_ga:o_sq], jnp.zeros((D_MODEL, PROJ_W - COL_GA - GLA_RANK), F32)], axis=1).astype(BF16)
    wa = jnp.concatenate(
        [w_a_up[0], jnp.zeros((LANES - GLA_RANK, GLA_QK_W), F32)], axis=0).astype(BF16)

    proj = _inproj(x2, attn_norm[0].reshape(1, D_MODEL), w_proj, tm=1024, rc=256)
    o_gla = _gla(proj, wa, b_a_up[0].reshape(1, GLA_QK_W), gla_norm[0].reshape(1, GLA_DV),
                 batch, seq, ts=512)
    o_swa = _swa(proj, sinks[0], pos3, freq_col, batch, seq)
    wo = w_out[0].astype(BF16)
    h = _outproj(x2, o_gla, o_swa, wo[:GLA_V_W], wo[GLA_V_W:], tm=512)
    y = _ffn(h, ffn_norm[0].reshape(1, D_MODEL), w_gate[0].astype(BF16), w_up[0].astype(BF16),
             conv_w[0], conv_b[0].reshape(1, D_FF), w_down[0].astype(BF16),
             final_norm.reshape(1, D_MODEL), seq, tm=1024, tf=512, rc=256)
    return y.reshape(batch, seq, D_MODEL)
```

```python
import functools

import jax
import jax.numpy as jnp
from jax import lax
from jax.experimental import pallas as pl
from jax.experimental.pallas import tpu as pltpu

D_MODEL = 2048
GLA_HEADS = 4
GLA_DK = 128
GLA_DV = 256
GLA_RANK = 16
GLA_GATE_NORM = 16.0
GLA_CHUNK = 64
SWA_HEADS = 16
SWA_KV_HEADS = 2
SWA_HEAD_DIM = 64
WINDOW = 128
ROPE_THETA = 10000.0
D_FF = 5632
CONV_WIDTH = 3
EPS = 1e-6
NEG_INF = -1e30
LOG2E = 1.4426950408889634

GLA_QK_W = GLA_HEADS * GLA_DK
GLA_V_W = GLA_HEADS * GLA_DV
SWA_Q_W = SWA_HEADS * SWA_HEAD_DIM
SWA_KV_W = SWA_KV_HEADS * SWA_HEAD_DIM

LANES = 128
SUBLANES = 8

COL_GQ = 0
COL_GK = COL_GQ + GLA_QK_W
COL_GV = COL_GK + GLA_QK_W
COL_GG = COL_GV + GLA_V_W
COL_SQ = COL_GG + GLA_V_W
COL_SK = COL_SQ + SWA_Q_W
COL_SV = COL_SK + SWA_KV_W
COL_GA = COL_SV + SWA_KV_W
PROJ_TN = 1536
PROJ_W = 3 * PROJ_TN

VMEM_LIMIT = 58 * 1024 * 1024

F32 = jnp.float32
BF16 = jnp.bfloat16

_NT = (((1,), (1,)), ((), ()))
_TN = (((0,), (0,)), ((), ()))


def _rms(x, gain):
    return x * lax.rsqrt(jnp.mean(x * x, axis=-1, keepdims=True) + EPS) * gain


def _wprep_kernel(w_ref, o_ref):
    tr = w_ref.shape[0]
    half = SWA_HEAD_DIM // 2
    o_ga = 2 * GLA_QK_W + 2 * GLA_V_W
    o_sq = o_ga + GLA_RANK
    lane = lax.broadcasted_iota(jnp.int32, (tr, LANES), 1)
    to_b_first = (lane >= half) & (lane < 2 * half)
    to_a_second = (lane >= 2 * half) & (lane < 3 * half)

    def pair_layout(tile):
        return jnp.where(to_b_first, pltpu.roll(tile, LANES - half, 1),
                         jnp.where(to_a_second, pltpu.roll(tile, half, 1), tile))

    o_ref[:, :o_ga] = w_ref[:, :o_ga].astype(BF16)
    rest = w_ref[:, o_sq:]
    n_rot = (SWA_Q_W + SWA_KV_W) // LANES
    for i in range(n_rot):
        o_ref[:, COL_SQ + i * LANES:COL_SQ + (i + 1) * LANES] = pair_layout(
            rest[:, i * LANES:(i + 1) * LANES]).astype(BF16)
    o_ref[:, COL_SV:COL_GA] = rest[:, n_rot * LANES:].astype(BF16)
    ga = w_ref[:, o_ga:o_sq]
    o_ref[:, COL_GA:COL_GA + LANES] = jnp.concatenate(
        [ga, jnp.zeros((tr, LANES - GLA_RANK), F32)], axis=1).astype(BF16)
    o_ref[:, COL_GA + LANES:] = jnp.zeros((tr, PROJ_W - COL_GA - LANES), BF16)


def _wprep(w, tr):
    k, n = w.shape
    return pl.pallas_call(
        _wprep_kernel,
        out_shape=jax.ShapeDtypeStruct((k, PROJ_W), BF16),
        grid=(k // tr,),
        in_specs=[pl.BlockSpec((tr, n), lambda i: (i, 0))],
        out_specs=pl.BlockSpec((tr, PROJ_W), lambda i: (i, 0)),
        compiler_params=pltpu.CompilerParams(
            dimension_semantics=("arbitrary",),
            vmem_limit_bytes=VMEM_LIMIT),
        name="wprep",
    )(w)


def _inproj_kernel(x_ref, g_ref, w_ref, o_ref, u_ref, *, tm, rc):
    j = pl.program_id(1)

    @pl.when(j == 0)
    def _():
        for r in range(tm // rc):
            rs = pl.ds(r * rc, rc)
            u = _rms(x_ref[rs, :], g_ref[...]).astype(BF16)
            u_ref[rs, :] = u
            o_ref[rs, :] = jnp.dot(u, w_ref[...], preferred_element_type=F32).astype(BF16)

    @pl.when(j > 0)
    def _():
        o_ref[...] = jnp.dot(u_ref[...], w_ref[...], preferred_element_type=F32).astype(BF16)


def _inproj(x2, gain, w, tm, rc):
    t = x2.shape[0]
    return pl.pallas_call(
        functools.partial(_inproj_kernel, tm=tm, rc=rc),
        out_shape=jax.ShapeDtypeStruct((t, PROJ_W), BF16),
        grid=(t // tm, PROJ_W // PROJ_TN),
        in_specs=[
            pl.BlockSpec((tm, D_MODEL), lambda i, j: (i, 0)),
            pl.BlockSpec((1, D_MODEL), lambda i, j: (0, 0)),
            pl.BlockSpec((D_MODEL, PROJ_TN), lambda i, j: (0, j)),
        ],
        out_specs=pl.BlockSpec((tm, PROJ_TN), lambda i, j: (i, j)),
        scratch_shapes=[pltpu.VMEM((tm, D_MODEL), BF16)],
        compiler_params=pltpu.CompilerParams(
            dimension_semantics=("arbitrary", "arbitrary"),
            vmem_limit_bytes=VMEM_LIMIT),
        name="inproj",
    )(x2, gain, w)


def _gla_kernel(q_ref, k_ref, v_ref, g_ref, a_ref, wa_ref, ba_ref, gn_ref, tril_ref, mask_ref,
                o_ref, s_ref, *, ts):
    c = GLA_CHUNK
    nc = ts // c
    heads = range(GLA_HEADS)
    ks = [slice(h * GLA_DK, (h + 1) * GLA_DK) for h in heads]
    vs = [slice(h * GLA_DV, (h + 1) * GLA_DV) for h in heads]

    @pl.when(pl.program_id(1) == 0)
    def _():
        s_ref[...] = jnp.zeros_like(s_ref)

    z = jnp.dot(a_ref[...], wa_ref[...], preferred_element_type=F32) + ba_ref[...]
    log_a = (jnp.minimum(z, 0.0) - jnp.log(1.0 + jnp.exp(-jnp.abs(z)))) * (1.0 / GLA_GATE_NORM)

    la_hi = log_a.astype(BF16)
    la_lo = (log_a - la_hi.astype(F32)).astype(BF16)
    cum2 = jnp.dot(tril_ref[...], jnp.concatenate([la_hi, la_lo], axis=1), preferred_element_type=F32)
    bcum = cum2[:, :GLA_QK_W] + cum2[:, GLA_QK_W:]
    bcum3 = bcum.reshape(nc, c, GLA_QK_W)
    b_last = bcum3[:, c - 1:c, :]

    q = q_ref[...].astype(F32) * (GLA_DK ** -0.5)
    k = k_ref[...].astype(F32)
    q_e = (q * jnp.exp(bcum)).astype(BF16)
    k_e = (k * jnp.exp(-bcum)).astype(BF16)
    k_d = (k * jnp.exp(b_last - bcum3).reshape(ts, GLA_QK_W)).astype(BF16)
    decay = jnp.exp(b_last)
    v = [v_ref[:, vs[h]] for h in heads]

    keep = mask_ref[...] != 0.0

    for h in heads:
        scores = lax.dot_general(q_e[:, ks[h]], k_e[:, ks[h]], _NT, preferred_element_type=F32)
        scores = jnp.where(keep, scores, 0.0).astype(BF16)
        o_intra = jnp.dot(scores, v[h], preferred_element_type=F32)

        state_t = s_ref[h]
        o_inter = []
        for n in range(nc):
            rows = slice(n * c, (n + 1) * c)
            o_inter.append(lax.dot_general(q_e[rows, ks[h]], state_t.astype(BF16), _NT,
                                           preferred_element_type=F32))
            kv_t = lax.dot_general(v[h][rows], k_d[rows, ks[h]], _TN, preferred_element_type=F32)
            state_t = state_t * decay[n][:, ks[h]] + kv_t
        s_ref[h] = state_t

        o = _rms(o_intra + jnp.concatenate(o_inter, axis=0), gn_ref[...])
        gate = g_ref[:, vs[h]].astype(F32)
        o_ref[:, vs[h]] = (o * (gate * jax.nn.sigmoid(gate))).astype(BF16)


def _gla(proj, wa, ba, gn, batch, seq, ts):
    nt = seq // ts
    kern = functools.partial(_gla_kernel, ts=ts)
    rows = lambda b, i: b * nt + i
    idx = jnp.arange(ts)
    causal = (idx[:, None] >= idx[None, :]) & (idx[:, None] // GLA_CHUNK == idx[None, :] // GLA_CHUNK)
    return pl.pallas_call(
        kern,
        out_shape=jax.ShapeDtypeStruct((batch * seq, GLA_V_W), BF16),
        grid=(batch, nt),
        in_specs=[
            pl.BlockSpec((ts, GLA_QK_W), lambda b, i: (rows(b, i), COL_GQ // GLA_QK_W)),
            pl.BlockSpec((ts, GLA_QK_W), lambda b, i: (rows(b, i), COL_GK // GLA_QK_W)),
            pl.BlockSpec((ts, GLA_V_W), lambda b, i: (rows(b, i), COL_GV // GLA_V_W)),
            pl.BlockSpec((ts, GLA_V_W), lambda b, i: (rows(b, i), COL_GG // GLA_V_W)),
            pl.BlockSpec((ts, LANES), lambda b, i: (rows(b, i), COL_GA // LANES)),
            pl.BlockSpec((LANES, GLA_QK_W), lambda b, i: (0, 0)),
            pl.BlockSpec((1, GLA_QK_W), lambda b, i: (0, 0)),
            pl.BlockSpec((1, GLA_DV), lambda b, i: (0, 0)),
            pl.BlockSpec((ts, ts), lambda b, i: (0, 0)),
            pl.BlockSpec((ts, ts), lambda b, i: (0, 0)),
        ],
        out_specs=pl.BlockSpec((ts, GLA_V_W), lambda b, i: (rows(b, i), 0)),
        scratch_shapes=[pltpu.VMEM((GLA_HEADS, GLA_DV, GLA_DK), F32)],
        compiler_params=pltpu.CompilerParams(
            dimension_semantics=("arbitrary", "arbitrary"),
            vmem_limit_bytes=VMEM_LIMIT),
        name="gla",
    )(proj, proj, proj, proj, proj, wa, ba, gn, causal.astype(BF16), causal.astype(F32))


def _swa_kernel(sinks_ref, q_ref, k_ref, v_ref, pos_ref, posn_ref, freq_ref, o_ref,
                k2p_ref, vt2p_ref, cos_ref, sin_ref):
    nb = pl.program_id(1)
    wb = WINDOW
    hd = SWA_HEAD_DIM
    half = hd // 2
    pairs_per_kv = SWA_Q_W // LANES // SWA_KV_HEADS
    heads_per_kv = SWA_HEADS // SWA_KV_HEADS
    lane = lax.broadcasted_iota(jnp.int32, (wb, LANES), 1)
    head_a = (lane % hd) < half
    low = lane < hd

    def rotary_tables(pos_row):
        ang_t = freq_ref[...] * pos_row.astype(F32)
        reps = LANES // half
        cs = jnp.concatenate([jnp.cos(ang_t)] * reps, axis=0).T
        sn = jnp.concatenate([jnp.sin(ang_t)] * reps, axis=0).T
        return cs, jnp.where(low, -sn, sn)

    @pl.when(nb == 0)
    def _():
        k2p_ref[...] = jnp.zeros_like(k2p_ref)
        vt2p_ref[...] = jnp.zeros_like(vt2p_ref)
        cos_ref[...], sin_ref[...] = rotary_tables(pos_ref[0])

    cos = cos_ref[...]
    sin = sin_ref[...]

    def rope(t, cs, sn):
        return t * cs + pltpu.roll(t, hd, 1) * sn

    k_r = rope(k_ref[...].astype(F32), cos, sin)
    k2_cur = [jnp.where(head_a, k_r, pltpu.roll(k_r, half, 1)).astype(BF16),
              jnp.where(head_a, pltpu.roll(k_r, LANES - half, 1), k_r).astype(BF16)]
    v_f = v_ref[...].astype(F32)
    v_sw = pltpu.roll(v_f, hd, 1)
    vt2_cur = [jnp.where(low, v_f, v_sw).T.astype(BF16), jnp.where(low, v_sw, v_f).T.astype(BF16)]
    k2 = [jnp.concatenate([k2p_ref[c], k2_cur[c]], axis=0) for c in range(SWA_KV_HEADS)]
    vt2 = [jnp.concatenate([vt2p_ref[c], vt2_cur[c]], axis=1) for c in range(SWA_KV_HEADS)]

    ki = lax.broadcasted_iota(jnp.int32, (2 * wb, wb), 0)
    qi = lax.broadcasted_iota(jnp.int32, (2 * wb, wb), 1) + wb
    mask = (ki <= qi) & (qi - ki < WINDOW) & ((nb > 0) | (ki >= wb))
    row_a = lax.broadcasted_iota(jnp.int32, (LANES, wb), 0) < hd

    qscale = (hd ** -0.5) * LOG2E
    cos_q = cos * qscale
    sin_q = sin * qscale

    for c in range(SWA_KV_HEADS):
        tiles = range(c * pairs_per_kv, (c + 1) * pairs_per_kv)
        lhs = []
        for t in tiles:
            q_r = rope(q_ref[:, t * LANES:(t + 1) * LANES].astype(F32), cos_q, sin_q)
            lhs += [jnp.where(head_a, q_r, 0.0), jnp.where(head_a, 0.0, q_r)]
        lhs = jnp.concatenate(lhs, axis=0).astype(BF16)
        st = lax.dot_general(k2[c], lhs, _NT, preferred_element_type=F32)
        st = jnp.concatenate([jnp.where(mask, st[:, i * wb:(i + 1) * wb], NEG_INF) for i in range(heads_per_kv)],
                             axis=1)
        m = jnp.max(st, axis=0, keepdims=True)
        e = jnp.exp2(st - m)
        sink = jnp.concatenate([jnp.full((1, wb), sinks_ref[2 * t + i], F32) for t in tiles for i in range(2)],
                               axis=1) * LOG2E
        rden = 1.0 / (jnp.sum(e, axis=0, keepdims=True) + jnp.exp2(sink - m))
        ot = jnp.dot(vt2[c], e.astype(BF16), preferred_element_type=F32) * rden
        for n, t in enumerate(tiles):
            pair = jnp.where(row_a, ot[:, 2 * n * wb:(2 * n + 1) * wb], ot[:, (2 * n + 1) * wb:(2 * n + 2) * wb])
            o_ref[:, t * LANES:(t + 1) * LANES] = pair.T.astype(BF16)

    for c in range(SWA_KV_HEADS):
        k2p_ref[c] = k2_cur[c]
        vt2p_ref[c] = vt2_cur[c]
    cos_ref[...], sin_ref[...] = rotary_tables(posn_ref[0])


def _swa(proj, sinks, pos3, freq_col, batch, seq):
    nb = seq // WINDOW
    rows = lambda b, n: b * nb + n
    return pl.pallas_call(
        _swa_kernel,
        out_shape=jax.ShapeDtypeStruct((batch * seq, SWA_Q_W), BF16),
        grid=(batch, nb),
        in_specs=[
            pl.BlockSpec(memory_space=pltpu.SMEM),
            pl.BlockSpec((WINDOW, SWA_Q_W), lambda b, n: (rows(b, n), COL_SQ // SWA_Q_W)),
            pl.BlockSpec((WINDOW, SWA_KV_W), lambda b, n: (rows(b, n), COL_SK // SWA_KV_W)),
            pl.BlockSpec((WINDOW, SWA_KV_W), lambda b, n: (rows(b, n), COL_SV // SWA_KV_W)),
            pl.BlockSpec((1, 1, WINDOW), lambda b, n: (rows(b, n), 0, 0)),
            pl.BlockSpec((1, 1, WINDOW), lambda b, n: (rows(b, jnp.minimum(n + 1, nb - 1)), 0, 0)),
            pl.BlockSpec((SWA_HEAD_DIM // 2, 1), lambda b, n: (0, 0)),
        ],
        out_specs=pl.BlockSpec((WINDOW, SWA_Q_W), lambda b, n: (rows(b, n), 0)),
        scratch_shapes=[pltpu.VMEM((SWA_KV_HEADS, WINDOW, LANES), BF16),
                        pltpu.VMEM((SWA_KV_HEADS, LANES, WINDOW), BF16),
                        pltpu.VMEM((WINDOW, LANES), F32), pltpu.VMEM((WINDOW, LANES), F32)],
        compiler_params=pltpu.CompilerParams(
            dimension_semantics=("arbitrary", "arbitrary"),
            vmem_limit_bytes=VMEM_LIMIT),
        name="swa",
    )(sinks, proj, proj, proj, pos3, pos3, freq_col)


def _outproj_kernel(x_ref, a_ref, b_ref, wa_ref, wb_ref, h_ref):
    acc = jnp.dot(a_ref[...], wa_ref[...], preferred_element_type=F32)
    acc = acc + jnp.dot(b_ref[...], wb_ref[...], preferred_element_type=F32)
    h_ref[...] = x_ref[...] + acc


def _outproj(x2, o_gla, o_swa, w, tm):
    t = x2.shape[0]
    return pl.pallas_call(
        _outproj_kernel,
        out_shape=jax.ShapeDtypeStruct((t, D_MODEL), F32),
        grid=(t // tm,),
        in_specs=[
            pl.BlockSpec((tm, D_MODEL), lambda i: (i, 0)),
            pl.BlockSpec((tm, GLA_V_W), lambda i: (i, 0)),
            pl.BlockSpec((tm, SWA_Q_W), lambda i: (i, 0)),
            pl.BlockSpec((GLA_V_W, D_MODEL), lambda i: (0, 0)),
            pl.BlockSpec((SWA_Q_W, D_MODEL), lambda i: (GLA_V_W // SWA_Q_W, 0)),
        ],
        out_specs=pl.BlockSpec((tm, D_MODEL), lambda i: (i, 0)),
        compiler_params=pltpu.CompilerParams(
            dimension_semantics=("arbitrary",),
            vmem_limit_bytes=VMEM_LIMIT),
        name="outproj",
    )(x2, o_gla, o_swa, w, w)


def _ffn_kernel(h_ref, gf_ref, wg_ref, wu_ref, cw_ref, cb_ref, wd_ref, gl_ref, o_ref,
                hn_ref, carry_ref, *, tm, rc, seq):
    i = pl.program_id(0)
    j = pl.program_id(1)
    nj = pl.num_programs(1)

    def step(first, last):
        seq_start = (i * tm) % seq == 0
        prev = jnp.where(seq_start, 0.0, carry_ref[j])
        rows = lax.broadcasted_iota(jnp.int32, prev.shape, 0)
        cb = cb_ref[...]
        cw0, cw1, cw2 = cw_ref[0:1, :], cw_ref[1:2, :], cw_ref[2:3, :]

        for r in range(tm // rc):
            rs = pl.ds(r * rc, rc)
            if first:
                res = h_ref[rs, :]
                hn = _rms(res, gf_ref[...]).astype(BF16)
                hn_ref[rs, :] = hn
            else:
                res = o_ref[rs, :]
                hn = hn_ref[rs, :]
            gate = jnp.dot(hn, wg_ref[...], preferred_element_type=F32)
            up = jnp.dot(hn, wu_ref[...], preferred_element_type=F32)

            def shifted(d):
                rolled = pltpu.roll(gate, d, 0)
                top = jnp.where(rows < d, pltpu.roll(prev, d, 0), rolled[:SUBLANES])
                return jnp.concatenate([top, rolled[SUBLANES:]], axis=0)

            conv = cb + cw0 * shifted(2) + cw1 * shifted(1) + cw2 * gate
            act = (conv * jax.nn.sigmoid(conv) * up).astype(BF16)
            acc = res + jnp.dot(act, wd_ref[...], preferred_element_type=F32)
            o_ref[rs, :] = _rms(acc, gl_ref[...]) if last else acc
            prev = gate[rc - SUBLANES:, :]
        carry_ref[j] = prev

    pl.when(j == 0)(functools.partial(step, True, False))
    pl.when((j > 0) & (j < nj - 1))(functools.partial(step, False, False))
    pl.when(j == nj - 1)(functools.partial(step, False, True))


def _ffn(h, gf, wg, wu, cw, cb, wd, gl, seq, tm, tf, rc):
    t = h.shape[0]
    nj = D_FF // tf
    kern = functools.partial(_ffn_kernel, tm=tm, rc=rc, seq=seq)
    return pl.pallas_call(
        kern,
        out_shape=jax.ShapeDtypeStruct((t, D_MODEL), F32),
        grid=(t // tm, nj),
        in_specs=[
            pl.BlockSpec((tm, D_MODEL), lambda i, j: (i, 0)),
            pl.BlockSpec((1, D_MODEL), lambda i, j: (0, 0)),
            pl.BlockSpec((D_MODEL, tf), lambda i, j: (0, j)),
            pl.BlockSpec((D_MODEL, tf), lambda i, j: (0, j)),
            pl.BlockSpec((CONV_WIDTH, tf), lambda i, j: (0, j)),
            pl.BlockSpec((1, tf), lambda i, j: (0, j)),
            pl.BlockSpec((tf, D_MODEL), lambda i, j: (j, 0)),
            pl.BlockSpec((1, D_MODEL), lambda i, j: (0, 0)),
        ],
        out_specs=pl.BlockSpec((tm, D_MODEL), lambda i, j: (i, 0)),
        scratch_shapes=[pltpu.VMEM((tm, D_MODEL), BF16), pltpu.VMEM((nj, SUBLANES, tf), F32)],
        compiler_params=pltpu.CompilerParams(
            dimension_semantics=("arbitrary", "arbitrary"),
            vmem_limit_bytes=VMEM_LIMIT),
        name="convffn",
    )(h, gf, wg, wu, cw, cb, wd, gl)


def kernel(x, positions, attn_norm, w_in, w_a_up, b_a_up, gla_norm, sinks, w_out, ffn_norm,
           w_gate, w_up, conv_w, conv_b, w_down, final_norm):
    batch, seq, _ = x.shape
    t = batch * seq
    assert w_in.shape[0] == 1, "the final norm is fused into the single layer's FFN kernel"
    x2 = x.reshape(t, D_MODEL)
    pos3 = positions.reshape(t // WINDOW, 1, WINDOW)
    half = SWA_HEAD_DIM // 2
    freq_col = (ROPE_THETA ** (-jnp.arange(half, dtype=F32) / half)).reshape(half, 1)

    w_proj = _wprep(w_in[0], tr=256)
    wa = jnp.concatenate(
        [w_a_up[0], jnp.zeros((LANES - GLA_RANK, GLA_QK_W), F32)], axis=0).astype(BF16)

    proj = _inproj(x2, attn_norm[0].reshape(1, D_MODEL), w_proj, tm=1024, rc=256)
    o_gla = _gla(proj, wa, b_a_up[0].reshape(1, GLA_QK_W), gla_norm[0].reshape(1, GLA_DV),
                 batch, seq, ts=512)
    o_swa = _swa(proj, sinks[0], pos3, freq_col, batch, seq)
    h = _outproj(x2, o_gla, o_swa, w_out[0].astype(BF16), tm=512)
    y = _ffn(h, ffn_norm[0].reshape(1, D_MODEL), w_gate[0].astype(BF16), w_up[0].astype(BF16),
             conv_w[0], conv_b[0].reshape(1, D_FF), w_down[0].astype(BF16),
             final_norm.reshape(1, D_MODEL), seq, tm=1024, tf=512, rc=256)
    return y.reshape(batch, seq, D_MODEL)
```

```python
import functools

import jax
import jax.numpy as jnp
from jax import lax
from jax.experimental import pallas as pl
from jax.experimental.pallas import tpu as pltpu

D_MODEL = 2048
GLA_HEADS = 4
GLA_DK = 128
GLA_DV = 256
GLA_RANK = 16
GLA_GATE_NORM = 16.0
GLA_CHUNK = 64
SWA_HEADS = 16
SWA_KV_HEADS = 2
SWA_HEAD_DIM = 64
WINDOW = 128
ROPE_THETA = 10000.0
D_FF = 5632
CONV_WIDTH = 3
EPS = 1e-6
NEG_INF = -1e30
LOG2E = 1.4426950408889634
SWA_VT_ROWS = SWA_HEAD_DIM + 16

GLA_QK_W = GLA_HEADS * GLA_DK
GLA_V_W = GLA_HEADS * GLA_DV
SWA_Q_W = SWA_HEADS * SWA_HEAD_DIM
SWA_KV_W = SWA_KV_HEADS * SWA_HEAD_DIM

LANES = 128
SUBLANES = 8

COL_GQ = 0
COL_GK = COL_GQ + GLA_QK_W
COL_GV = COL_GK + GLA_QK_W
COL_GG = COL_GV + GLA_V_W
COL_SQ = COL_GG + GLA_V_W
COL_SK = COL_SQ + SWA_Q_W
COL_SV = COL_SK + SWA_KV_W
COL_GA = COL_SV + SWA_KV_W
PROJ_TN = 1536
PROJ_W = 3 * PROJ_TN

VMEM_LIMIT = 58 * 1024 * 1024

F32 = jnp.float32
BF16 = jnp.bfloat16

_NT = (((1,), (1,)), ((), ()))
_TN = (((0,), (0,)), ((), ()))


def _rms(x, gain):
    return x * lax.rsqrt(jnp.mean(x * x, axis=-1, keepdims=True) + EPS) * gain


def _wprep_kernel(wt_ref, o_ref):
    tk = wt_ref.shape[1]
    half = SWA_HEAD_DIM // 2
    o_ga = 2 * GLA_QK_W + 2 * GLA_V_W
    o_sq = o_ga + GLA_RANK

    def put(col, rows):
        o_ref[:, col:col + rows.shape[0]] = rows.T.astype(BF16)

    put(0, wt_ref[:o_ga, :])
    for i in range((SWA_Q_W + SWA_KV_W) // LANES):
        r0 = o_sq + i * LANES
        blk = [wt_ref[r0 + j * half:r0 + (j + 1) * half, :] for j in range(4)]
        put(COL_SQ + i * LANES, jnp.concatenate([blk[0], blk[2], blk[1], blk[3]], axis=0))
    put(COL_SV, wt_ref[o_sq + SWA_Q_W + SWA_KV_W:, :])
    keep = lax.broadcasted_iota(jnp.int32, (LANES, tk), 0) < GLA_RANK
    put(COL_GA, jnp.where(keep, wt_ref[o_ga:o_ga + LANES, :], 0.0))
    o_ref[:, COL_GA + LANES:] = jnp.zeros((tk, PROJ_W - COL_GA - LANES), BF16)


def _wprep(wt, tk):
    n, k = wt.shape
    return pl.pallas_call(
        _wprep_kernel,
        out_shape=jax.ShapeDtypeStruct((k, PROJ_W), BF16),
        grid=(k // tk,),
        in_specs=[pl.BlockSpec((n, tk), lambda i: (0, i))],
        out_specs=pl.BlockSpec((tk, PROJ_W), lambda i: (i, 0)),
        compiler_params=pltpu.CompilerParams(
            dimension_semantics=("arbitrary",),
            vmem_limit_bytes=VMEM_LIMIT),
        name="wprep",
    )(wt)


def _inproj_kernel(x_ref, g_ref, w_ref, o_ref, u_ref, *, tm, rc):
    j = pl.program_id(1)

    @pl.when(j == 0)
    def _():
        for r in range(tm // rc):
            rs = pl.ds(r * rc, rc)
            u = _rms(x_ref[rs, :], g_ref[...]).astype(BF16)
            u_ref[rs, :] = u
            o_ref[rs, :] = jnp.dot(u, w_ref[...], preferred_element_type=F32).astype(BF16)

    @pl.when(j > 0)
    def _():
        o_ref[...] = jnp.dot(u_ref[...], w_ref[...], preferred_element_type=F32).astype(BF16)


def _inproj(x2, gain, w, tm, rc):
    t = x2.shape[0]
    return pl.pallas_call(
        functools.partial(_inproj_kernel, tm=tm, rc=rc),
        out_shape=jax.ShapeDtypeStruct((t, PROJ_W), BF16),
        grid=(t // tm, PROJ_W // PROJ_TN),
        in_specs=[
            pl.BlockSpec((tm, D_MODEL), lambda i, j: (i, 0)),
            pl.BlockSpec((1, D_MODEL), lambda i, j: (0, 0)),
            pl.BlockSpec((D_MODEL, PROJ_TN), lambda i, j: (0, j)),
        ],
        out_specs=pl.BlockSpec((tm, PROJ_TN), lambda i, j: (i, j)),
        scratch_shapes=[pltpu.VMEM((tm, D_MODEL), BF16)],
        compiler_params=pltpu.CompilerParams(
            dimension_semantics=("arbitrary", "arbitrary"),
            vmem_limit_bytes=VMEM_LIMIT),
        name="inproj",
    )(x2, gain, w)


def _gla_kernel(q_ref, k_ref, v_ref, g_ref, a_ref, wa_ref, ba_ref, gn_ref, tril_ref, mask_ref,
                o_ref, s_ref, *, ts):
    c = GLA_CHUNK
    nc = ts // c
    heads = range(GLA_HEADS)
    ks = [slice(h * GLA_DK, (h + 1) * GLA_DK) for h in heads]
    vs = [slice(h * GLA_DV, (h + 1) * GLA_DV) for h in heads]

    @pl.when(pl.program_id(1) == 0)
    def _():
        s_ref[...] = jnp.zeros_like(s_ref)

    z = jnp.dot(a_ref[...], wa_ref[...], preferred_element_type=F32) + ba_ref[...]
    log_a = (jnp.minimum(z, 0.0) - jnp.log(1.0 + jnp.exp(-jnp.abs(z)))) * (1.0 / GLA_GATE_NORM)

    la_hi = log_a.astype(BF16)
    la_lo = (log_a - la_hi.astype(F32)).astype(BF16)
    cum2 = jnp.dot(tril_ref[...], jnp.concatenate([la_hi, la_lo], axis=1), preferred_element_type=F32)
    bcum = cum2[:, :GLA_QK_W] + cum2[:, GLA_QK_W:]
    bcum3 = bcum.reshape(nc, c, GLA_QK_W)
    b_last = bcum3[:, c - 1:c, :]

    q = q_ref[...].astype(F32) * (GLA_DK ** -0.5)
    k = k_ref[...].astype(F32)
    q_e = (q * jnp.exp(bcum)).astype(BF16)
    k_e = (k * jnp.exp(-bcum)).astype(BF16)
    k_d = (k * jnp.exp(b_last - bcum3).reshape(ts, GLA_QK_W)).astype(BF16)
    decay = jnp.exp(b_last)
    v = [v_ref[:, vs[h]] for h in heads]

    keep = mask_ref[...] != 0.0

    for h in heads:
        scores = lax.dot_general(q_e[:, ks[h]], k_e[:, ks[h]], _NT, preferred_element_type=F32)
        scores = jnp.where(keep, scores, 0.0).astype(BF16)
        o_intra = jnp.dot(scores, v[h], preferred_element_type=F32)

        state_t = s_ref[h]
        o_inter = []
        for n in range(nc):
            rows = slice(n * c, (n + 1) * c)
            o_inter.append(lax.dot_general(q_e[rows, ks[h]], state_t.astype(BF16), _NT,
                                           preferred_element_type=F32))
            kv_t = lax.dot_general(v[h][rows], k_d[rows, ks[h]], _TN, preferred_element_type=F32)
            state_t = state_t * decay[n][:, ks[h]] + kv_t
        s_ref[h] = state_t

        o = _rms(o_intra + jnp.concatenate(o_inter, axis=0), gn_ref[...])
        gate = g_ref[:, vs[h]].astype(F32)
        o_ref[:, vs[h]] = (o * (gate * jax.nn.sigmoid(gate))).astype(BF16)


def _gla(proj, wa, ba, gn, batch, seq, ts):
    nt = seq // ts
    kern = functools.partial(_gla_kernel, ts=ts)
    rows = lambda b, i: b * nt + i
    idx = jnp.arange(ts)
    causal = (idx[:, None] >= idx[None, :]) & (idx[:, None] // GLA_CHUNK == idx[None, :] // GLA_CHUNK)
    return pl.pallas_call(
        kern,
        out_shape=jax.ShapeDtypeStruct((batch * seq, GLA_V_W), BF16),
        grid=(batch, nt),
        in_specs=[
            pl.BlockSpec((ts, GLA_QK_W), lambda b, i: (rows(b, i), COL_GQ // GLA_QK_W)),
            pl.BlockSpec((ts, GLA_QK_W), lambda b, i: (rows(b, i), COL_GK // GLA_QK_W)),
            pl.BlockSpec((ts, GLA_V_W), lambda b, i: (rows(b, i), COL_GV // GLA_V_W)),
            pl.BlockSpec((ts, GLA_V_W), lambda b, i: (rows(b, i), COL_GG // GLA_V_W)),
            pl.BlockSpec((ts, LANES), lambda b, i: (rows(b, i), COL_GA // LANES)),
            pl.BlockSpec((LANES, GLA_QK_W), lambda b, i: (0, 0)),
            pl.BlockSpec((1, GLA_QK_W), lambda b, i: (0, 0)),
            pl.BlockSpec((1, GLA_DV), lambda b, i: (0, 0)),
            pl.BlockSpec((ts, ts), lambda b, i: (0, 0)),
            pl.BlockSpec((ts, ts), lambda b, i: (0, 0)),
        ],
        out_specs=pl.BlockSpec((ts, GLA_V_W), lambda b, i: (rows(b, i), 0)),
        scratch_shapes=[pltpu.VMEM((GLA_HEADS, GLA_DV, GLA_DK), F32)],
        compiler_params=pltpu.CompilerParams(
            dimension_semantics=("arbitrary", "arbitrary"),
            vmem_limit_bytes=VMEM_LIMIT),
        name="gla",
    )(proj, proj, proj, proj, proj, wa, ba, gn, causal.astype(BF16), causal.astype(F32))


def _swa_kernel(sinks_ref, q_ref, k_ref, v_ref, pos_ref, posn_ref, freq_ref, eye_ref, o_ref,
                k2p_ref, vtp_ref, cos_ref, sin_ref):
    nb = pl.program_id(1)
    wb = WINDOW
    hd = SWA_HEAD_DIM
    half = hd // 2
    pairs_per_kv = SWA_Q_W // LANES // SWA_KV_HEADS
    lane = lax.broadcasted_iota(jnp.int32, (wb, LANES), 1)
    head_a = (lane % hd) < half
    low = lane < hd

    def rotary_tables(pos_row):
        ang_t = freq_ref[...] * pos_row.astype(F32)
        reps = LANES // half
        cs = jnp.concatenate([jnp.cos(ang_t)] * reps, axis=0).T
        sn = jnp.concatenate([jnp.sin(ang_t)] * reps, axis=0).T
        return cs, jnp.where(low, -sn, sn)

    @pl.when(nb == 0)
    def _():
        k2p_ref[...] = jnp.zeros_like(k2p_ref)
        vtp_ref[...] = jnp.zeros_like(vtp_ref)
        cos_ref[...], sin_ref[...] = rotary_tables(pos_ref[0])

    cos = cos_ref[...]
    sin = sin_ref[...]

    def rope(t, cs, sn):
        return t * cs + pltpu.roll(t, hd, 1) * sn

    k_r = rope(k_ref[...].astype(F32), cos, sin)
    k2_cur = [jnp.where(head_a, k_r, pltpu.roll(k_r, half, 1)).astype(BF16),
              jnp.where(head_a, pltpu.roll(k_r, LANES - half, 1), k_r).astype(BF16)]
    v_t = v_ref[...].astype(F32).T
    ones_rows = (lax.broadcasted_iota(jnp.int32, (SWA_VT_ROWS - hd, wb), 0) == 0).astype(F32)
    vt_cur = [jnp.concatenate([v_t[c * hd:(c + 1) * hd], ones_rows], axis=0).astype(BF16)
              for c in range(SWA_KV_HEADS)]
    vt = [jnp.concatenate([vtp_ref[c], vt_cur[c]], axis=1) for c in range(SWA_KV_HEADS)]

    ki = lax.broadcasted_iota(jnp.int32, (2 * wb, wb), 0)
    qi = lax.broadcasted_iota(jnp.int32, (2 * wb, wb), 1) + wb
    mask = (ki <= qi) & (qi - ki < WINDOW) & ((nb > 0) | (ki >= wb))
    bias = jnp.where(mask, 0.0, NEG_INF).astype(BF16)
    k2 = [jnp.concatenate([k2p_ref[c], k2_cur[c]], axis=0) for c in range(SWA_KV_HEADS)]
    k2 = [jnp.concatenate([k2[c], bias], axis=1) for c in range(SWA_KV_HEADS)]

    qscale = (hd ** -0.5) * LOG2E
    cos_q = cos * qscale
    sin_q = sin * qscale

    n_tiles = SWA_Q_W // LANES
    lhs = []
    for t in range(n_tiles):
        q_r = rope(q_ref[:, t * LANES:(t + 1) * LANES].astype(F32), cos_q, sin_q)
        lhs += [jnp.where(head_a, q_r, 0.0), jnp.where(head_a, 0.0, q_r)]
    cols_per_kv = 2 * pairs_per_kv * wb
    st = []
    for c in range(SWA_KV_HEADS):
        lhs_c = jnp.concatenate(lhs[2 * c * pairs_per_kv:2 * (c + 1) * pairs_per_kv], axis=0).astype(BF16)
        lhs_c = jnp.concatenate([lhs_c, eye_ref[...]], axis=1)
        st.append(lax.dot_general(k2[c], lhs_c, _NT, preferred_element_type=F32))
    st = jnp.concatenate(st, axis=1)
    m = jnp.max(st, axis=0, keepdims=True)
    e = jnp.exp2(st - m).astype(BF16)
    sink = jnp.concatenate([jnp.full((1, wb), sinks_ref[i], F32) for i in range(SWA_HEADS)], axis=1) * LOG2E
    sink_term = jnp.exp2(sink - m)
    for c in range(SWA_KV_HEADS):
        cols = slice(c * cols_per_kv, (c + 1) * cols_per_kv)
        ot = jnp.dot(vt[c], e[:, cols], preferred_element_type=F32)
        o_n = ot[:hd] * (1.0 / (ot[hd:hd + 1] + sink_term[:, cols]))
        for n in range(pairs_per_kv):
            t = c * pairs_per_kv + n
            pair = jnp.concatenate([o_n[:, 2 * n * wb:(2 * n + 1) * wb], o_n[:, (2 * n + 1) * wb:(2 * n + 2) * wb]],
                                   axis=0)
            o_ref[:, t * LANES:(t + 1) * LANES] = pair.T.astype(BF16)

    for c in range(SWA_KV_HEADS):
        k2p_ref[c] = k2_cur[c]
        vtp_ref[c] = vt_cur[c]
    cos_ref[...], sin_ref[...] = rotary_tables(posn_ref[0])


def _swa(proj, sinks, pos3, freq_col, batch, seq):
    nb = seq // WINDOW
    eye = jnp.tile(jnp.eye(WINDOW, dtype=BF16), (SWA_HEADS // SWA_KV_HEADS, 1))
    rows = lambda b, n: b * nb + n
    return pl.pallas_call(
        _swa_kernel,
        out_shape=jax.ShapeDtypeStruct((batch * seq, SWA_Q_W), BF16),
        grid=(batch, nb),
        in_specs=[
            pl.BlockSpec(memory_space=pltpu.SMEM),
            pl.BlockSpec((WINDOW, SWA_Q_W), lambda b, n: (rows(b, n), COL_SQ // SWA_Q_W)),
            pl.BlockSpec((WINDOW, SWA_KV_W), lambda b, n: (rows(b, n), COL_SK // SWA_KV_W)),
            pl.BlockSpec((WINDOW, SWA_KV_W), lambda b, n: (rows(b, n), COL_SV // SWA_KV_W)),
            pl.BlockSpec((1, 1, WINDOW), lambda b, n: (rows(b, n), 0, 0)),
            pl.BlockSpec((1, 1, WINDOW), lambda b, n: (rows(b, jnp.minimum(n + 1, nb - 1)), 0, 0)),
            pl.BlockSpec((SWA_HEAD_DIM // 2, 1), lambda b, n: (0, 0)),
            pl.BlockSpec((SWA_HEADS // SWA_KV_HEADS * WINDOW, WINDOW), lambda b, n: (0, 0)),
        ],
        out_specs=pl.BlockSpec((WINDOW, SWA_Q_W), lambda b, n: (rows(b, n), 0)),
        scratch_shapes=[pltpu.VMEM((SWA_KV_HEADS, WINDOW, LANES), BF16),
                        pltpu.VMEM((SWA_KV_HEADS, SWA_VT_ROWS, WINDOW), BF16),
                        pltpu.VMEM((WINDOW, LANES), F32), pltpu.VMEM((WINDOW, LANES), F32)],
        compiler_params=pltpu.CompilerParams(
            dimension_semantics=("arbitrary", "arbitrary"),
            vmem_limit_bytes=VMEM_LIMIT),
        name="swa",
    )(sinks, proj, proj, proj, pos3, pos3, freq_col, eye)


def _outproj_kernel(x_ref, a_ref, b_ref, wa_ref, wb_ref, h_ref):
    acc = jnp.dot(a_ref[...], wa_ref[...], preferred_element_type=F32)
    acc = acc + jnp.dot(b_ref[...], wb_ref[...], preferred_element_type=F32)
    h_ref[...] = x_ref[...] + acc


def _outproj(x2, o_gla, o_swa, w, tm):
    t = x2.shape[0]
    return pl.pallas_call(
        _outproj_kernel,
        out_shape=jax.ShapeDtypeStruct((t, D_MODEL), F32),
        grid=(t // tm,),
        in_specs=[
            pl.BlockSpec((tm, D_MODEL), lambda i: (i, 0)),
            pl.BlockSpec((tm, GLA_V_W), lambda i: (i, 0)),
            pl.BlockSpec((tm, SWA_Q_W), lambda i: (i, 0)),
            pl.BlockSpec((GLA_V_W, D_MODEL), lambda i: (0, 0)),
            pl.BlockSpec((SWA_Q_W, D_MODEL), lambda i: (GLA_V_W // SWA_Q_W, 0)),
        ],
        out_specs=pl.BlockSpec((tm, D_MODEL), lambda i: (i, 0)),
        compiler_params=pltpu.CompilerParams(
            dimension_semantics=("arbitrary",),
            vmem_limit_bytes=VMEM_LIMIT),
        name="outproj",
    )(x2, o_gla, o_swa, w, w)


def _ffn_kernel(h_ref, gf_ref, wg_ref, wu_ref, cw_ref, cb_ref, wd_ref, gl_ref, o_ref,
                hn_ref, carry_ref, *, tm, rc, seq):
    i = pl.program_id(0)
    j = pl.program_id(1)
    nj = pl.num_programs(1)

    def step(first, last):
        seq_start = (i * tm) % seq == 0
        prev = jnp.where(seq_start, 0.0, carry_ref[j])
        rows = lax.broadcasted_iota(jnp.int32, prev.shape, 0)
        cb = cb_ref[...]
        cw0, cw1, cw2 = cw_ref[0:1, :], cw_ref[1:2, :], cw_ref[2:3, :]

        for r in range(tm // rc):
            rs = pl.ds(r * rc, rc)
            if first:
                res = h_ref[rs, :]
                hn = _rms(res, gf_ref[...]).astype(BF16)
                hn_ref[rs, :] = hn
            else:
                res = o_ref[rs, :]
                hn = hn_ref[rs, :]
            gate = jnp.dot(hn, wg_ref[...], preferred_element_type=F32)
            up = jnp.dot(hn, wu_ref[...], preferred_element_type=F32)

            def shifted(d):
                rolled = pltpu.roll(gate, d, 0)
                top = jnp.where(rows < d, pltpu.roll(prev, d, 0), rolled[:SUBLANES])
                return jnp.concatenate([top, rolled[SUBLANES:]], axis=0)

            conv = cb + cw0 * shifted(2) + cw1 * shifted(1) + cw2 * gate
            act = (conv * jax.nn.sigmoid(conv) * up).astype(BF16)
            acc = res + jnp.dot(act, wd_ref[...], preferred_element_type=F32)
            o_ref[rs, :] = _rms(acc, gl_ref[...]) if last else acc
            prev = gate[rc - SUBLANES:, :]
        carry_ref[j] = prev

    pl.when(j == 0)(functools.partial(step, True, False))
    pl.when((j > 0) & (j < nj - 1))(functools.partial(step, False, False))
    pl.when(j == nj - 1)(functools.partial(step, False, True))


def _ffn(h, gf, wg, wu, cw, cb, wd, gl, seq, tm, tf, rc):
    t = h.shape[0]
    nj = D_FF // tf
    kern = functools.partial(_ffn_kernel, tm=tm, rc=rc, seq=seq)
    return pl.pallas_call(
        kern,
        out_shape=jax.ShapeDtypeStruct((t, D_MODEL), F32),
        grid=(t // tm, nj),
        in_specs=[
            pl.BlockSpec((tm, D_MODEL), lambda i, j: (i, 0)),
            pl.BlockSpec((1, D_MODEL), lambda i, j: (0, 0)),
            pl.BlockSpec((D_MODEL, tf), lambda i, j: (0, j)),
            pl.BlockSpec((D_MODEL, tf), lambda i, j: (0, j)),
            pl.BlockSpec((CONV_WIDTH, tf), lambda i, j: (0, j)),
            pl.BlockSpec((1, tf), lambda i, j: (0, j)),
            pl.BlockSpec((tf, D_MODEL), lambda i, j: (j, 0)),
            pl.BlockSpec((1, D_MODEL), lambda i, j: (0, 0)),
        ],
        out_specs=pl.BlockSpec((tm, D_MODEL), lambda i, j: (i, 0)),
        scratch_shapes=[pltpu.VMEM((tm, D_MODEL), BF16), pltpu.VMEM((nj, SUBLANES, tf), F32)],
        compiler_params=pltpu.CompilerParams(
            dimension_semantics=("arbitrary", "arbitrary"),
            vmem_limit_bytes=VMEM_LIMIT),
        name="convffn",
    )(h, gf, wg, wu, cw, cb, wd, gl)


def kernel(x, positions, attn_norm, w_in, w_a_up, b_a_up, gla_norm, sinks, w_out, ffn_norm,
           w_gate, w_up, conv_w, conv_b, w_down, final_norm):
    batch, seq, _ = x.shape
    t = batch * seq
    assert w_in.shape[0] == 1, "the final norm is fused into the single layer's FFN kernel"
    x2 = x.reshape(t, D_MODEL)
    pos3 = positions.reshape(t // WINDOW, 1, WINDOW)
    half = SWA_HEAD_DIM // 2
    freq_col = (ROPE_THETA ** (-jnp.arange(half, dtype=F32) / half)).reshape(half, 1)

    w_proj = _wprep(jnp.swapaxes(w_in[0], 0, 1), tk=256)
    wa = jnp.concatenate(
        [w_a_up[0], jnp.zeros((LANES - GLA_RANK, GLA_QK_W), F32)], axis=0).astype(BF16)

    proj = _inproj(x2, attn_norm[0].reshape(1, D_MODEL), w_proj, tm=1024, rc=256)
    o_gla = _gla(proj, wa, b_a_up[0].reshape(1, GLA_QK_W), gla_norm[0].reshape(1, GLA_DV),
                 batch, seq, ts=512)
    o_swa = _swa(proj, sinks[0], pos3, freq_col, batch, seq)
    h = _outproj(x2, o_gla, o_swa, w_out[0].astype(BF16), tm=512)
    y = _ffn(h, ffn_norm[0].reshape(1, D_MODEL), w_gate[0].astype(BF16), w_up[0].astype(BF16),
             conv_w[0], conv_b[0].reshape(1, D_FF), w_down[0].astype(BF16),
             final_norm.reshape(1, D_MODEL), seq, tm=1024, tf=512, rc=256)
    return y.reshape(batch, seq, D_MODEL)
```

```python
import functools

import jax
import jax.numpy as jnp
from jax import lax
from jax.experimental import pallas as pl
from jax.experimental.pallas import tpu as pltpu

D_MODEL = 2048
GLA_HEADS = 4
GLA_DK = 128
GLA_DV = 256
GLA_RANK = 16
GLA_GATE_NORM = 16.0
GLA_CHUNK = 64
GLA_SUB = 256
SWA_HEADS = 16
SWA_KV_HEADS = 2
SWA_HEAD_DIM = 64
WINDOW = 128
ROPE_THETA = 10000.0
D_FF = 5632
CONV_WIDTH = 3
EPS = 1e-6
NEG_INF = -1e30
LOG2E = 1.4426950408889634
SWA_VT_ROWS = SWA_HEAD_DIM + 16

GLA_QK_W = GLA_HEADS * GLA_DK
GLA_V_W = GLA_HEADS * GLA_DV
SWA_Q_W = SWA_HEADS * SWA_HEAD_DIM
SWA_KV_W = SWA_KV_HEADS * SWA_HEAD_DIM

LANES = 128
SUBLANES = 8

COL_GQ = 0
COL_GK = COL_GQ + GLA_QK_W
COL_GV = COL_GK + GLA_QK_W
COL_GG = COL_GV + GLA_V_W
COL_SQ = COL_GG + GLA_V_W
COL_SK = COL_SQ + SWA_Q_W
COL_SV = COL_SK + SWA_KV_W
COL_GA = COL_SV + SWA_KV_W
PROJ_TN = 1536
PROJ_W = 3 * PROJ_TN

VMEM_LIMIT = 58 * 1024 * 1024

F32 = jnp.float32
BF16 = jnp.bfloat16

_NT = (((1,), (1,)), ((), ()))
_TN = (((0,), (0,)), ((), ()))


def _rms(x, gain):
    return x * lax.rsqrt(jnp.mean(x * x, axis=-1, keepdims=True) + EPS) * gain


def _wprep_kernel(wt_ref, o_ref):
    tk = wt_ref.shape[1]
    half = SWA_HEAD_DIM // 2
    o_ga = 2 * GLA_QK_W + 2 * GLA_V_W
    o_sq = o_ga + GLA_RANK

    def put(col, rows):
        o_ref[:, col:col + rows.shape[0]] = rows.T.astype(BF16)

    put(0, wt_ref[:o_ga, :])
    for i in range((SWA_Q_W + SWA_KV_W) // LANES):
        r0 = o_sq + i * LANES
        blk = [wt_ref[r0 + j * half:r0 + (j + 1) * half, :] for j in range(4)]
        put(COL_SQ + i * LANES, jnp.concatenate([blk[0], blk[2], blk[1], blk[3]], axis=0))
    put(COL_SV, wt_ref[o_sq + SWA_Q_W + SWA_KV_W:, :])
    keep = lax.broadcasted_iota(jnp.int32, (LANES, tk), 0) < GLA_RANK
    put(COL_GA, jnp.where(keep, wt_ref[o_ga:o_ga + LANES, :], 0.0))
    o_ref[:, COL_GA + LANES:] = jnp.zeros((tk, PROJ_W - COL_GA - LANES), BF16)


def _wprep(wt, tk):
    n, k = wt.shape
    return pl.pallas_call(
        _wprep_kernel,
        out_shape=jax.ShapeDtypeStruct((k, PROJ_W), BF16),
        grid=(k // tk,),
        in_specs=[pl.BlockSpec((n, tk), lambda i: (0, i))],
        out_specs=pl.BlockSpec((tk, PROJ_W), lambda i: (i, 0)),
        compiler_params=pltpu.CompilerParams(
            dimension_semantics=("arbitrary",),
            vmem_limit_bytes=VMEM_LIMIT),
        name="wprep",
    )(wt)


def _inproj_kernel(x_ref, g_ref, w_ref, o_ref, u_ref, *, tm, rc):
    j = pl.program_id(1)

    @pl.when(j == 0)
    def _():
        for r in range(tm // rc):
            rs = pl.ds(r * rc, rc)
            u = _rms(x_ref[rs, :], g_ref[...]).astype(BF16)
            u_ref[rs, :] = u
            o_ref[rs, :] = jnp.dot(u, w_ref[...], preferred_element_type=F32).astype(BF16)

    @pl.when(j > 0)
    def _():
        o_ref[...] = jnp.dot(u_ref[...], w_ref[...], preferred_element_type=F32).astype(BF16)


def _inproj(x2, gain, w, tm, rc):
    t = x2.shape[0]
    return pl.pallas_call(
        functools.partial(_inproj_kernel, tm=tm, rc=rc),
        out_shape=jax.ShapeDtypeStruct((t, PROJ_W), BF16),
        grid=(t // tm, PROJ_W // PROJ_TN),
        in_specs=[
            pl.BlockSpec((tm, D_MODEL), lambda i, j: (i, 0)),
            pl.BlockSpec((1, D_MODEL), lambda i, j: (0, 0)),
            pl.BlockSpec((D_MODEL, PROJ_TN), lambda i, j: (0, j)),
        ],
        out_specs=pl.BlockSpec((tm, PROJ_TN), lambda i, j: (i, j)),
        scratch_shapes=[pltpu.VMEM((tm, D_MODEL), BF16)],
        compiler_params=pltpu.CompilerParams(
            dimension_semantics=("arbitrary", "arbitrary"),
            vmem_limit_bytes=VMEM_LIMIT),
        name="inproj",
    )(x2, gain, w)


def _gla_kernel(q_ref, k_ref, v_ref, g_ref, a_ref, wa_ref, ba_ref, gn_ref, tril_ref, mask_ref,
                o_ref, s_ref, *, ts):
    c = GLA_CHUNK
    nc = ts // c
    heads = range(GLA_HEADS)
    ks = [slice(h * GLA_DK, (h + 1) * GLA_DK) for h in heads]
    vs = [slice(h * GLA_DV, (h + 1) * GLA_DV) for h in heads]

    @pl.when(pl.program_id(1) == 0)
    def _():
        s_ref[...] = jnp.zeros_like(s_ref)

    z = jnp.dot(a_ref[...], wa_ref[...], preferred_element_type=F32) + ba_ref[...]
    log_a = (jnp.minimum(z, 0.0) - jnp.log(1.0 + jnp.exp(-jnp.abs(z)))) * (1.0 / GLA_GATE_NORM)

    la_hi = log_a.astype(BF16)
    la_lo = (log_a - la_hi.astype(F32)).astype(BF16)
    la_hl = jnp.concatenate([la_hi, la_lo], axis=1)
    subs = [slice(s * GLA_SUB, (s + 1) * GLA_SUB) for s in range(ts // GLA_SUB)]
    cum2 = jnp.concatenate([jnp.dot(tril_ref[...], la_hl[rs], preferred_element_type=F32) for rs in subs], axis=0)
    bcum = cum2[:, :GLA_QK_W] + cum2[:, GLA_QK_W:]
    bcum3 = bcum.reshape(nc, c, GLA_QK_W)
    b_last = bcum3[:, c - 1:c, :]

    q = q_ref[...].astype(F32) * (GLA_DK ** -0.5)
    k = k_ref[...].astype(F32)
    q_e = (q * jnp.exp(bcum)).astype(BF16)
    k_e = (k * jnp.exp(-bcum)).astype(BF16)
    k_d = (k * jnp.exp(b_last - bcum3).reshape(ts, GLA_QK_W)).astype(BF16)
    decay = jnp.exp(b_last)
    v = [v_ref[:, vs[h]] for h in heads]

    keep = mask_ref[...] != 0.0

    for h in heads:
        o_intra = []
        for rs in subs:
            scores = lax.dot_general(q_e[rs, ks[h]], k_e[rs, ks[h]], _NT, preferred_element_type=F32)
            scores = jnp.where(keep, scores, 0.0).astype(BF16)
            o_intra.append(jnp.dot(scores, v[h][rs], preferred_element_type=F32))
        o_intra = jnp.concatenate(o_intra, axis=0)

        state_t = s_ref[h]
        o_inter = []
        for n in range(nc):
            rows = slice(n * c, (n + 1) * c)
            o_inter.append(lax.dot_general(q_e[rows, ks[h]], state_t.astype(BF16), _NT,
                                           preferred_element_type=F32))
            kv_t = lax.dot_general(v[h][rows], k_d[rows, ks[h]], _TN, preferred_element_type=F32)
            state_t = state_t * decay[n][:, ks[h]] + kv_t
        s_ref[h] = state_t

        o = _rms(o_intra + jnp.concatenate(o_inter, axis=0), gn_ref[...])
        gate = g_ref[:, vs[h]].astype(F32)
        o_ref[:, vs[h]] = (o * (gate * jax.nn.sigmoid(gate))).astype(BF16)


def _gla(proj, wa, ba, gn, batch, seq, ts):
    nt = seq // ts
    kern = functools.partial(_gla_kernel, ts=ts)
    rows = lambda b, i: b * nt + i
    idx = jnp.arange(GLA_SUB)
    causal = (idx[:, None] >= idx[None, :]) & (idx[:, None] // GLA_CHUNK == idx[None, :] // GLA_CHUNK)
    return pl.pallas_call(
        kern,
        out_shape=jax.ShapeDtypeStruct((batch * seq, GLA_V_W), BF16),
        grid=(batch, nt),
        in_specs=[
            pl.BlockSpec((ts, GLA_QK_W), lambda b, i: (rows(b, i), COL_GQ // GLA_QK_W)),
            pl.BlockSpec((ts, GLA_QK_W), lambda b, i: (rows(b, i), COL_GK // GLA_QK_W)),
            pl.BlockSpec((ts, GLA_V_W), lambda b, i: (rows(b, i), COL_GV // GLA_V_W)),
            pl.BlockSpec((ts, GLA_V_W), lambda b, i: (rows(b, i), COL_GG // GLA_V_W)),
            pl.BlockSpec((ts, LANES), lambda b, i: (rows(b, i), COL_GA // LANES)),
            pl.BlockSpec((LANES, GLA_QK_W), lambda b, i: (0, 0)),
            pl.BlockSpec((1, GLA_QK_W), lambda b, i: (0, 0)),
            pl.BlockSpec((1, GLA_DV), lambda b, i: (0, 0)),
            pl.BlockSpec((GLA_SUB, GLA_SUB), lambda b, i: (0, 0)),
            pl.BlockSpec((GLA_SUB, GLA_SUB), lambda b, i: (0, 0)),
        ],
        out_specs=pl.BlockSpec((ts, GLA_V_W), lambda b, i: (rows(b, i), 0)),
        scratch_shapes=[pltpu.VMEM((GLA_HEADS, GLA_DV, GLA_DK), F32)],
        compiler_params=pltpu.CompilerParams(
            dimension_semantics=("arbitrary", "arbitrary"),
            vmem_limit_bytes=VMEM_LIMIT),
        name="gla",
    )(proj, proj, proj, proj, proj, wa, ba, gn, causal.astype(BF16), causal.astype(F32))


def _swa_kernel(sinks_ref, q_ref, k_ref, v_ref, pos_ref, posn_ref, freq_ref, eye_ref, o_ref,
                k2p_ref, vtp_ref, cos_ref, sin_ref):
    nb = pl.program_id(1)
    wb = WINDOW
    hd = SWA_HEAD_DIM
    half = hd // 2
    pairs_per_kv = SWA_Q_W // LANES // SWA_KV_HEADS
    lane = lax.broadcasted_iota(jnp.int32, (wb, LANES), 1)
    head_a = (lane % hd) < half
    low = lane < hd

    def rotary_tables(pos_row):
        ang_t = freq_ref[...] * pos_row.astype(F32)
        reps = LANES // half
        cs = jnp.concatenate([jnp.cos(ang_t)] * reps, axis=0).T
        sn = jnp.concatenate([jnp.sin(ang_t)] * reps, axis=0).T
        return cs, jnp.where(low, -sn, sn)

    @pl.when(nb == 0)
    def _():
        k2p_ref[...] = jnp.zeros_like(k2p_ref)
        vtp_ref[...] = jnp.zeros_like(vtp_ref)
        cos_ref[...], sin_ref[...] = rotary_tables(pos_ref[0])

    cos = cos_ref[...]
    sin = sin_ref[...]

    def rope(t, cs, sn):
        return t * cs + pltpu.roll(t, hd, 1) * sn

    k_r = rope(k_ref[...].astype(F32), cos, sin)
    k2_cur = [jnp.where(head_a, k_r, pltpu.roll(k_r, half, 1)).astype(BF16),
              jnp.where(head_a, pltpu.roll(k_r, LANES - half, 1), k_r).astype(BF16)]
    v_t = v_ref[...].astype(F32).T
    ones_rows = (lax.broadcasted_iota(jnp.int32, (SWA_VT_ROWS - hd, wb), 0) == 0).astype(F32)
    vt_cur = [jnp.concatenate([v_t[c * hd:(c + 1) * hd], ones_rows], axis=0).astype(BF16)
              for c in range(SWA_KV_HEADS)]
    vt = [jnp.concatenate([vtp_ref[c], vt_cur[c]], axis=1) for c in range(SWA_KV_HEADS)]

    ki = lax.broadcasted_iota(jnp.int32, (2 * wb, wb), 0)
    qi = lax.broadcasted_iota(jnp.int32, (2 * wb, wb), 1) + wb
    mask = (ki <= qi) & (qi - ki < WINDOW) & ((nb > 0) | (ki >= wb))
    bias = jnp.where(mask, 0.0, NEG_INF).astype(BF16)
    k2 = [jnp.concatenate([k2p_ref[c], k2_cur[c]], axis=0) for c in range(SWA_KV_HEADS)]
    k2 = [jnp.concatenate([k2[c], bias], axis=1) for c in range(SWA_KV_HEADS)]

    qscale = (hd ** -0.5) * LOG2E
    cos_q = cos * qscale
    sin_q = sin * qscale

    n_tiles = SWA_Q_W // LANES
    lhs = []
    for t in range(n_tiles):
        q_r = rope(q_ref[:, t * LANES:(t + 1) * LANES].astype(F32), cos_q, sin_q)
        lhs += [jnp.where(head_a, q_r, 0.0), jnp.where(head_a, 0.0, q_r)]
    cols_per_kv = 2 * pairs_per_kv * wb
    st = []
    for c in range(SWA_KV_HEADS):
        lhs_c = jnp.concatenate(lhs[2 * c * pairs_per_kv:2 * (c + 1) * pairs_per_kv], axis=0).astype(BF16)
        lhs_c = jnp.concatenate([lhs_c, eye_ref[...]], axis=1)
        st.append(lax.dot_general(k2[c], lhs_c, _NT, preferred_element_type=F32))
    st = jnp.concatenate(st, axis=1)
    m = jnp.max(st, axis=0, keepdims=True)
    e = jnp.exp2(st - m).astype(BF16)
    sink = jnp.concatenate([jnp.full((1, wb), sinks_ref[i], F32) for i in range(SWA_HEADS)], axis=1) * LOG2E
    sink_term = jnp.exp2(sink - m)
    for c in range(SWA_KV_HEADS):
        cols = slice(c * cols_per_kv, (c + 1) * cols_per_kv)
        ot = jnp.dot(vt[c], e[:, cols], preferred_element_type=F32)
        o_n = ot[:hd] * (1.0 / (ot[hd:hd + 1] + sink_term[:, cols]))
        for n in range(pairs_per_kv):
            t = c * pairs_per_kv + n
            pair = jnp.concatenate([o_n[:, 2 * n * wb:(2 * n + 1) * wb], o_n[:, (2 * n + 1) * wb:(2 * n + 2) * wb]],
                                   axis=0)
            o_ref[:, t * LANES:(t + 1) * LANES] = pair.T.astype(BF16)

    for c in range(SWA_KV_HEADS):
        k2p_ref[c] = k2_cur[c]
        vtp_ref[c] = vt_cur[c]
    cos_ref[...], sin_ref[...] = rotary_tables(posn_ref[0])


def _swa(proj, sinks, pos3, freq_col, batch, seq):
    nb = seq // WINDOW
    eye = jnp.tile(jnp.eye(WINDOW, dtype=BF16), (SWA_HEADS // SWA_KV_HEADS, 1))
    rows = lambda b, n: b * nb + n
    return pl.pallas_call(
        _swa_kernel,
        out_shape=jax.ShapeDtypeStruct((batch * seq, SWA_Q_W), BF16),
        grid=(batch, nb),
        in_specs=[
            pl.BlockSpec(memory_space=pltpu.SMEM),
            pl.BlockSpec((WINDOW, SWA_Q_W), lambda b, n: (rows(b, n), COL_SQ // SWA_Q_W)),
            pl.BlockSpec((WINDOW, SWA_KV_W), lambda b, n: (rows(b, n), COL_SK // SWA_KV_W)),
            pl.BlockSpec((WINDOW, SWA_KV_W), lambda b, n: (rows(b, n), COL_SV // SWA_KV_W)),
            pl.BlockSpec((1, 1, WINDOW), lambda b, n: (rows(b, n), 0, 0)),
            pl.BlockSpec((1, 1, WINDOW), lambda b, n: (rows(b, jnp.minimum(n + 1, nb - 1)), 0, 0)),
            pl.BlockSpec((SWA_HEAD_DIM // 2, 1), lambda b, n: (0, 0)),
            pl.BlockSpec((SWA_HEADS // SWA_KV_HEADS * WINDOW, WINDOW), lambda b, n: (0, 0)),
        ],
        out_specs=pl.BlockSpec((WINDOW, SWA_Q_W), lambda b, n: (rows(b, n), 0)),
        scratch_shapes=[pltpu.VMEM((SWA_KV_HEADS, WINDOW, LANES), BF16),
                        pltpu.VMEM((SWA_KV_HEADS, SWA_VT_ROWS, WINDOW), BF16),
                        pltpu.VMEM((WINDOW, LANES), F32), pltpu.VMEM((WINDOW, LANES), F32)],
        compiler_params=pltpu.CompilerParams(
            dimension_semantics=("arbitrary", "arbitrary"),
            vmem_limit_bytes=VMEM_LIMIT),
        name="swa",
    )(sinks, proj, proj, proj, pos3, pos3, freq_col, eye)


def _outproj_kernel(x_ref, a_ref, b_ref, wa_ref, wb_ref, h_ref):
    acc = jnp.dot(a_ref[...], wa_ref[...], preferred_element_type=F32)
    acc = acc + jnp.dot(b_ref[...], wb_ref[...], preferred_element_type=F32)
    h_ref[...] = x_ref[...] + acc


def _outproj(x2, o_gla, o_swa, w, tm):
    t = x2.shape[0]
    return pl.pallas_call(
        _outproj_kernel,
        out_shape=jax.ShapeDtypeStruct((t, D_MODEL), F32),
        grid=(t // tm,),
        in_specs=[
            pl.BlockSpec((tm, D_MODEL), lambda i: (i, 0)),
            pl.BlockSpec((tm, GLA_V_W), lambda i: (i, 0)),
            pl.BlockSpec((tm, SWA_Q_W), lambda i: (i, 0)),
            pl.BlockSpec((GLA_V_W, D_MODEL), lambda i: (0, 0)),
            pl.BlockSpec((SWA_Q_W, D_MODEL), lambda i: (GLA_V_W // SWA_Q_W, 0)),
        ],
        out_specs=pl.BlockSpec((tm, D_MODEL), lambda i: (i, 0)),
        compiler_params=pltpu.CompilerParams(
            dimension_semantics=("arbitrary",),
            vmem_limit_bytes=VMEM_LIMIT),
        name="outproj",
    )(x2, o_gla, o_swa, w, w)


def _ffn_kernel(h_ref, gf_ref, wg_ref, wu_ref, cw_ref, cb_ref, wd_ref, gl_ref, o_ref,
                hn_ref, carry_ref, *, tm, rc, seq):
    i = pl.program_id(0)
    j = pl.program_id(1)
    nj = pl.num_programs(1)

    def step(first, last):
        seq_start = (i * tm) % seq == 0
        prev = jnp.where(seq_start, 0.0, carry_ref[j])
        rows = lax.broadcasted_iota(jnp.int32, prev.shape, 0)
        cb = cb_ref[...]
        cw0, cw1, cw2 = cw_ref[0:1, :], cw_ref[1:2, :], cw_ref[2:3, :]

        for r in range(tm // rc):
            rs = pl.ds(r * rc, rc)
            if first:
                res = h_ref[rs, :]
                hn = _rms(res, gf_ref[...]).astype(BF16)
                hn_ref[rs, :] = hn
            else:
                res = o_ref[rs, :]
                hn = hn_ref[rs, :]
            gate = jnp.dot(hn, wg_ref[...], preferred_element_type=F32)
            up = jnp.dot(hn, wu_ref[...], preferred_element_type=F32)

            def shifted(d):
                rolled = pltpu.roll(gate, d, 0)
                top = jnp.where(rows < d, pltpu.roll(prev, d, 0), rolled[:SUBLANES])
                return jnp.concatenate([top, rolled[SUBLANES:]], axis=0)

            conv = cb + cw0 * shifted(2) + cw1 * shifted(1) + cw2 * gate
            act = (conv * jax.nn.sigmoid(conv) * up).astype(BF16)
            acc = res + jnp.dot(act, wd_ref[...], preferred_element_type=F32)
            o_ref[rs, :] = _rms(acc, gl_ref[...]) if last else acc
            prev = gate[rc - SUBLANES:, :]
        carry_ref[j] = prev

    pl.when(j == 0)(functools.partial(step, True, False))
    pl.when((j > 0) & (j < nj - 1))(functools.partial(step, False, False))
    pl.when(j == nj - 1)(functools.partial(step, False, True))


def _ffn(h, gf, wg, wu, cw, cb, wd, gl, seq, tm, tf, rc):
    t = h.shape[0]
    nj = D_FF // tf
    kern = functools.partial(_ffn_kernel, tm=tm, rc=rc, seq=seq)
    return pl.pallas_call(
        kern,
        out_shape=jax.ShapeDtypeStruct((t, D_MODEL), F32),
        grid=(t // tm, nj),
        in_specs=[
            pl.BlockSpec((tm, D_MODEL), lambda i, j: (i, 0)),
            pl.BlockSpec((1, D_MODEL), lambda i, j: (0, 0)),
            pl.BlockSpec((D_MODEL, tf), lambda i, j: (0, j)),
            pl.BlockSpec((D_MODEL, tf), lambda i, j: (0, j)),
            pl.BlockSpec((CONV_WIDTH, tf), lambda i, j: (0, j)),
            pl.BlockSpec((1, tf), lambda i, j: (0, j)),
            pl.BlockSpec((tf, D_MODEL), lambda i, j: (j, 0)),
            pl.BlockSpec((1, D_MODEL), lambda i, j: (0, 0)),
        ],
        out_specs=pl.BlockSpec((tm, D_MODEL), lambda i, j: (i, 0)),
        scratch_shapes=[pltpu.VMEM((tm, D_MODEL), BF16), pltpu.VMEM((nj, SUBLANES, tf), F32)],
        compiler_params=pltpu.CompilerParams(
            dimension_semantics=("arbitrary", "arbitrary"),
            vmem_limit_bytes=VMEM_LIMIT),
        name="convffn",
    )(h, gf, wg, wu, cw, cb, wd, gl)


def kernel(x, positions, attn_norm, w_in, w_a_up, b_a_up, gla_norm, sinks, w_out, ffn_norm,
           w_gate, w_up, conv_w, conv_b, w_down, final_norm):
    batch, seq, _ = x.shape
    t = batch * seq
    assert w_in.shape[0] == 1, "the final norm is fused into the single layer's FFN kernel"
    x2 = x.reshape(t, D_MODEL)
    pos3 = positions.reshape(t // WINDOW, 1, WINDOW)
    half = SWA_HEAD_DIM // 2
    freq_col = (ROPE_THETA ** (-jnp.arange(half, dtype=F32) / half)).reshape(half, 1)

    w_proj = _wprep(jnp.swapaxes(w_in[0], 0, 1), tk=256)
    wa = jnp.concatenate(
        [w_a_up[0], jnp.zeros((LANES - GLA_RANK, GLA_QK_W), F32)], axis=0).astype(BF16)

    proj = _inproj(x2, attn_norm[0].reshape(1, D_MODEL), w_proj, tm=1024, rc=256)
    o_gla = _gla(proj, wa, b_a_up[0].reshape(1, GLA_QK_W), gla_norm[0].reshape(1, GLA_DV),
                 batch, seq, ts=1024)
    o_swa = _swa(proj, sinks[0], pos3, freq_col, batch, seq)
    h = _outproj(x2, o_gla, o_swa, w_out[0].astype(BF16), tm=512)
    y = _ffn(h, ffn_norm[0].reshape(1, D_MODEL), w_gate[0].astype(BF16), w_up[0].astype(BF16),
             conv_w[0], conv_b[0].reshape(1, D_FF), w_down[0].astype(BF16),
             final_norm.reshape(1, D_MODEL), seq, tm=1024, tf=512, rc=512)
    return y.reshape(batch, seq, D_MODEL)
```

```python
import functools

import jax
import jax.numpy as jnp
from jax import lax
from jax.experimental import pallas as pl
from jax.experimental.pallas import tpu as pltpu

D_MODEL = 2048
GLA_HEADS = 4
GLA_DK = 128
GLA_DV = 256
GLA_RANK = 16
GLA_GATE_NORM = 16.0
GLA_CHUNK = 64
GLA_SUB = 256
SWA_HEADS = 16
SWA_KV_HEADS = 2
SWA_HEAD_DIM = 64
WINDOW = 128
ROPE_THETA = 10000.0
D_FF = 5632
CONV_WIDTH = 3
EPS = 1e-6
NEG_INF = -1e30
LOG2E = 1.4426950408889634
SWA_VT_ROWS = SWA_HEAD_DIM + 16

GLA_QK_W = GLA_HEADS * GLA_DK
GLA_V_W = GLA_HEADS * GLA_DV
SWA_Q_W = SWA_HEADS * SWA_HEAD_DIM
SWA_KV_W = SWA_KV_HEADS * SWA_HEAD_DIM

LANES = 128
SUBLANES = 8

COL_GQ = 0
COL_GK = COL_GQ + GLA_QK_W
COL_GV = COL_GK + GLA_QK_W
COL_GG = COL_GV + GLA_V_W
COL_SQ = COL_GG + GLA_V_W
COL_SK = COL_SQ + SWA_Q_W
COL_SV = COL_SK + SWA_KV_W
COL_GA = COL_SV + SWA_KV_W
PROJ_TN = 1536
PROJ_W = 3 * PROJ_TN

VMEM_LIMIT = 58 * 1024 * 1024

F32 = jnp.float32
BF16 = jnp.bfloat16

_NT = (((1,), (1,)), ((), ()))
_TN = (((0,), (0,)), ((), ()))


def _rms(x, gain):
    return x * lax.rsqrt(jnp.mean(x * x, axis=-1, keepdims=True) + EPS) * gain


def _cast_specs(weights, steps, step_index):
    in_specs, out_specs, out_shapes = [], [], []
    for w in weights:
        rows, cols = w.shape
        slab = rows // steps
        assert slab * steps == rows and slab % (2 * SUBLANES) == 0, (w.shape, steps)
        spec = pl.BlockSpec((slab, cols), lambda *g: (step_index(*g), 0))
        in_specs.append(spec)
        out_specs.append(spec)
        out_shapes.append(jax.ShapeDtypeStruct(w.shape, BF16))
    return in_specs, out_specs, out_shapes


def _cast_slabs(src_refs, dst_refs):
    for src, dst in zip(src_refs, dst_refs):
        dst[...] = src[...].astype(BF16)


def _wprep_kernel(wt_ref, o_ref):
    tk = wt_ref.shape[1]
    half = SWA_HEAD_DIM // 2
    o_ga = 2 * GLA_QK_W + 2 * GLA_V_W
    o_sq = o_ga + GLA_RANK

    def put(col, rows):
        o_ref[:, col:col + rows.shape[0]] = rows.T.astype(BF16)

    put(0, wt_ref[:o_ga, :])
    for i in range((SWA_Q_W + SWA_KV_W) // LANES):
        r0 = o_sq + i * LANES
        blk = [wt_ref[r0 + j * half:r0 + (j + 1) * half, :] for j in range(4)]
        put(COL_SQ + i * LANES, jnp.concatenate([blk[0], blk[2], blk[1], blk[3]], axis=0))
    put(COL_SV, wt_ref[o_sq + SWA_Q_W + SWA_KV_W:, :])
    keep = lax.broadcasted_iota(jnp.int32, (LANES, tk), 0) < GLA_RANK
    put(COL_GA, jnp.where(keep, wt_ref[o_ga:o_ga + LANES, :], 0.0))
    o_ref[:, COL_GA + LANES:] = jnp.zeros((tk, PROJ_W - COL_GA - LANES), BF16)


def _wprep(wt, tk):
    n, k = wt.shape
    return pl.pallas_call(
        _wprep_kernel,
        out_shape=jax.ShapeDtypeStruct((k, PROJ_W), BF16),
        grid=(k // tk,),
        in_specs=[pl.BlockSpec((n, tk), lambda i: (0, i))],
        out_specs=pl.BlockSpec((tk, PROJ_W), lambda i: (i, 0)),
        compiler_params=pltpu.CompilerParams(
            dimension_semantics=("arbitrary",),
            vmem_limit_bytes=VMEM_LIMIT),
        name="wprep",
    )(wt)


def _inproj_kernel(x_ref, g_ref, w_ref, o_ref, u_ref, *, tm, rc):
    j = pl.program_id(1)

    @pl.when(j == 0)
    def _():
        for r in range(tm // rc):
            rs = pl.ds(r * rc, rc)
            u = _rms(x_ref[rs, :], g_ref[...]).astype(BF16)
            u_ref[rs, :] = u
            o_ref[rs, :] = jnp.dot(u, w_ref[...], preferred_element_type=F32).astype(BF16)

    @pl.when(j > 0)
    def _():
        o_ref[...] = jnp.dot(u_ref[...], w_ref[...], preferred_element_type=F32).astype(BF16)


def _inproj(x2, gain, w, tm, rc):
    t = x2.shape[0]
    return pl.pallas_call(
        functools.partial(_inproj_kernel, tm=tm, rc=rc),
        out_shape=jax.ShapeDtypeStruct((t, PROJ_W), BF16),
        grid=(t // tm, PROJ_W // PROJ_TN),
        in_specs=[
            pl.BlockSpec((tm, D_MODEL), lambda i, j: (i, 0)),
            pl.BlockSpec((1, D_MODEL), lambda i, j: (0, 0)),
            pl.BlockSpec((D_MODEL, PROJ_TN), lambda i, j: (0, j)),
        ],
        out_specs=pl.BlockSpec((tm, PROJ_TN), lambda i, j: (i, j)),
        scratch_shapes=[pltpu.VMEM((tm, D_MODEL), BF16)],
        compiler_params=pltpu.CompilerParams(
            dimension_semantics=("arbitrary", "arbitrary"),
            vmem_limit_bytes=VMEM_LIMIT),
        name="inproj",
    )(x2, gain, w)


def _gla_kernel(q_ref, k_ref, v_ref, g_ref, a_ref, wa_ref, ba_ref, gn_ref, tril_ref, mask_ref, *rest, ts):
    c = GLA_CHUNK
    nc = ts // c
    n_cast = (len(rest) - 2) // 2
    o_ref, s_ref = rest[n_cast], rest[-1]
    _cast_slabs(rest[:n_cast], rest[n_cast + 1:-1])
    heads = range(GLA_HEADS)
    ks = [slice(h * GLA_DK, (h + 1) * GLA_DK) for h in heads]
    vs = [slice(h * GLA_DV, (h + 1) * GLA_DV) for h in heads]

    @pl.when(pl.program_id(1) == 0)
    def _():
        s_ref[...] = jnp.zeros_like(s_ref)

    z = jnp.dot(a_ref[...], wa_ref[...], preferred_element_type=F32) + ba_ref[...]
    log_a = (jnp.minimum(z, 0.0) - jnp.log(1.0 + jnp.exp(-jnp.abs(z)))) * (1.0 / GLA_GATE_NORM)

    la_hi = log_a.astype(BF16)
    la_lo = (log_a - la_hi.astype(F32)).astype(BF16)
    la_hl = jnp.concatenate([la_hi, la_lo], axis=1)
    subs = [slice(s * GLA_SUB, (s + 1) * GLA_SUB) for s in range(ts // GLA_SUB)]
    cum2 = jnp.concatenate([jnp.dot(tril_ref[...], la_hl[rs], preferred_element_type=F32) for rs in subs], axis=0)
    bcum = cum2[:, :GLA_QK_W] + cum2[:, GLA_QK_W:]
    bcum3 = bcum.reshape(nc, c, GLA_QK_W)
    b_last = bcum3[:, c - 1:c, :]

    q = q_ref[...].astype(F32) * (GLA_DK ** -0.5)
    k = k_ref[...].astype(F32)
    q_e = (q * jnp.exp(bcum)).astype(BF16)
    k_e = (k * jnp.exp(-bcum)).astype(BF16)
    k_d = (k * jnp.exp(b_last - bcum3).reshape(ts, GLA_QK_W)).astype(BF16)
    decay = jnp.exp(b_last)
    v = [v_ref[:, vs[h]] for h in heads]

    keep = mask_ref[...] != 0.0

    for h in heads:
        o_intra = []
        for rs in subs:
            scores = lax.dot_general(q_e[rs, ks[h]], k_e[rs, ks[h]], _NT, preferred_element_type=F32)
            scores = jnp.where(keep, scores, 0.0).astype(BF16)
            o_intra.append(jnp.dot(scores, v[h][rs], preferred_element_type=F32))
        o_intra = jnp.concatenate(o_intra, axis=0)

        state_t = s_ref[h]
        o_inter = []
        for n in range(nc):
            rows = slice(n * c, (n + 1) * c)
            o_inter.append(lax.dot_general(q_e[rows, ks[h]], state_t.astype(BF16), _NT,
                                           preferred_element_type=F32))
            kv_t = lax.dot_general(v[h][rows], k_d[rows, ks[h]], _TN, preferred_element_type=F32)
            state_t = state_t * decay[n][:, ks[h]] + kv_t
        s_ref[h] = state_t

        o = _rms(o_intra + jnp.concatenate(o_inter, axis=0), gn_ref[...])
        gate = g_ref[:, vs[h]].astype(F32)
        o_ref[:, vs[h]] = (o * (gate * jax.nn.sigmoid(gate))).astype(BF16)


def _gla(proj, wa, ba, gn, batch, seq, ts, to_cast):
    nt = seq // ts
    cast_in, cast_out, cast_shapes = _cast_specs(to_cast, batch * nt, lambda b, i: b * nt + i)
    kern = functools.partial(_gla_kernel, ts=ts)
    rows = lambda b, i: b * nt + i
    idx = jnp.arange(GLA_SUB)
    causal = (idx[:, None] >= idx[None, :]) & (idx[:, None] // GLA_CHUNK == idx[None, :] // GLA_CHUNK)
    return pl.pallas_call(
        kern,
        out_shape=[jax.ShapeDtypeStruct((batch * seq, GLA_V_W), BF16)] + cast_shapes,
        grid=(batch, nt),
        in_specs=[
            pl.BlockSpec((ts, GLA_QK_W), lambda b, i: (rows(b, i), COL_GQ // GLA_QK_W)),
            pl.BlockSpec((ts, GLA_QK_W), lambda b, i: (rows(b, i), COL_GK // GLA_QK_W)),
            pl.BlockSpec((ts, GLA_V_W), lambda b, i: (rows(b, i), COL_GV // GLA_V_W)),
            pl.BlockSpec((ts, GLA_V_W), lambda b, i: (rows(b, i), COL_GG // GLA_V_W)),
            pl.BlockSpec((ts, LANES), lambda b, i: (rows(b, i), COL_GA // LANES)),
            pl.BlockSpec((LANES, GLA_QK_W), lambda b, i: (0, 0)),
            pl.BlockSpec((1, GLA_QK_W), lambda b, i: (0, 0)),
            pl.BlockSpec((1, GLA_DV), lambda b, i: (0, 0)),
            pl.BlockSpec((GLA_SUB, GLA_SUB), lambda b, i: (0, 0)),
            pl.BlockSpec((GLA_SUB, GLA_SUB), lambda b, i: (0, 0)),
        ] + cast_in,
        out_specs=[pl.BlockSpec((ts, GLA_V_W), lambda b, i: (rows(b, i), 0))] + cast_out,
        scratch_shapes=[pltpu.VMEM((GLA_HEADS, GLA_DV, GLA_DK), F32)],
        compiler_params=pltpu.CompilerParams(
            dimension_semantics=("arbitrary", "arbitrary"),
            vmem_limit_bytes=VMEM_LIMIT),
        name="gla",
    )(proj, proj, proj, proj, proj, wa, ba, gn, causal.astype(BF16), causal.astype(F32), *to_cast)


def _swa_kernel(sinks_ref, q_ref, k_ref, v_ref, pos_ref, posn_ref, freq_ref, eye_ref, *rest):
    n_cast = (len(rest) - 5) // 2
    o_ref = rest[n_cast]
    k2p_ref, vtp_ref, cos_ref, sin_ref = rest[-4:]
    _cast_slabs(rest[:n_cast], rest[n_cast + 1:-4])
    nb = pl.program_id(1)
    wb = WINDOW
    hd = SWA_HEAD_DIM
    half = hd // 2
    pairs_per_kv = SWA_Q_W // LANES // SWA_KV_HEADS
    lane = lax.broadcasted_iota(jnp.int32, (wb, LANES), 1)
    head_a = (lane % hd) < half
    low = lane < hd

    def rotary_tables(pos_row):
        ang_t = freq_ref[...] * pos_row.astype(F32)
        reps = LANES // half
        cs = jnp.concatenate([jnp.cos(ang_t)] * reps, axis=0).T
        sn = jnp.concatenate([jnp.sin(ang_t)] * reps, axis=0).T
        return cs, jnp.where(low, -sn, sn)

    @pl.when(nb == 0)
    def _():
        k2p_ref[...] = jnp.zeros_like(k2p_ref)
        vtp_ref[...] = jnp.zeros_like(vtp_ref)
        cos_ref[...], sin_ref[...] = rotary_tables(pos_ref[0])

    cos = cos_ref[...]
    sin = sin_ref[...]

    def rope(t, cs, sn):
        return t * cs + pltpu.roll(t, hd, 1) * sn

    k_r = rope(k_ref[...].astype(F32), cos, sin)
    k2_cur = [jnp.where(head_a, k_r, pltpu.roll(k_r, half, 1)).astype(BF16),
              jnp.where(head_a, pltpu.roll(k_r, LANES - half, 1), k_r).astype(BF16)]
    v_t = v_ref[...].astype(F32).T
    ones_rows = (lax.broadcasted_iota(jnp.int32, (SWA_VT_ROWS - hd, wb), 0) == 0).astype(F32)
    vt_cur = [jnp.concatenate([v_t[c * hd:(c + 1) * hd], ones_rows], axis=0).astype(BF16)
              for c in range(SWA_KV_HEADS)]
    vt = [jnp.concatenate([vtp_ref[c], vt_cur[c]], axis=1) for c in range(SWA_KV_HEADS)]

    ki = lax.broadcasted_iota(jnp.int32, (2 * wb, wb), 0)
    qi = lax.broadcasted_iota(jnp.int32, (2 * wb, wb), 1) + wb
    mask = (ki <= qi) & (qi - ki < WINDOW) & ((nb > 0) | (ki >= wb))
    bias = jnp.where(mask, 0.0, NEG_INF).astype(BF16)
    k2 = [jnp.concatenate([k2p_ref[c], k2_cur[c]], axis=0) for c in range(SWA_KV_HEADS)]
    k2 = [jnp.concatenate([k2[c], bias], axis=1) for c in range(SWA_KV_HEADS)]

    qscale = (hd ** -0.5) * LOG2E
    cos_q = cos * qscale
    sin_q = sin * qscale

    n_tiles = SWA_Q_W // LANES
    lhs = []
    for t in range(n_tiles):
        q_r = rope(q_ref[:, t * LANES:(t + 1) * LANES].astype(F32), cos_q, sin_q)
        lhs += [jnp.where(head_a, q_r, 0.0), jnp.where(head_a, 0.0, q_r)]
    cols_per_kv = 2 * pairs_per_kv * wb
    st = []
    for c in range(SWA_KV_HEADS):
        lhs_c = jnp.concatenate(lhs[2 * c * pairs_per_kv:2 * (c + 1) * pairs_per_kv], axis=0).astype(BF16)
        lhs_c = jnp.concatenate([lhs_c, eye_ref[...]], axis=1)
        st.append(lax.dot_general(k2[c], lhs_c, _NT, preferred_element_type=F32))
    st = jnp.concatenate(st, axis=1)
    m = jnp.max(st, axis=0, keepdims=True)
    e = jnp.exp2(st - m).astype(BF16)
    sink = jnp.concatenate([jnp.full((1, wb), sinks_ref[i], F32) for i in range(SWA_HEADS)], axis=1) * LOG2E
    sink_term = jnp.exp2(sink - m)
    for c in range(SWA_KV_HEADS):
        cols = slice(c * cols_per_kv, (c + 1) * cols_per_kv)
        ot = jnp.dot(vt[c], e[:, cols], preferred_element_type=F32)
        o_n = ot[:hd] * (1.0 / (ot[hd:hd + 1] + sink_term[:, cols]))
        for n in range(pairs_per_kv):
            t = c * pairs_per_kv + n
            pair = jnp.concatenate([o_n[:, 2 * n * wb:(2 * n + 1) * wb], o_n[:, (2 * n + 1) * wb:(2 * n + 2) * wb]],
                                   axis=0)
            o_ref[:, t * LANES:(t + 1) * LANES] = pair.T.astype(BF16)

    for c in range(SWA_KV_HEADS):
        k2p_ref[c] = k2_cur[c]
        vtp_ref[c] = vt_cur[c]
    cos_ref[...], sin_ref[...] = rotary_tables(posn_ref[0])


def _swa(proj, sinks, pos3, freq_col, batch, seq, to_cast):
    nb = seq // WINDOW
    cast_in, cast_out, cast_shapes = _cast_specs(to_cast, batch * nb, lambda b, n: b * nb + n)
    eye = jnp.tile(jnp.eye(WINDOW, dtype=BF16), (SWA_HEADS // SWA_KV_HEADS, 1))
    rows = lambda b, n: b * nb + n
    return pl.pallas_call(
        _swa_kernel,
        out_shape=[jax.ShapeDtypeStruct((batch * seq, SWA_Q_W), BF16)] + cast_shapes,
        grid=(batch, nb),
        in_specs=[
            pl.BlockSpec(memory_space=pltpu.SMEM),
            pl.BlockSpec((WINDOW, SWA_Q_W), lambda b, n: (rows(b, n), COL_SQ // SWA_Q_W)),
            pl.BlockSpec((WINDOW, SWA_KV_W), lambda b, n: (rows(b, n), COL_SK // SWA_KV_W)),
            pl.BlockSpec((WINDOW, SWA_KV_W), lambda b, n: (rows(b, n), COL_SV // SWA_KV_W)),
            pl.BlockSpec((1, 1, WINDOW), lambda b, n: (rows(b, n), 0, 0)),
            pl.BlockSpec((1, 1, WINDOW), lambda b, n: (rows(b, jnp.minimum(n + 1, nb - 1)), 0, 0)),
            pl.BlockSpec((SWA_HEAD_DIM // 2, 1), lambda b, n: (0, 0)),
            pl.BlockSpec((SWA_HEADS // SWA_KV_HEADS * WINDOW, WINDOW), lambda b, n: (0, 0)),
        ] + cast_in,
        out_specs=[pl.BlockSpec((WINDOW, SWA_Q_W), lambda b, n: (rows(b, n), 0))] + cast_out,
        scratch_shapes=[pltpu.VMEM((SWA_KV_HEADS, WINDOW, LANES), BF16),
                        pltpu.VMEM((SWA_KV_HEADS, SWA_VT_ROWS, WINDOW), BF16),
                        pltpu.VMEM((WINDOW, LANES), F32), pltpu.VMEM((WINDOW, LANES), F32)],
        compiler_params=pltpu.CompilerParams(
            dimension_semantics=("arbitrary", "arbitrary"),
            vmem_limit_bytes=VMEM_LIMIT),
        name="swa",
    )(sinks, proj, proj, proj, pos3, pos3, freq_col, eye, *to_cast)


def _outproj_kernel(x_ref, a_ref, b_ref, wa_ref, wb_ref, h_ref):
    acc = jnp.dot(a_ref[...], wa_ref[...], preferred_element_type=F32)
    acc = acc + jnp.dot(b_ref[...], wb_ref[...], preferred_element_type=F32)
    h_ref[...] = x_ref[...] + acc


def _outproj(x2, o_gla, o_swa, w, tm):
    t = x2.shape[0]
    return pl.pallas_call(
        _outproj_kernel,
        out_shape=jax.ShapeDtypeStruct((t, D_MODEL), F32),
        grid=(t // tm,),
        in_specs=[
            pl.BlockSpec((tm, D_MODEL), lambda i: (i, 0)),
            pl.BlockSpec((tm, GLA_V_W), lambda i: (i, 0)),
            pl.BlockSpec((tm, SWA_Q_W), lambda i: (i, 0)),
            pl.BlockSpec((GLA_V_W, D_MODEL), lambda i: (0, 0)),
            pl.BlockSpec((SWA_Q_W, D_MODEL), lambda i: (GLA_V_W // SWA_Q_W, 0)),
        ],
        out_specs=pl.BlockSpec((tm, D_MODEL), lambda i: (i, 0)),
        compiler_params=pltpu.CompilerParams(
            dimension_semantics=("arbitrary",),
            vmem_limit_bytes=VMEM_LIMIT),
        name="outproj",
    )(x2, o_gla, o_swa, w, w)


def _ffn_kernel(h_ref, gf_ref, wg_ref, wu_ref, cw_ref, cb_ref, wd_ref, gl_ref, o_ref,
                hn_ref, carry_ref, *, tm, rc, seq):
    i = pl.program_id(0)
    j = pl.program_id(1)
    nj = pl.num_programs(1)

    def step(first, last):
        seq_start = (i * tm) % seq == 0
        prev = jnp.where(seq_start, 0.0, carry_ref[j])
        rows = lax.broadcasted_iota(jnp.int32, prev.shape, 0)
        cb = cb_ref[...]
        cw0, cw1, cw2 = cw_ref[0:1, :], cw_ref[1:2, :], cw_ref[2:3, :]

        for r in range(tm // rc):
            rs = pl.ds(r * rc, rc)
            if first:
                res = h_ref[rs, :]
                hn = _rms(res, gf_ref[...]).astype(BF16)
                hn_ref[rs, :] = hn
            else:
                res = o_ref[rs, :]
                hn = hn_ref[rs, :]
            gate = jnp.dot(hn, wg_ref[...], preferred_element_type=F32)
            up = jnp.dot(hn, wu_ref[...], preferred_element_type=F32)

            def shifted(d):
                rolled = pltpu.roll(gate, d, 0)
                top = jnp.where(rows < d, pltpu.roll(prev, d, 0), rolled[:SUBLANES])
                return jnp.concatenate([top, rolled[SUBLANES:]], axis=0)

            conv = cb + cw0 * shifted(2) + cw1 * shifted(1) + cw2 * gate
            act = (conv * jax.nn.sigmoid(conv) * up).astype(BF16)
            acc = res + jnp.dot(act, wd_ref[...], preferred_element_type=F32)
            o_ref[rs, :] = _rms(acc, gl_ref[...]) if last else acc
            prev = gate[rc - SUBLANES:, :]
        carry_ref[j] = prev

    pl.when(j == 0)(functools.partial(step, True, False))
    pl.when((j > 0) & (j < nj - 1))(functools.partial(step, False, False))
    pl.when(j == nj - 1)(functools.partial(step, False, True))


def _ffn(h, gf, wg, wu, cw, cb, wd, gl, seq, tm, tf, rc):
    t = h.shape[0]
    nj = D_FF // tf
    kern = functools.partial(_ffn_kernel, tm=tm, rc=rc, seq=seq)
    return pl.pallas_call(
        kern,
        out_shape=jax.ShapeDtypeStruct((t, D_MODEL), F32),
        grid=(t // tm, nj),
        in_specs=[
            pl.BlockSpec((tm, D_MODEL), lambda i, j: (i, 0)),
            pl.BlockSpec((1, D_MODEL), lambda i, j: (0, 0)),
            pl.BlockSpec((D_MODEL, tf), lambda i, j: (0, j)),
            pl.BlockSpec((D_MODEL, tf), lambda i, j: (0, j)),
            pl.BlockSpec((CONV_WIDTH, tf), lambda i, j: (0, j)),
            pl.BlockSpec((1, tf), lambda i, j: (0, j)),
            pl.BlockSpec((tf, D_MODEL), lambda i, j: (j, 0)),
            pl.BlockSpec((1, D_MODEL), lambda i, j: (0, 0)),
        ],
        out_specs=pl.BlockSpec((tm, D_MODEL), lambda i, j: (i, 0)),
        scratch_shapes=[pltpu.VMEM((tm, D_MODEL), BF16), pltpu.VMEM((nj, SUBLANES, tf), F32)],
        compiler_params=pltpu.CompilerParams(
            dimension_semantics=("arbitrary", "arbitrary"),
            vmem_limit_bytes=VMEM_LIMIT),
        name="convffn",
    )(h, gf, wg, wu, cw, cb, wd, gl)


def kernel(x, positions, attn_norm, w_in, w_a_up, b_a_up, gla_norm, sinks, w_out, ffn_norm,
           w_gate, w_up, conv_w, conv_b, w_down, final_norm):
    batch, seq, _ = x.shape
    t = batch * seq
    assert w_in.shape[0] == 1, "the final norm is fused into the single layer's FFN kernel"
    x2 = x.reshape(t, D_MODEL)
    pos3 = positions.reshape(t // WINDOW, 1, WINDOW)
    half = SWA_HEAD_DIM // 2
    freq_col = (ROPE_THETA ** (-jnp.arange(half, dtype=F32) / half)).reshape(half, 1)

    w_proj = _wprep(jnp.swapaxes(w_in[0], 0, 1), tk=256)
    wa = jnp.concatenate(
        [w_a_up[0], jnp.zeros((LANES - GLA_RANK, GLA_QK_W), F32)], axis=0).astype(BF16)

    proj = _inproj(x2, attn_norm[0].reshape(1, D_MODEL), w_proj, tm=1024, rc=256)
    o_gla, wd_bf, wo_bf = _gla(proj, wa, b_a_up[0].reshape(1, GLA_QK_W), gla_norm[0].reshape(1, GLA_DV),
                               batch, seq, ts=1024, to_cast=[w_down[0], w_out[0]])
    o_swa, wg_bf, wu_bf = _swa(proj, sinks[0], pos3, freq_col, batch, seq, to_cast=[w_gate[0], w_up[0]])
    h = _outproj(x2, o_gla, o_swa, wo_bf, tm=512)
    y = _ffn(h, ffn_norm[0].reshape(1, D_MODEL), wg_bf, wu_bf, conv_w[0], conv_b[0].reshape(1, D_FF), wd_bf,
             final_norm.reshape(1, D_MODEL), seq, tm=1024, tf=512, rc=512)
    return y.reshape(batch, seq, D_MODEL)
```

```python
import functools

import jax
import jax.numpy as jnp
from jax import lax
from jax.experimental import pallas as pl
from jax.experimental.pallas import tpu as pltpu

D_MODEL = 2048
GLA_HEADS = 4
GLA_DK = 128
GLA_DV = 256
GLA_RANK = 16
GLA_GATE_NORM = 16.0
GLA_CHUNK = 64
GLA_SUB = 256
SWA_HEADS = 16
SWA_KV_HEADS = 2
SWA_HEAD_DIM = 64
WINDOW = 128
ROPE_THETA = 10000.0
D_FF = 5632
CONV_WIDTH = 3
EPS = 1e-6
NEG_INF = -1e30
LOG2E = 1.4426950408889634
SWA_VT_ROWS = SWA_HEAD_DIM + 16

GLA_QK_W = GLA_HEADS * GLA_DK
GLA_V_W = GLA_HEADS * GLA_DV
SWA_Q_W = SWA_HEADS * SWA_HEAD_DIM
SWA_KV_W = SWA_KV_HEADS * SWA_HEAD_DIM

LANES = 128
SUBLANES = 8

COL_GQ = 0
COL_GK = COL_GQ + GLA_QK_W
COL_GV = COL_GK + GLA_QK_W
COL_GG = COL_GV + GLA_V_W
COL_SQ = COL_GG + GLA_V_W
COL_SK = COL_SQ + SWA_Q_W
COL_SV = COL_SK + SWA_KV_W
COL_GA = COL_SV + SWA_KV_W
PROJ_TN = 1536
PROJ_W = 3 * PROJ_TN

VMEM_LIMIT = 58 * 1024 * 1024

F32 = jnp.float32
BF16 = jnp.bfloat16

_NT = (((1,), (1,)), ((), ()))
_TN = (((0,), (0,)), ((), ()))


def _rms(x, gain):
    return x * lax.rsqrt(jnp.mean(x * x, axis=-1, keepdims=True) + EPS) * gain


def _cast_specs(weights, steps, step_index):
    in_specs, out_specs, out_shapes = [], [], []
    for w in weights:
        rows, cols = w.shape
        slab = rows // steps
        assert slab * steps == rows and slab % (2 * SUBLANES) == 0, (w.shape, steps)
        spec = pl.BlockSpec((slab, cols), lambda *g: (step_index(*g), 0))
        in_specs.append(spec)
        out_specs.append(spec)
        out_shapes.append(jax.ShapeDtypeStruct(w.shape, BF16))
    return in_specs, out_specs, out_shapes


def _cast_slabs(src_refs, dst_refs):
    for src, dst in zip(src_refs, dst_refs):
        dst[...] = src[...].astype(BF16)


def _wprep_kernel(wt_ref, o_ref):
    tk = wt_ref.shape[1]
    half = SWA_HEAD_DIM // 2
    o_ga = 2 * GLA_QK_W + 2 * GLA_V_W
    o_sq = o_ga + GLA_RANK

    def put(col, rows):
        o_ref[:, col:col + rows.shape[0]] = rows.T.astype(BF16)

    put(0, wt_ref[:o_ga, :])
    for i in range((SWA_Q_W + SWA_KV_W) // LANES):
        r0 = o_sq + i * LANES
        blk = [wt_ref[r0 + j * half:r0 + (j + 1) * half, :] for j in range(4)]
        put(COL_SQ + i * LANES, jnp.concatenate([blk[0], blk[2], blk[1], blk[3]], axis=0))
    put(COL_SV, wt_ref[o_sq + SWA_Q_W + SWA_KV_W:, :])
    keep = lax.broadcasted_iota(jnp.int32, (LANES, tk), 0) < GLA_RANK
    put(COL_GA, jnp.where(keep, wt_ref[o_ga:o_ga + LANES, :], 0.0))
    o_ref[:, COL_GA + LANES:] = jnp.zeros((tk, PROJ_W - COL_GA - LANES), BF16)


def _wprep(wt, tk):
    n, k = wt.shape
    return pl.pallas_call(
        _wprep_kernel,
        out_shape=jax.ShapeDtypeStruct((k, PROJ_W), BF16),
        grid=(k // tk,),
        in_specs=[pl.BlockSpec((n, tk), lambda i: (0, i))],
        out_specs=pl.BlockSpec((tk, PROJ_W), lambda i: (i, 0)),
        compiler_params=pltpu.CompilerParams(
            dimension_semantics=("arbitrary",),
            vmem_limit_bytes=VMEM_LIMIT),
        name="wprep",
    )(wt)


def _inproj_kernel(x_ref, g_ref, w_ref, *rest, tm, rc):
    n_cast = (len(rest) - 2) // 2
    o_ref, u_ref = rest[n_cast], rest[-1]
    j = pl.program_id(1)

    @pl.when(j == 0)
    def _():
        for r in range(tm // rc):
            rs = pl.ds(r * rc, rc)
            u = _rms(x_ref[rs, :], g_ref[...]).astype(BF16)
            u_ref[rs, :] = u
            o_ref[rs, :] = jnp.dot(u, w_ref[...], preferred_element_type=F32).astype(BF16)

    @pl.when(j > 0)
    def _():
        o_ref[...] = jnp.dot(u_ref[...], w_ref[...], preferred_element_type=F32).astype(BF16)
        _cast_slabs(rest[:n_cast], rest[n_cast + 1:-1])


def _inproj(x2, gain, w, tm, rc, to_cast):
    t = x2.shape[0]
    nj = PROJ_W // PROJ_TN
    cast_in, cast_out, cast_shapes = _cast_specs(
        to_cast, t // tm * (nj - 1), lambda i, j: (nj - 1) * i + jnp.maximum(j, 1) - 1)
    return pl.pallas_call(
        functools.partial(_inproj_kernel, tm=tm, rc=rc),
        out_shape=[jax.ShapeDtypeStruct((t, PROJ_W), BF16)] + cast_shapes,
        grid=(t // tm, nj),
        in_specs=[
            pl.BlockSpec((tm, D_MODEL), lambda i, j: (i, 0)),
            pl.BlockSpec((1, D_MODEL), lambda i, j: (0, 0)),
            pl.BlockSpec((D_MODEL, PROJ_TN), lambda i, j: (0, j)),
        ] + cast_in,
        out_specs=[pl.BlockSpec((tm, PROJ_TN), lambda i, j: (i, j))] + cast_out,
        scratch_shapes=[pltpu.VMEM((tm, D_MODEL), BF16)],
        compiler_params=pltpu.CompilerParams(
            dimension_semantics=("arbitrary", "arbitrary"),
            vmem_limit_bytes=VMEM_LIMIT),
        name="inproj",
    )(x2, gain, w, *to_cast)


def _gla_kernel(q_ref, k_ref, v_ref, g_ref, a_ref, wa_ref, ba_ref, gn_ref, tril_ref, mask_ref, *rest, ts):
    c = GLA_CHUNK
    nc = ts // c
    n_cast = (len(rest) - 2) // 2
    o_ref, s_ref = rest[n_cast], rest[-1]
    _cast_slabs(rest[:n_cast], rest[n_cast + 1:-1])
    heads = range(GLA_HEADS)
    ks = [slice(h * GLA_DK, (h + 1) * GLA_DK) for h in heads]
    vs = [slice(h * GLA_DV, (h + 1) * GLA_DV) for h in heads]

    @pl.when(pl.program_id(1) == 0)
    def _():
        s_ref[...] = jnp.zeros_like(s_ref)

    z = jnp.dot(a_ref[...], wa_ref[...], preferred_element_type=F32) + ba_ref[...]
    log_a = (jnp.minimum(z, 0.0) - jnp.log(1.0 + jnp.exp(-jnp.abs(z)))) * (1.0 / GLA_GATE_NORM)

    la_hi = log_a.astype(BF16)
    la_lo = (log_a - la_hi.astype(F32)).astype(BF16)
    la_hl = jnp.concatenate([la_hi, la_lo], axis=1)
    subs = [slice(s * GLA_SUB, (s + 1) * GLA_SUB) for s in range(ts // GLA_SUB)]
    cum2 = jnp.concatenate([jnp.dot(tril_ref[...], la_hl[rs], preferred_element_type=F32) for rs in subs], axis=0)
    bcum = cum2[:, :GLA_QK_W] + cum2[:, GLA_QK_W:]
    bcum3 = bcum.reshape(nc, c, GLA_QK_W)
    b_last = bcum3[:, c - 1:c, :]

    q = q_ref[...].astype(F32) * (GLA_DK ** -0.5)
    k = k_ref[...].astype(F32)
    q_e = (q * jnp.exp(bcum)).astype(BF16)
    k_e = (k * jnp.exp(-bcum)).astype(BF16)
    k_d = (k * jnp.exp(b_last - bcum3).reshape(ts, GLA_QK_W)).astype(BF16)
    decay = jnp.exp(b_last)
    v = [v_ref[:, vs[h]] for h in heads]

    keep = mask_ref[...] != 0.0

    for h in heads:
        o_intra = []
        for rs in subs:
            scores = lax.dot_general(q_e[rs, ks[h]], k_e[rs, ks[h]], _NT, preferred_element_type=F32)
            scores = jnp.where(keep, scores, 0.0).astype(BF16)
            o_intra.append(jnp.dot(scores, v[h][rs], preferred_element_type=F32))
        o_intra = jnp.concatenate(o_intra, axis=0)

        state_t = s_ref[h]
        o_inter = []
        for n in range(nc):
            rows = slice(n * c, (n + 1) * c)
            o_inter.append(lax.dot_general(q_e[rows, ks[h]], state_t.astype(BF16), _NT,
                                           preferred_element_type=F32))
            kv_t = lax.dot_general(v[h][rows], k_d[rows, ks[h]], _TN, preferred_element_type=F32)
            state_t = state_t * decay[n][:, ks[h]] + kv_t
        s_ref[h] = state_t

        o = _rms(o_intra + jnp.concatenate(o_inter, axis=0), gn_ref[...])
        gate = g_ref[:, vs[h]].astype(F32)
        o_ref[:, vs[h]] = (o * (gate * jax.nn.sigmoid(gate))).astype(BF16)


def _gla(proj, wa, ba, gn, batch, seq, ts, to_cast):
    nt = seq // ts
    cast_in, cast_out, cast_shapes = _cast_specs(to_cast, batch * nt, lambda b, i: b * nt + i)
    kern = functools.partial(_gla_kernel, ts=ts)
    rows = lambda b, i: b * nt + i
    idx = jnp.arange(GLA_SUB)
    causal = (idx[:, None] >= idx[None, :]) & (idx[:, None] // GLA_CHUNK == idx[None, :] // GLA_CHUNK)
    return pl.pallas_call(
        kern,
        out_shape=[jax.ShapeDtypeStruct((batch * seq, GLA_V_W), BF16)] + cast_shapes,
        grid=(batch, nt),
        in_specs=[
            pl.BlockSpec((ts, GLA_QK_W), lambda b, i: (rows(b, i), COL_GQ // GLA_QK_W)),
            pl.BlockSpec((ts, GLA_QK_W), lambda b, i: (rows(b, i), COL_GK // GLA_QK_W)),
            pl.BlockSpec((ts, GLA_V_W), lambda b, i: (rows(b, i), COL_GV // GLA_V_W)),
            pl.BlockSpec((ts, GLA_V_W), lambda b, i: (rows(b, i), COL_GG // GLA_V_W)),
            pl.BlockSpec((ts, LANES), lambda b, i: (rows(b, i), COL_GA // LANES)),
            pl.BlockSpec((LANES, GLA_QK_W), lambda b, i: (0, 0)),
            pl.BlockSpec((1, GLA_QK_W), lambda b, i: (0, 0)),
            pl.BlockSpec((1, GLA_DV), lambda b, i: (0, 0)),
            pl.BlockSpec((GLA_SUB, GLA_SUB), lambda b, i: (0, 0)),
            pl.BlockSpec((GLA_SUB, GLA_SUB), lambda b, i: (0, 0)),
        ] + cast_in,
        out_specs=[pl.BlockSpec((ts, GLA_V_W), lambda b, i: (rows(b, i), 0))] + cast_out,
        scratch_shapes=[pltpu.VMEM((GLA_HEADS, GLA_DV, GLA_DK), F32)],
        compiler_params=pltpu.CompilerParams(
            dimension_semantics=("arbitrary", "arbitrary"),
            vmem_limit_bytes=VMEM_LIMIT),
        name="gla",
    )(proj, proj, proj, proj, proj, wa, ba, gn, causal.astype(BF16), causal.astype(F32), *to_cast)


def _swa_kernel(sinks_ref, q_ref, k_ref, v_ref, pos_ref, posn_ref, freq_ref, eye_ref, *rest):
    n_cast = (len(rest) - 5) // 2
    o_ref = rest[n_cast]
    k2p_ref, vtp_ref, cos_ref, sin_ref = rest[-4:]
    _cast_slabs(rest[:n_cast], rest[n_cast + 1:-4])
    nb = pl.program_id(1)
    wb = WINDOW
    hd = SWA_HEAD_DIM
    half = hd // 2
    pairs_per_kv = SWA_Q_W // LANES // SWA_KV_HEADS
    lane = lax.broadcasted_iota(jnp.int32, (wb, LANES), 1)
    head_a = (lane % hd) < half
    low = lane < hd

    def rotary_tables(pos_row):
        ang_t = freq_ref[...] * pos_row.astype(F32)
        reps = LANES // half
        cs = jnp.concatenate([jnp.cos(ang_t)] * reps, axis=0).T
        sn = jnp.concatenate([jnp.sin(ang_t)] * reps, axis=0).T
        return cs, jnp.where(low, -sn, sn)

    @pl.when(nb == 0)
    def _():
        k2p_ref[...] = jnp.zeros_like(k2p_ref)
        vtp_ref[...] = jnp.zeros_like(vtp_ref)
        cos_ref[...], sin_ref[...] = rotary_tables(pos_ref[0])

    cos = cos_ref[...]
    sin = sin_ref[...]

    def rope(t, cs, sn):
        return t * cs + pltpu.roll(t, hd, 1) * sn

    k_r = rope(k_ref[...].astype(F32), cos, sin)
    k2_cur = [jnp.where(head_a, k_r, pltpu.roll(k_r, half, 1)).astype(BF16),
              jnp.where(head_a, pltpu.roll(k_r, LANES - half, 1), k_r).astype(BF16)]
    v_t = v_ref[...].astype(F32).T
    ones_rows = (lax.broadcasted_iota(jnp.int32, (SWA_VT_ROWS - hd, wb), 0) == 0).astype(F32)
    vt_cur = [jnp.concatenate([v_t[c * hd:(c + 1) * hd], ones_rows], axis=0).astype(BF16)
              for c in range(SWA_KV_HEADS)]
    vt = [jnp.concatenate([vtp_ref[c], vt_cur[c]], axis=1) for c in range(SWA_KV_HEADS)]

    ki = lax.broadcasted_iota(jnp.int32, (2 * wb, wb), 0)
    qi = lax.broadcasted_iota(jnp.int32, (2 * wb, wb), 1) + wb
    mask = (ki <= qi) & (qi - ki < WINDOW) & ((nb > 0) | (ki >= wb))
    bias = jnp.where(mask, 0.0, NEG_INF).astype(BF16)
    k2 = [jnp.concatenate([k2p_ref[c], k2_cur[c]], axis=0) for c in range(SWA_KV_HEADS)]
    k2 = [jnp.concatenate([k2[c], bias], axis=1) for c in range(SWA_KV_HEADS)]

    qscale = (hd ** -0.5) * LOG2E
    cos_q = cos * qscale
    sin_q = sin * qscale

    n_tiles = SWA_Q_W // LANES
    lhs = []
    for t in range(n_tiles):
        q_r = rope(q_ref[:, t * LANES:(t + 1) * LANES].astype(F32), cos_q, sin_q)
        lhs += [jnp.where(head_a, q_r, 0.0), jnp.where(head_a, 0.0, q_r)]
    cols_per_kv = 2 * pairs_per_kv * wb
    st = []
    for c in range(SWA_KV_HEADS):
        lhs_c = jnp.concatenate(lhs[2 * c * pairs_per_kv:2 * (c + 1) * pairs_per_kv], axis=0).astype(BF16)
        lhs_c = jnp.concatenate([lhs_c, eye_ref[...]], axis=1)
        st.append(lax.dot_general(k2[c], lhs_c, _NT, preferred_element_type=F32))
    st = jnp.concatenate(st, axis=1)
    m = jnp.max(st, axis=0, keepdims=True)
    e = jnp.exp2(st - m).astype(BF16)
    sink = jnp.concatenate([jnp.full((1, wb), sinks_ref[i], F32) for i in range(SWA_HEADS)], axis=1) * LOG2E
    sink_term = jnp.exp2(sink - m)
    for c in range(SWA_KV_HEADS):
        cols = slice(c * cols_per_kv, (c + 1) * cols_per_kv)
        ot = jnp.dot(vt[c], e[:, cols], preferred_element_type=F32)
        o_n = ot[:hd] * (1.0 / (ot[hd:hd + 1] + sink_term[:, cols]))
        for n in range(pairs_per_kv):
            t = c * pairs_per_kv + n
            pair = jnp.concatenate([o_n[:, 2 * n * wb:(2 * n + 1) * wb], o_n[:, (2 * n + 1) * wb:(2 * n + 2) * wb]],
                                   axis=0)
            o_ref[:, t * LANES:(t + 1) * LANES] = pair.T.astype(BF16)

    for c in range(SWA_KV_HEADS):
        k2p_ref[c] = k2_cur[c]
        vtp_ref[c] = vt_cur[c]
    cos_ref[...], sin_ref[...] = rotary_tables(posn_ref[0])


def _swa(proj, sinks, pos3, freq_col, batch, seq, to_cast):
    nb = seq // WINDOW
    cast_in, cast_out, cast_shapes = _cast_specs(to_cast, batch * nb, lambda b, n: b * nb + n)
    eye = jnp.tile(jnp.eye(WINDOW, dtype=BF16), (SWA_HEADS // SWA_KV_HEADS, 1))
    rows = lambda b, n: b * nb + n
    return pl.pallas_call(
        _swa_kernel,
        out_shape=[jax.ShapeDtypeStruct((batch * seq, SWA_Q_W), BF16)] + cast_shapes,
        grid=(batch, nb),
        in_specs=[
            pl.BlockSpec(memory_space=pltpu.SMEM),
            pl.BlockSpec((WINDOW, SWA_Q_W), lambda b, n: (rows(b, n), COL_SQ // SWA_Q_W)),
            pl.BlockSpec((WINDOW, SWA_KV_W), lambda b, n: (rows(b, n), COL_SK // SWA_KV_W)),
            pl.BlockSpec((WINDOW, SWA_KV_W), lambda b, n: (rows(b, n), COL_SV // SWA_KV_W)),
            pl.BlockSpec((1, 1, WINDOW), lambda b, n: (rows(b, n), 0, 0)),
            pl.BlockSpec((1, 1, WINDOW), lambda b, n: (rows(b, jnp.minimum(n + 1, nb - 1)), 0, 0)),
            pl.BlockSpec((SWA_HEAD_DIM // 2, 1), lambda b, n: (0, 0)),
            pl.BlockSpec((SWA_HEADS // SWA_KV_HEADS * WINDOW, WINDOW), lambda b, n: (0, 0)),
        ] + cast_in,
        out_specs=[pl.BlockSpec((WINDOW, SWA_Q_W), lambda b, n: (rows(b, n), 0))] + cast_out,
        scratch_shapes=[pltpu.VMEM((SWA_KV_HEADS, WINDOW, LANES), BF16),
                        pltpu.VMEM((SWA_KV_HEADS, SWA_VT_ROWS, WINDOW), BF16),
                        pltpu.VMEM((WINDOW, LANES), F32), pltpu.VMEM((WINDOW, LANES), F32)],
        compiler_params=pltpu.CompilerParams(
            dimension_semantics=("arbitrary", "arbitrary"),
            vmem_limit_bytes=VMEM_LIMIT),
        name="swa",
    )(sinks, proj, proj, proj, pos3, pos3, freq_col, eye, *to_cast)


def _outproj_kernel(x_ref, a_ref, b_ref, wa_ref, wb_ref, h_ref):
    acc = jnp.dot(a_ref[...], wa_ref[...], preferred_element_type=F32)
    acc = acc + jnp.dot(b_ref[...], wb_ref[...], preferred_element_type=F32)
    h_ref[...] = x_ref[...] + acc


def _outproj(x2, o_gla, o_swa, w, tm):
    t = x2.shape[0]
    return pl.pallas_call(
        _outproj_kernel,
        out_shape=jax.ShapeDtypeStruct((t, D_MODEL), F32),
        grid=(t // tm,),
        in_specs=[
            pl.BlockSpec((tm, D_MODEL), lambda i: (i, 0)),
            pl.BlockSpec((tm, GLA_V_W), lambda i: (i, 0)),
            pl.BlockSpec((tm, SWA_Q_W), lambda i: (i, 0)),
            pl.BlockSpec((GLA_V_W, D_MODEL), lambda i: (0, 0)),
            pl.BlockSpec((SWA_Q_W, D_MODEL), lambda i: (GLA_V_W // SWA_Q_W, 0)),
        ],
        out_specs=pl.BlockSpec((tm, D_MODEL), lambda i: (i, 0)),
        compiler_params=pltpu.CompilerParams(
            dimension_semantics=("arbitrary",),
            vmem_limit_bytes=VMEM_LIMIT),
        name="outproj",
    )(x2, o_gla, o_swa, w, w)


def _ffn_kernel(h_ref, gf_ref, wg_ref, wu_ref, cw_ref, cb_ref, wd_ref, gl_ref, o_ref,
                hn_ref, carry_ref, *, tm, rc, seq):
    i = pl.program_id(0)
    j = pl.program_id(1)
    nj = pl.num_programs(1)

    def step(first, last):
        seq_start = (i * tm) % seq == 0
        prev = jnp.where(seq_start, 0.0, carry_ref[j])
        rows = lax.broadcasted_iota(jnp.int32, prev.shape, 0)
        cb = cb_ref[...]
        cw0, cw1, cw2 = cw_ref[0:1, :], cw_ref[1:2, :], cw_ref[2:3, :]

        for r in range(tm // rc):
            rs = pl.ds(r * rc, rc)
            if first:
                res = h_ref[rs, :]
                hn = _rms(res, gf_ref[...]).astype(BF16)
                hn_ref[rs, :] = hn
            else:
                res = o_ref[rs, :]
                hn = hn_ref[rs, :]
            gate = jnp.dot(hn, wg_ref[...], preferred_element_type=F32)
            up = jnp.dot(hn, wu_ref[...], preferred_element_type=F32)

            def shifted(d):
                rolled = pltpu.roll(gate, d, 0)
                top = jnp.where(rows < d, pltpu.roll(prev, d, 0), rolled[:SUBLANES])
                return jnp.concatenate([top, rolled[SUBLANES:]], axis=0)

            conv = cb + cw0 * shifted(2) + cw1 * shifted(1) + cw2 * gate
            act = (conv * jax.nn.sigmoid(conv) * up).astype(BF16)
            acc = res + jnp.dot(act, wd_ref[...], preferred_element_type=F32)
            o_ref[rs, :] = _rms(acc, gl_ref[...]) if last else acc
            prev = gate[rc - SUBLANES:, :]
        carry_ref[j] = prev

    pl.when(j == 0)(functools.partial(step, True, False))
    pl.when((j > 0) & (j < nj - 1))(functools.partial(step, False, False))
    pl.when(j == nj - 1)(functools.partial(step, False, True))


def _ffn(h, gf, wg, wu, cw, cb, wd, gl, seq, tm, tf, rc):
    t = h.shape[0]
    nj = D_FF // tf
    kern = functools.partial(_ffn_kernel, tm=tm, rc=rc, seq=seq)
    return pl.pallas_call(
        kern,
        out_shape=jax.ShapeDtypeStruct((t, D_MODEL), F32),
        grid=(t // tm, nj),
        in_specs=[
            pl.BlockSpec((tm, D_MODEL), lambda i, j: (i, 0)),
            pl.BlockSpec((1, D_MODEL), lambda i, j: (0, 0)),
            pl.BlockSpec((D_MODEL, tf), lambda i, j: (0, j)),
            pl.BlockSpec((D_MODEL, tf), lambda i, j: (0, j)),
            pl.BlockSpec((CONV_WIDTH, tf), lambda i, j: (0, j)),
            pl.BlockSpec((1, tf), lambda i, j: (0, j)),
            pl.BlockSpec((tf, D_MODEL), lambda i, j: (j, 0)),
            pl.BlockSpec((1, D_MODEL), lambda i, j: (0, 0)),
        ],
        out_specs=pl.BlockSpec((tm, D_MODEL), lambda i, j: (i, 0)),
        scratch_shapes=[pltpu.VMEM((tm, D_MODEL), BF16), pltpu.VMEM((nj, SUBLANES, tf), F32)],
        compiler_params=pltpu.CompilerParams(
            dimension_semantics=("arbitrary", "arbitrary"),
            vmem_limit_bytes=VMEM_LIMIT),
        name="convffn",
    )(h, gf, wg, wu, cw, cb, wd, gl)


def kernel(x, positions, attn_norm, w_in, w_a_up, b_a_up, gla_norm, sinks, w_out, ffn_norm,
           w_gate, w_up, conv_w, conv_b, w_down, final_norm):
    batch, seq, _ = x.shape
    t = batch * seq
    assert w_in.shape[0] == 1, "the final norm is fused into the single layer's FFN kernel"
    x2 = x.reshape(t, D_MODEL)
    pos3 = positions.reshape(t // WINDOW, 1, WINDOW)
    half = SWA_HEAD_DIM // 2
    freq_col = (ROPE_THETA ** (-jnp.arange(half, dtype=F32) / half)).reshape(half, 1)

    w_proj = _wprep(jnp.swapaxes(w_in[0], 0, 1), tk=256)
    wa = jnp.concatenate(
        [w_a_up[0], jnp.zeros((LANES - GLA_RANK, GLA_QK_W), F32)], axis=0).astype(BF16)

    proj, wg_bf, wu_bf = _inproj(x2, attn_norm[0].reshape(1, D_MODEL), w_proj, tm=1024, rc=256,
                                 to_cast=[w_gate[0], w_up[0]])
    o_gla, wd_bf, wo_bf = _gla(proj, wa, b_a_up[0].reshape(1, GLA_QK_W), gla_norm[0].reshape(1, GLA_DV),
                               batch, seq, ts=1024, to_cast=[w_down[0], w_out[0]])
    o_swa, = _swa(proj, sinks[0], pos3, freq_col, batch, seq, to_cast=[])
    h = _outproj(x2, o_gla, o_swa, wo_bf, tm=512)
    y = _ffn(h, ffn_norm[0].reshape(1, D_MODEL), wg_bf, wu_bf, conv_w[0], conv_b[0].reshape(1, D_FF), wd_bf,
             final_norm.reshape(1, D_MODEL), seq, tm=1024, tf=512, rc=512)
    return y.reshape(batch, seq, D_MODEL)
```

```python
import functools

import jax
import jax.numpy as jnp
from jax import lax
from jax.experimental import pallas as pl
from jax.experimental.pallas import tpu as pltpu

D_MODEL = 2048
GLA_HEADS = 4
GLA_DK = 128
GLA_DV = 256
GLA_RANK = 16
GLA_GATE_NORM = 16.0
GLA_CHUNK = 64
GLA_SUB = 256
SWA_HEADS = 16
SWA_KV_HEADS = 2
SWA_HEAD_DIM = 64
WINDOW = 128
ROPE_THETA = 10000.0
D_FF = 5632
CONV_WIDTH = 3
EPS = 1e-6
NEG_INF = -1e30
LOG2E = 1.4426950408889634
SWA_VT_ROWS = SWA_HEAD_DIM + 16

GLA_QK_W = GLA_HEADS * GLA_DK
GLA_V_W = GLA_HEADS * GLA_DV
SWA_Q_W = SWA_HEADS * SWA_HEAD_DIM
SWA_KV_W = SWA_KV_HEADS * SWA_HEAD_DIM

LANES = 128
SUBLANES = 8

COL_GQ = 0
COL_GK = COL_GQ + GLA_QK_W
COL_GV = COL_GK + GLA_QK_W
COL_GG = COL_GV + GLA_V_W
COL_SQ = COL_GG + GLA_V_W
COL_SK = COL_SQ + SWA_Q_W
COL_SV = COL_SK + SWA_KV_W
COL_GA = COL_SV + SWA_KV_W
PROJ_TN = 1536
PROJ_W = 3 * PROJ_TN

VMEM_LIMIT = 58 * 1024 * 1024

F32 = jnp.float32
BF16 = jnp.bfloat16

_NT = (((1,), (1,)), ((), ()))
_TN = (((0,), (0,)), ((), ()))


def _rms(x, gain):
    return x * lax.rsqrt(jnp.mean(x * x, axis=-1, keepdims=True) + EPS) * gain


def _cast_specs(weights, steps, step_index):
    in_specs, out_specs, out_shapes = [], [], []
    for w in weights:
        rows, cols = w.shape
        slab = rows // steps
        assert slab * steps == rows and slab % (2 * SUBLANES) == 0, (w.shape, steps)
        spec = pl.BlockSpec((slab, cols), lambda *g: (step_index(*g), 0))
        in_specs.append(spec)
        out_specs.append(spec)
        out_shapes.append(jax.ShapeDtypeStruct(w.shape, BF16))
    return in_specs, out_specs, out_shapes


def _cast_slabs(src_refs, dst_refs):
    for src, dst in zip(src_refs, dst_refs):
        dst[...] = src[...].astype(BF16)


def _wprep_kernel(wt_ref, o_ref):
    tk = wt_ref.shape[1]
    half = SWA_HEAD_DIM // 2
    o_ga = 2 * GLA_QK_W + 2 * GLA_V_W
    o_sq = o_ga + GLA_RANK

    def put(col, rows):
        o_ref[:, col:col + rows.shape[0]] = rows.T.astype(BF16)

    put(0, wt_ref[:o_ga, :])
    for i in range((SWA_Q_W + SWA_KV_W) // LANES):
        r0 = o_sq + i * LANES
        blk = [wt_ref[r0 + j * half:r0 + (j + 1) * half, :] for j in range(4)]
        put(COL_SQ + i * LANES, jnp.concatenate([blk[0], blk[2], blk[1], blk[3]], axis=0))
    put(COL_SV, wt_ref[o_sq + SWA_Q_W + SWA_KV_W:, :])
    keep = lax.broadcasted_iota(jnp.int32, (LANES, tk), 0) < GLA_RANK
    put(COL_GA, jnp.where(keep, wt_ref[o_ga:o_ga + LANES, :], 0.0))
    o_ref[:, COL_GA + LANES:] = jnp.zeros((tk, PROJ_W - COL_GA - LANES), BF16)


def _wprep(wt, tk):
    n, k = wt.shape
    return pl.pallas_call(
        _wprep_kernel,
        out_shape=jax.ShapeDtypeStruct((k, PROJ_W), BF16),
        grid=(k // tk,),
        in_specs=[pl.BlockSpec((n, tk), lambda i: (0, i))],
        out_specs=pl.BlockSpec((tk, PROJ_W), lambda i: (i, 0)),
        compiler_params=pltpu.CompilerParams(
            dimension_semantics=("arbitrary",),
            vmem_limit_bytes=VMEM_LIMIT),
        name="wprep",
    )(wt)


def _inproj_kernel(x_ref, g_ref, w_ref, *rest, tm, rc):
    n_cast = (len(rest) - 1) // 2
    o_ref = rest[n_cast]
    _cast_slabs(rest[:n_cast], rest[n_cast + 1:])
    for r in range(tm // rc):
        rs = pl.ds(r * rc, rc)
        u = _rms(x_ref[rs, :], g_ref[...]).astype(BF16)
        for c in range(PROJ_W // PROJ_TN):
            cs = pl.ds(c * PROJ_TN, PROJ_TN)
            o_ref[rs, cs] = jnp.dot(u, w_ref[:, cs], preferred_element_type=F32).astype(BF16)


def _inproj(x2, gain, w, tm, rc, to_cast):
    t = x2.shape[0]
    cast_in, cast_out, cast_shapes = _cast_specs(to_cast, t // tm, lambda i: i)
    return pl.pallas_call(
        functools.partial(_inproj_kernel, tm=tm, rc=rc),
        out_shape=[jax.ShapeDtypeStruct((t, PROJ_W), BF16)] + cast_shapes,
        grid=(t // tm,),
        in_specs=[
            pl.BlockSpec((tm, D_MODEL), lambda i: (i, 0)),
            pl.BlockSpec((1, D_MODEL), lambda i: (0, 0)),
            pl.BlockSpec((D_MODEL, PROJ_W), lambda i: (0, 0), pipeline_mode=pl.Buffered(1)),
        ] + cast_in,
        out_specs=[pl.BlockSpec((tm, PROJ_W), lambda i: (i, 0))] + cast_out,
        compiler_params=pltpu.CompilerParams(
            dimension_semantics=("arbitrary",),
            vmem_limit_bytes=VMEM_LIMIT),
        name="inproj",
    )(x2, gain, w, *to_cast)


def _gla_kernel(q_ref, k_ref, v_ref, g_ref, a_ref, wa_ref, ba_ref, gn_ref, tril_ref, mask_ref, *rest, ts):
    c = GLA_CHUNK
    nc = ts // c
    n_cast = (len(rest) - 2) // 2
    o_ref, s_ref = rest[n_cast], rest[-1]
    _cast_slabs(rest[:n_cast], rest[n_cast + 1:-1])
    heads = range(GLA_HEADS)
    ks = [slice(h * GLA_DK, (h + 1) * GLA_DK) for h in heads]
    vs = [slice(h * GLA_DV, (h + 1) * GLA_DV) for h in heads]

    @pl.when(pl.program_id(1) == 0)
    def _():
        s_ref[...] = jnp.zeros_like(s_ref)

    z = jnp.dot(a_ref[...], wa_ref[...], preferred_element_type=F32) + ba_ref[...]
    log_a = (jnp.minimum(z, 0.0) - jnp.log(1.0 + jnp.exp(-jnp.abs(z)))) * (1.0 / GLA_GATE_NORM)

    la_hi = log_a.astype(BF16)
    la_lo = (log_a - la_hi.astype(F32)).astype(BF16)
    la_hl = jnp.concatenate([la_hi, la_lo], axis=1)
    subs = [slice(s * GLA_SUB, (s + 1) * GLA_SUB) for s in range(ts // GLA_SUB)]
    cum2 = jnp.concatenate([jnp.dot(tril_ref[...], la_hl[rs], preferred_element_type=F32) for rs in subs], axis=0)
    bcum = cum2[:, :GLA_QK_W] + cum2[:, GLA_QK_W:]
    bcum3 = bcum.reshape(nc, c, GLA_QK_W)
    b_last = bcum3[:, c - 1:c, :]

    q = q_ref[...].astype(F32) * (GLA_DK ** -0.5)
    k = k_ref[...].astype(F32)
    q_e = (q * jnp.exp(bcum)).astype(BF16)
    k_e = (k * jnp.exp(-bcum)).astype(BF16)
    k_d = (k * jnp.exp(b_last - bcum3).reshape(ts, GLA_QK_W)).astype(BF16)
    decay = jnp.exp(b_last)
    v = [v_ref[:, vs[h]] for h in heads]

    keep = mask_ref[...] != 0.0

    for h in heads:
        o_intra = []
        for rs in subs:
            scores = lax.dot_general(q_e[rs, ks[h]], k_e[rs, ks[h]], _NT, preferred_element_type=F32)
            scores = jnp.where(keep, scores, 0.0).astype(BF16)
            o_intra.append(jnp.dot(scores, v[h][rs], preferred_element_type=F32))
        o_intra = jnp.concatenate(o_intra, axis=0)

        state_t = s_ref[h]
        o_inter = []
        for n in range(nc):
            rows = slice(n * c, (n + 1) * c)
            o_inter.append(lax.dot_general(q_e[rows, ks[h]], state_t.astype(BF16), _NT,
                                           preferred_element_type=F32))
            kv_t = lax.dot_general(v[h][rows], k_d[rows, ks[h]], _TN, preferred_element_type=F32)
            state_t = state_t * decay[n][:, ks[h]] + kv_t
        s_ref[h] = state_t

        o = _rms(o_intra + jnp.concatenate(o_inter, axis=0), gn_ref[...])
        gate = g_ref[:, vs[h]].astype(F32)
        o_ref[:, vs[h]] = (o * (gate * jax.nn.sigmoid(gate))).astype(BF16)


def _gla(proj, wa, ba, gn, batch, seq, ts, to_cast):
    nt = seq // ts
    cast_in, cast_out, cast_shapes = _cast_specs(to_cast, batch * nt, lambda b, i: b * nt + i)
    kern = functools.partial(_gla_kernel, ts=ts)
    rows = lambda b, i: b * nt + i
    idx = jnp.arange(GLA_SUB)
    causal = (idx[:, None] >= idx[None, :]) & (idx[:, None] // GLA_CHUNK == idx[None, :] // GLA_CHUNK)
    return pl.pallas_call(
        kern,
        out_shape=[jax.ShapeDtypeStruct((batch * seq, GLA_V_W), BF16)] + cast_shapes,
        grid=(batch, nt),
        in_specs=[
            pl.BlockSpec((ts, GLA_QK_W), lambda b, i: (rows(b, i), COL_GQ // GLA_QK_W)),
            pl.BlockSpec((ts, GLA_QK_W), lambda b, i: (rows(b, i), COL_GK // GLA_QK_W)),
            pl.BlockSpec((ts, GLA_V_W), lambda b, i: (rows(b, i), COL_GV // GLA_V_W)),
            pl.BlockSpec((ts, GLA_V_W), lambda b, i: (rows(b, i), COL_GG // GLA_V_W)),
            pl.BlockSpec((ts, LANES), lambda b, i: (rows(b, i), COL_GA // LANES)),
            pl.BlockSpec((LANES, GLA_QK_W), lambda b, i: (0, 0)),
            pl.BlockSpec((1, GLA_QK_W), lambda b, i: (0, 0)),
            pl.BlockSpec((1, GLA_DV), lambda b, i: (0, 0)),
            pl.BlockSpec((GLA_SUB, GLA_SUB), lambda b, i: (0, 0)),
            pl.BlockSpec((GLA_SUB, GLA_SUB), lambda b, i: (0, 0)),
        ] + cast_in,
        out_specs=[pl.BlockSpec((ts, GLA_V_W), lambda b, i: (rows(b, i), 0))] + cast_out,
        scratch_shapes=[pltpu.VMEM((GLA_HEADS, GLA_DV, GLA_DK), F32)],
        compiler_params=pltpu.CompilerParams(
            dimension_semantics=("arbitrary", "arbitrary"),
            vmem_limit_bytes=VMEM_LIMIT),
        name="gla",
    )(proj, proj, proj, proj, proj, wa, ba, gn, causal.astype(BF16), causal.astype(F32), *to_cast)


def _swa_kernel(sinks_ref, q_ref, k_ref, v_ref, pos_ref, posn_ref, freq_ref, eye_ref, *rest):
    n_cast = (len(rest) - 5) // 2
    o_ref = rest[n_cast]
    k2p_ref, vtp_ref, cos_ref, sin_ref = rest[-4:]
    _cast_slabs(rest[:n_cast], rest[n_cast + 1:-4])
    nb = pl.program_id(1)
    wb = WINDOW
    hd = SWA_HEAD_DIM
    half = hd // 2
    pairs_per_kv = SWA_Q_W // LANES // SWA_KV_HEADS
    lane = lax.broadcasted_iota(jnp.int32, (wb, LANES), 1)
    head_a = (lane % hd) < half
    low = lane < hd

    def rotary_tables(pos_row):
        ang_t = freq_ref[...] * pos_row.astype(F32)
        reps = LANES // half
        cs = jnp.concatenate([jnp.cos(ang_t)] * reps, axis=0).T
        sn = jnp.concatenate([jnp.sin(ang_t)] * reps, axis=0).T
        return cs, jnp.where(low, -sn, sn)

    @pl.when(nb == 0)
    def _():
        k2p_ref[...] = jnp.zeros_like(k2p_ref)
        vtp_ref[...] = jnp.zeros_like(vtp_ref)
        cos_ref[...], sin_ref[...] = rotary_tables(pos_ref[0])

    cos = cos_ref[...]
    sin = sin_ref[...]

    def rope(t, cs, sn):
        return t * cs + pltpu.roll(t, hd, 1) * sn

    k_r = rope(k_ref[...].astype(F32), cos, sin)
    k2_cur = [jnp.where(head_a, k_r, pltpu.roll(k_r, half, 1)).astype(BF16),
              jnp.where(head_a, pltpu.roll(k_r, LANES - half, 1), k_r).astype(BF16)]
    v_t = v_ref[...].astype(F32).T
    ones_rows = (lax.broadcasted_iota(jnp.int32, (SWA_VT_ROWS - hd, wb), 0) == 0).astype(F32)
    vt_cur = [jnp.concatenate([v_t[c * hd:(c + 1) * hd], ones_rows], axis=0).astype(BF16)
              for c in range(SWA_KV_HEADS)]
    vt = [jnp.concatenate([vtp_ref[c], vt_cur[c]], axis=1) for c in range(SWA_KV_HEADS)]

    ki = lax.broadcasted_iota(jnp.int32, (2 * wb, wb), 0)
    qi = lax.broadcasted_iota(jnp.int32, (2 * wb, wb), 1) + wb
    mask = (ki <= qi) & (qi - ki < WINDOW) & ((nb > 0) | (ki >= wb))
    bias = jnp.where(mask, 0.0, NEG_INF).astype(BF16)
    k2 = [jnp.concatenate([k2p_ref[c], k2_cur[c]], axis=0) for c in range(SWA_KV_HEADS)]
    k2 = [jnp.concatenate([k2[c], bias], axis=1) for c in range(SWA_KV_HEADS)]

    qscale = (hd ** -0.5) * LOG2E
    cos_q = cos * qscale
    sin_q = sin * qscale

    n_tiles = SWA_Q_W // LANES
    lhs = []
    for t in range(n_tiles):
        q_r = rope(q_ref[:, t * LANES:(t + 1) * LANES].astype(F32), cos_q, sin_q)
        lhs += [jnp.where(head_a, q_r, 0.0), jnp.where(head_a, 0.0, q_r)]
    cols_per_kv = 2 * pairs_per_kv * wb
    st = []
    for c in range(SWA_KV_HEADS):
        lhs_c = jnp.concatenate(lhs[2 * c * pairs_per_kv:2 * (c + 1) * pairs_per_kv], axis=0).astype(BF16)
        lhs_c = jnp.concatenate([lhs_c, eye_ref[...]], axis=1)
        st.append(lax.dot_general(k2[c], lhs_c, _NT, preferred_element_type=F32))
    st = jnp.concatenate(st, axis=1)
    m = jnp.max(st, axis=0, keepdims=True)
    e = jnp.exp2(st - m).astype(BF16)
    sink = jnp.concatenate([jnp.full((1, wb), sinks_ref[i], F32) for i in range(SWA_HEADS)], axis=1) * LOG2E
    sink_term = jnp.exp2(sink - m)
    for c in range(SWA_KV_HEADS):
        cols = slice(c * cols_per_kv, (c + 1) * cols_per_kv)
        ot = jnp.dot(vt[c], e[:, cols], preferred_element_type=F32)
        o_n = ot[:hd] * (1.0 / (ot[hd:hd + 1] + sink_term[:, cols]))
        for n in range(pairs_per_kv):
            t = c * pairs_per_kv + n
            pair = jnp.concatenate([o_n[:, 2 * n * wb:(2 * n + 1) * wb], o_n[:, (2 * n + 1) * wb:(2 * n + 2) * wb]],
                                   axis=0)
            o_ref[:, t * LANES:(t + 1) * LANES] = pair.T.astype(BF16)

    for c in range(SWA_KV_HEADS):
        k2p_ref[c] = k2_cur[c]
        vtp_ref[c] = vt_cur[c]
    cos_ref[...], sin_ref[...] = rotary_tables(posn_ref[0])


def _swa(proj, sinks, pos3, freq_col, batch, seq, to_cast):
    nb = seq // WINDOW
    cast_in, cast_out, cast_shapes = _cast_specs(to_cast, batch * nb, lambda b, n: b * nb + n)
    eye = jnp.tile(jnp.eye(WINDOW, dtype=BF16), (SWA_HEADS // SWA_KV_HEADS, 1))
    rows = lambda b, n: b * nb + n
    return pl.pallas_call(
        _swa_kernel,
        out_shape=[jax.ShapeDtypeStruct((batch * seq, SWA_Q_W), BF16)] + cast_shapes,
        grid=(batch, nb),
        in_specs=[
            pl.BlockSpec(memory_space=pltpu.SMEM),
            pl.BlockSpec((WINDOW, SWA_Q_W), lambda b, n: (rows(b, n), COL_SQ // SWA_Q_W)),
            pl.BlockSpec((WINDOW, SWA_KV_W), lambda b, n: (rows(b, n), COL_SK // SWA_KV_W)),
            pl.BlockSpec((WINDOW, SWA_KV_W), lambda b, n: (rows(b, n), COL_SV // SWA_KV_W)),
            pl.BlockSpec((1, 1, WINDOW), lambda b, n: (rows(b, n), 0, 0)),
            pl.BlockSpec((1, 1, WINDOW), lambda b, n: (rows(b, jnp.minimum(n + 1, nb - 1)), 0, 0)),
            pl.BlockSpec((SWA_HEAD_DIM // 2, 1), lambda b, n: (0, 0)),
            pl.BlockSpec((SWA_HEADS // SWA_KV_HEADS * WINDOW, WINDOW), lambda b, n: (0, 0)),
        ] + cast_in,
        out_specs=[pl.BlockSpec((WINDOW, SWA_Q_W), lambda b, n: (rows(b, n), 0))] + cast_out,
        scratch_shapes=[pltpu.VMEM((SWA_KV_HEADS, WINDOW, LANES), BF16),
                        pltpu.VMEM((SWA_KV_HEADS, SWA_VT_ROWS, WINDOW), BF16),
                        pltpu.VMEM((WINDOW, LANES), F32), pltpu.VMEM((WINDOW, LANES), F32)],
        compiler_params=pltpu.CompilerParams(
            dimension_semantics=("arbitrary", "arbitrary"),
            vmem_limit_bytes=VMEM_LIMIT),
        name="swa",
    )(sinks, proj, proj, proj, pos3, pos3, freq_col, eye, *to_cast)


def _outproj_kernel(x_ref, a_ref, b_ref, wa_ref, wb_ref, h_ref):
    acc = jnp.dot(a_ref[...], wa_ref[...], preferred_element_type=F32)
    acc = acc + jnp.dot(b_ref[...], wb_ref[...], preferred_element_type=F32)
    h_ref[...] = x_ref[...] + acc


def _outproj(x2, o_gla, o_swa, w, tm):
    t = x2.shape[0]
    return pl.pallas_call(
        _outproj_kernel,
        out_shape=jax.ShapeDtypeStruct((t, D_MODEL), F32),
        grid=(t // tm,),
        in_specs=[
            pl.BlockSpec((tm, D_MODEL), lambda i: (i, 0)),
            pl.BlockSpec((tm, GLA_V_W), lambda i: (i, 0)),
            pl.BlockSpec((tm, SWA_Q_W), lambda i: (i, 0)),
            pl.BlockSpec((GLA_V_W, D_MODEL), lambda i: (0, 0)),
            pl.BlockSpec((SWA_Q_W, D_MODEL), lambda i: (GLA_V_W // SWA_Q_W, 0)),
        ],
        out_specs=pl.BlockSpec((tm, D_MODEL), lambda i: (i, 0)),
        compiler_params=pltpu.CompilerParams(
            dimension_semantics=("arbitrary",),
            vmem_limit_bytes=VMEM_LIMIT),
        name="outproj",
    )(x2, o_gla, o_swa, w, w)


def _ffn_kernel(h_ref, gf_ref, wg_ref, wu_ref, cw_ref, cb_ref, wd_ref, gl_ref, o_ref,
                hn_ref, carry_ref, *, tm, rc, seq):
    i = pl.program_id(0)
    j = pl.program_id(1)
    nj = pl.num_programs(1)

    def step(first, last):
        seq_start = (i * tm) % seq == 0
        prev = jnp.where(seq_start, 0.0, carry_ref[j])
        rows = lax.broadcasted_iota(jnp.int32, prev.shape, 0)
        cb = cb_ref[...]
        cw0, cw1, cw2 = cw_ref[0:1, :], cw_ref[1:2, :], cw_ref[2:3, :]

        for r in range(tm // rc):
            rs = pl.ds(r * rc, rc)
            if first:
                res = h_ref[rs, :]
                hn = _rms(res, gf_ref[...]).astype(BF16)
                hn_ref[rs, :] = hn
            else:
                res = o_ref[rs, :]
                hn = hn_ref[rs, :]
            gate = jnp.dot(hn, wg_ref[...], preferred_element_type=F32)
            up = jnp.dot(hn, wu_ref[...], preferred_element_type=F32)

            def shifted(d):
                rolled = pltpu.roll(gate, d, 0)
                top = jnp.where(rows < d, pltpu.roll(prev, d, 0), rolled[:SUBLANES])
                return jnp.concatenate([top, rolled[SUBLANES:]], axis=0)

            conv = cb + cw0 * shifted(2) + cw1 * shifted(1) + cw2 * gate
            act = (conv * jax.nn.sigmoid(conv) * up).astype(BF16)
            acc = res + jnp.dot(act, wd_ref[...], preferred_element_type=F32)
            o_ref[rs, :] = _rms(acc, gl_ref[...]) if last else acc
            prev = gate[rc - SUBLANES:, :]
        carry_ref[j] = prev

    pl.when(j == 0)(functools.partial(step, True, False))
    pl.when((j > 0) & (j < nj - 1))(functools.partial(step, False, False))
    pl.when(j == nj - 1)(functools.partial(step, False, True))


def _ffn(h, gf, wg, wu, cw, cb, wd, gl, seq, tm, tf, rc):
    t = h.shape[0]
    nj = D_FF // tf
    kern = functools.partial(_ffn_kernel, tm=tm, rc=rc, seq=seq)
    return pl.pallas_call(
        kern,
        out_shape=jax.ShapeDtypeStruct((t, D_MODEL), F32),
        grid=(t // tm, nj),
        in_specs=[
            pl.BlockSpec((tm, D_MODEL), lambda i, j: (i, 0)),
            pl.BlockSpec((1, D_MODEL), lambda i, j: (0, 0)),
            pl.BlockSpec((D_MODEL, tf), lambda i, j: (0, j)),
            pl.BlockSpec((D_MODEL, tf), lambda i, j: (0, j)),
            pl.BlockSpec((CONV_WIDTH, tf), lambda i, j: (0, j)),
            pl.BlockSpec((1, tf), lambda i, j: (0, j)),
            pl.BlockSpec((tf, D_MODEL), lambda i, j: (j, 0)),
            pl.BlockSpec((1, D_MODEL), lambda i, j: (0, 0)),
        ],
        out_specs=pl.BlockSpec((tm, D_MODEL), lambda i, j: (i, 0)),
        scratch_shapes=[pltpu.VMEM((tm, D_MODEL), BF16), pltpu.VMEM((nj, SUBLANES, tf), F32)],
        compiler_params=pltpu.CompilerParams(
            dimension_semantics=("arbitrary", "arbitrary"),
            vmem_limit_bytes=VMEM_LIMIT),
        name="convffn",
    )(h, gf, wg, wu, cw, cb, wd, gl)


def kernel(x, positions, attn_norm, w_in, w_a_up, b_a_up, gla_norm, sinks, w_out, ffn_norm,
           w_gate, w_up, conv_w, conv_b, w_down, final_norm):
    batch, seq, _ = x.shape
    t = batch * seq
    assert w_in.shape[0] == 1, "the final norm is fused into the single layer's FFN kernel"
    x2 = x.reshape(t, D_MODEL)
    pos3 = positions.reshape(t // WINDOW, 1, WINDOW)
    half = SWA_HEAD_DIM // 2
    freq_col = (ROPE_THETA ** (-jnp.arange(half, dtype=F32) / half)).reshape(half, 1)

    w_proj = _wprep(jnp.swapaxes(w_in[0], 0, 1), tk=256)
    wa = jnp.concatenate(
        [w_a_up[0], jnp.zeros((LANES - GLA_RANK, GLA_QK_W), F32)], axis=0).astype(BF16)

    proj, wg_bf, wu_bf = _inproj(x2, attn_norm[0].reshape(1, D_MODEL), w_proj, tm=512, rc=256,
                                 to_cast=[w_gate[0], w_up[0]])
    o_gla, wd_bf, wo_bf = _gla(proj, wa, b_a_up[0].reshape(1, GLA_QK_W), gla_norm[0].reshape(1, GLA_DV),
                               batch, seq, ts=1024, to_cast=[w_down[0], w_out[0]])
    o_swa, = _swa(proj, sinks[0], pos3, freq_col, batch, seq, to_cast=[])
    h = _outproj(x2, o_gla, o_swa, wo_bf, tm=512)
    y = _ffn(h, ffn_norm[0].reshape(1, D_MODEL), wg_bf, wu_bf, conv_w[0], conv_b[0].reshape(1, D_FF), wd_bf,
             final_norm.reshape(1, D_MODEL), seq, tm=1024, tf=512, rc=512)
    return y.reshape(batch, seq, D_MODEL)
```

```python
import functools

import jax
import jax.numpy as jnp
from jax import lax
from jax.experimental import pallas as pl
from jax.experimental.pallas import tpu as pltpu

D_MODEL = 2048
GLA_HEADS = 4
GLA_DK = 128
GLA_DV = 256
GLA_RANK = 16
GLA_GATE_NORM = 16.0
GLA_CHUNK = 64
GLA_SUB = 256
SWA_HEADS = 16
SWA_KV_HEADS = 2
SWA_HEAD_DIM = 64
WINDOW = 128
ROPE_THETA = 10000.0
D_FF = 5632
CONV_WIDTH = 3
EPS = 1e-6
NEG_INF = -1e30
LOG2E = 1.4426950408889634
SWA_VT_ROWS = SWA_HEAD_DIM + 16

GLA_QK_W = GLA_HEADS * GLA_DK
GLA_V_W = GLA_HEADS * GLA_DV
SWA_Q_W = SWA_HEADS * SWA_HEAD_DIM
SWA_KV_W = SWA_KV_HEADS * SWA_HEAD_DIM

LANES = 128
SUBLANES = 8

COL_GQ = 0
COL_GK = COL_GQ + GLA_QK_W
COL_GV = COL_GK + GLA_QK_W
COL_GG = COL_GV + GLA_V_W
COL_SQ = COL_GG + GLA_V_W
COL_SK = COL_SQ + SWA_Q_W
COL_SV = COL_SK + SWA_KV_W
COL_GA = COL_SV + SWA_KV_W
PROJ_TN = 1536
PROJ_W = 3 * PROJ_TN

VMEM_LIMIT = 58 * 1024 * 1024

F32 = jnp.float32
BF16 = jnp.bfloat16

_NT = (((1,), (1,)), ((), ()))
_TN = (((0,), (0,)), ((), ()))


def _rms(x, gain):
    return x * lax.rsqrt(jnp.mean(x * x, axis=-1, keepdims=True) + EPS) * gain


def _cast_specs(weights, steps, step_index):
    in_specs, out_specs, out_shapes = [], [], []
    for w in weights:
        rows, cols = w.shape
        slab = rows // steps
        assert slab * steps == rows and slab % (2 * SUBLANES) == 0, (w.shape, steps)
        spec = pl.BlockSpec((slab, cols), lambda *g: (step_index(*g), 0))
        in_specs.append(spec)
        out_specs.append(spec)
        out_shapes.append(jax.ShapeDtypeStruct(w.shape, BF16))
    return in_specs, out_specs, out_shapes


def _cast_slabs(src_refs, dst_refs):
    for src, dst in zip(src_refs, dst_refs):
        dst[...] = src[...].astype(BF16)


def _wprep_kernel(wt_ref, o_ref):
    tk = wt_ref.shape[1]
    half = SWA_HEAD_DIM // 2
    o_ga = 2 * GLA_QK_W + 2 * GLA_V_W
    o_sq = o_ga + GLA_RANK

    def put(col, rows):
        o_ref[:, col:col + rows.shape[0]] = rows.T.astype(BF16)

    put(0, wt_ref[:o_ga, :])
    for i in range((SWA_Q_W + SWA_KV_W) // LANES):
        r0 = o_sq + i * LANES
        blk = [wt_ref[r0 + j * half:r0 + (j + 1) * half, :] for j in range(4)]
        put(COL_SQ + i * LANES, jnp.concatenate([blk[0], blk[2], blk[1], blk[3]], axis=0))
    put(COL_SV, wt_ref[o_sq + SWA_Q_W + SWA_KV_W:, :])
    keep = lax.broadcasted_iota(jnp.int32, (LANES, tk), 0) < GLA_RANK
    put(COL_GA, jnp.where(keep, wt_ref[o_ga:o_ga + LANES, :], 0.0))
    o_ref[:, COL_GA + LANES:] = jnp.zeros((tk, PROJ_W - COL_GA - LANES), BF16)


def _wprep(wt, tk):
    n, k = wt.shape
    return pl.pallas_call(
        _wprep_kernel,
        out_shape=jax.ShapeDtypeStruct((k, PROJ_W), BF16),
        grid=(k // tk,),
        in_specs=[pl.BlockSpec((n, tk), lambda i: (0, i))],
        out_specs=pl.BlockSpec((tk, PROJ_W), lambda i: (i, 0)),
        compiler_params=pltpu.CompilerParams(
            dimension_semantics=("arbitrary",),
            vmem_limit_bytes=VMEM_LIMIT),
        name="wprep",
    )(wt)


def _inproj_kernel(x_ref, g_ref, w_ref, pos_ref, freq_ref, *rest, tm, rc):
    n_cast = (len(rest) - 3) // 2
    o_ref, cos_ref, sin_ref = rest[n_cast:n_cast + 3]
    _cast_slabs(rest[:n_cast], rest[n_cast + 3:])

    half = SWA_HEAD_DIM // 2
    low = lax.broadcasted_iota(jnp.int32, (WINDOW, LANES), 1) < SWA_HEAD_DIM
    for b in range(tm // WINDOW):
        ang_t = freq_ref[...] * pos_ref[b].astype(F32)
        cs = jnp.concatenate([jnp.cos(ang_t)] * (LANES // half), axis=0).T
        sn = jnp.concatenate([jnp.sin(ang_t)] * (LANES // half), axis=0).T
        cos_ref[b * WINDOW:(b + 1) * WINDOW, :] = cs
        sin_ref[b * WINDOW:(b + 1) * WINDOW, :] = jnp.where(low, -sn, sn)

    for r in range(tm // rc):
        rs = pl.ds(r * rc, rc)
        u = _rms(x_ref[rs, :], g_ref[...]).astype(BF16)
        for c in range(PROJ_W // PROJ_TN):
            cols = pl.ds(c * PROJ_TN, PROJ_TN)
            o_ref[rs, cols] = jnp.dot(u, w_ref[:, cols], preferred_element_type=F32).astype(BF16)


def _inproj(x2, gain, w, pos3, freq_col, tm, rc, to_cast):
    t = x2.shape[0]
    cast_in, cast_out, cast_shapes = _cast_specs(to_cast, t // tm, lambda i: i)
    return pl.pallas_call(
        functools.partial(_inproj_kernel, tm=tm, rc=rc),
        out_shape=[jax.ShapeDtypeStruct((t, PROJ_W), BF16), jax.ShapeDtypeStruct((t, LANES), F32),
                   jax.ShapeDtypeStruct((t, LANES), F32)] + cast_shapes,
        grid=(t // tm,),
        in_specs=[
            pl.BlockSpec((tm, D_MODEL), lambda i: (i, 0)),
            pl.BlockSpec((1, D_MODEL), lambda i: (0, 0)),
            pl.BlockSpec((D_MODEL, PROJ_W), lambda i: (0, 0), pipeline_mode=pl.Buffered(1)),
            pl.BlockSpec((tm // WINDOW, 1, WINDOW), lambda i: (i, 0, 0)),
            pl.BlockSpec((SWA_HEAD_DIM // 2, 1), lambda i: (0, 0)),
        ] + cast_in,
        out_specs=[pl.BlockSpec((tm, PROJ_W), lambda i: (i, 0)), pl.BlockSpec((tm, LANES), lambda i: (i, 0)),
                   pl.BlockSpec((tm, LANES), lambda i: (i, 0))] + cast_out,
        compiler_params=pltpu.CompilerParams(
            dimension_semantics=("arbitrary",),
            vmem_limit_bytes=VMEM_LIMIT),
        name="inproj",
    )(x2, gain, w, pos3, freq_col, *to_cast)


def _gla_kernel(q_ref, k_ref, v_ref, g_ref, a_ref, wa_ref, ba_ref, gn_ref, tril_ref, mask_ref, *rest, ts):
    c = GLA_CHUNK
    nc = ts // c
    n_cast = (len(rest) - 2) // 2
    o_ref, s_ref = rest[n_cast], rest[-1]
    _cast_slabs(rest[:n_cast], rest[n_cast + 1:-1])
    heads = range(GLA_HEADS)
    ks = [slice(h * GLA_DK, (h + 1) * GLA_DK) for h in heads]
    vs = [slice(h * GLA_DV, (h + 1) * GLA_DV) for h in heads]

    @pl.when(pl.program_id(1) == 0)
    def _():
        s_ref[...] = jnp.zeros_like(s_ref)

    z = jnp.dot(a_ref[...], wa_ref[...], preferred_element_type=F32) + ba_ref[...]
    log_a = (jnp.minimum(z, 0.0) - jnp.log(1.0 + jnp.exp(-jnp.abs(z)))) * (1.0 / GLA_GATE_NORM)

    la_hi = log_a.astype(BF16)
    la_lo = (log_a - la_hi.astype(F32)).astype(BF16)
    la_hl = jnp.concatenate([la_hi, la_lo], axis=1)
    subs = [slice(s * GLA_SUB, (s + 1) * GLA_SUB) for s in range(ts // GLA_SUB)]
    cum2 = jnp.concatenate([jnp.dot(tril_ref[...], la_hl[rs], preferred_element_type=F32) for rs in subs], axis=0)
    bcum = cum2[:, :GLA_QK_W] + cum2[:, GLA_QK_W:]
    bcum3 = bcum.reshape(nc, c, GLA_QK_W)
    b_last = bcum3[:, c - 1:c, :]

    q = q_ref[...].astype(F32) * (GLA_DK ** -0.5)
    k = k_ref[...].astype(F32)
    q_e = (q * jnp.exp(bcum)).astype(BF16)
    k_e = (k * jnp.exp(-bcum)).astype(BF16)
    k_d = (k * jnp.exp(b_last - bcum3).reshape(ts, GLA_QK_W)).astype(BF16)
    decay = jnp.exp(b_last)
    v = [v_ref[:, vs[h]] for h in heads]

    keep = mask_ref[...] != 0.0

    for h in heads:
        o_intra = []
        for rs in subs:
            scores = lax.dot_general(q_e[rs, ks[h]], k_e[rs, ks[h]], _NT, preferred_element_type=F32)
            scores = jnp.where(keep, scores, 0.0).astype(BF16)
            o_intra.append(jnp.dot(scores, v[h][rs], preferred_element_type=F32))
        o_intra = jnp.concatenate(o_intra, axis=0)

        state_t = s_ref[h]
        o_inter = []
        for n in range(nc):
            rows = slice(n * c, (n + 1) * c)
            o_inter.append(lax.dot_general(q_e[rows, ks[h]], state_t.astype(BF16), _NT,
                                           preferred_element_type=F32))
            kv_t = lax.dot_general(v[h][rows], k_d[rows, ks[h]], _TN, preferred_element_type=F32)
            state_t = state_t * decay[n][:, ks[h]] + kv_t
        s_ref[h] = state_t

        o = _rms(o_intra + jnp.concatenate(o_inter, axis=0), gn_ref[...])
        gate = g_ref[:, vs[h]].astype(F32)
        o_ref[:, vs[h]] = (o * (gate * jax.nn.sigmoid(gate))).astype(BF16)


def _gla(proj, wa, ba, gn, batch, seq, ts, to_cast):
    nt = seq // ts
    cast_in, cast_out, cast_shapes = _cast_specs(to_cast, batch * nt, lambda b, i: b * nt + i)
    kern = functools.partial(_gla_kernel, ts=ts)
    rows = lambda b, i: b * nt + i
    idx = jnp.arange(GLA_SUB)
    causal = (idx[:, None] >= idx[None, :]) & (idx[:, None] // GLA_CHUNK == idx[None, :] // GLA_CHUNK)
    return pl.pallas_call(
        kern,
        out_shape=[jax.ShapeDtypeStruct((batch * seq, GLA_V_W), BF16)] + cast_shapes,
        grid=(batch, nt),
        in_specs=[
            pl.BlockSpec((ts, GLA_QK_W), lambda b, i: (rows(b, i), COL_GQ // GLA_QK_W)),
            pl.BlockSpec((ts, GLA_QK_W), lambda b, i: (rows(b, i), COL_GK // GLA_QK_W)),
            pl.BlockSpec((ts, GLA_V_W), lambda b, i: (rows(b, i), COL_GV // GLA_V_W)),
            pl.BlockSpec((ts, GLA_V_W), lambda b, i: (rows(b, i), COL_GG // GLA_V_W)),
            pl.BlockSpec((ts, LANES), lambda b, i: (rows(b, i), COL_GA // LANES)),
            pl.BlockSpec((LANES, GLA_QK_W), lambda b, i: (0, 0)),
            pl.BlockSpec((1, GLA_QK_W), lambda b, i: (0, 0)),
            pl.BlockSpec((1, GLA_DV), lambda b, i: (0, 0)),
            pl.BlockSpec((GLA_SUB, GLA_SUB), lambda b, i: (0, 0)),
            pl.BlockSpec((GLA_SUB, GLA_SUB), lambda b, i: (0, 0)),
        ] + cast_in,
        out_specs=[pl.BlockSpec((ts, GLA_V_W), lambda b, i: (rows(b, i), 0))] + cast_out,
        scratch_shapes=[pltpu.VMEM((GLA_HEADS, GLA_DV, GLA_DK), F32)],
        compiler_params=pltpu.CompilerParams(
            dimension_semantics=("arbitrary", "arbitrary"),
            vmem_limit_bytes=VMEM_LIMIT),
        name="gla",
    )(proj, proj, proj, proj, proj, wa, ba, gn, causal.astype(BF16), causal.astype(F32), *to_cast)


def _swa_kernel(sinks_ref, q_ref, k_ref, v_ref, cos_ref, sin_ref, eye_ref, *rest):
    n_cast = (len(rest) - 3) // 2
    o_ref = rest[n_cast]
    k2p_ref, vtp_ref = rest[-2:]
    _cast_slabs(rest[:n_cast], rest[n_cast + 1:-2])
    nb = pl.program_id(1)
    wb = WINDOW
    hd = SWA_HEAD_DIM
    half = hd // 2
    pairs_per_kv = SWA_Q_W // LANES // SWA_KV_HEADS
    lane = lax.broadcasted_iota(jnp.int32, (wb, LANES), 1)
    head_a = (lane % hd) < half

    @pl.when(nb == 0)
    def _():
        k2p_ref[...] = jnp.zeros_like(k2p_ref)
        vtp_ref[...] = jnp.zeros_like(vtp_ref)

    cos = cos_ref[...]
    sin = sin_ref[...]

    def rope(t, cs, sn):
        return t * cs + pltpu.roll(t, hd, 1) * sn

    k_r = rope(k_ref[...].astype(F32), cos, sin)
    k2_cur = [jnp.where(head_a, k_r, pltpu.roll(k_r, half, 1)).astype(BF16),
              jnp.where(head_a, pltpu.roll(k_r, LANES - half, 1), k_r).astype(BF16)]
    v_t = v_ref[...].astype(F32).T
    ones_rows = (lax.broadcasted_iota(jnp.int32, (SWA_VT_ROWS - hd, wb), 0) == 0).astype(F32)
    vt_cur = [jnp.concatenate([v_t[c * hd:(c + 1) * hd], ones_rows], axis=0).astype(BF16)
              for c in range(SWA_KV_HEADS)]
    vt = [jnp.concatenate([vtp_ref[c], vt_cur[c]], axis=1) for c in range(SWA_KV_HEADS)]

    ki = lax.broadcasted_iota(jnp.int32, (2 * wb, wb), 0)
    qi = lax.broadcasted_iota(jnp.int32, (2 * wb, wb), 1) + wb
    mask = (ki <= qi) & (qi - ki < WINDOW) & ((nb > 0) | (ki >= wb))
    bias = jnp.where(mask, 0.0, NEG_INF).astype(BF16)
    k2 = [jnp.concatenate([k2p_ref[c], k2_cur[c]], axis=0) for c in range(SWA_KV_HEADS)]
    k2 = [jnp.concatenate([k2[c], bias], axis=1) for c in range(SWA_KV_HEADS)]

    qscale = (hd ** -0.5) * LOG2E
    cos_q = cos * qscale
    sin_q = sin * qscale

    n_tiles = SWA_Q_W // LANES
    lhs = []
    for t in range(n_tiles):
        q_r = rope(q_ref[:, t * LANES:(t + 1) * LANES].astype(F32), cos_q, sin_q)
        lhs += [jnp.where(head_a, q_r, 0.0), jnp.where(head_a, 0.0, q_r)]
    cols_per_kv = 2 * pairs_per_kv * wb
    st = []
    for c in range(SWA_KV_HEADS):
        lhs_c = jnp.concatenate(lhs[2 * c * pairs_per_kv:2 * (c + 1) * pairs_per_kv], axis=0).astype(BF16)
        lhs_c = jnp.concatenate([lhs_c, eye_ref[...]], axis=1)
        st.append(lax.dot_general(k2[c], lhs_c, _NT, preferred_element_type=F32))
    st = jnp.concatenate(st, axis=1)
    m = jnp.max(st, axis=0, keepdims=True)
    e = jnp.exp2(st - m).astype(BF16)
    sink = jnp.concatenate([jnp.full((1, wb), sinks_ref[i], F32) for i in range(SWA_HEADS)], axis=1) * LOG2E
    sink_term = jnp.exp2(sink - m)
    for c in range(SWA_KV_HEADS):
        cols = slice(c * cols_per_kv, (c + 1) * cols_per_kv)
        ot = jnp.dot(vt[c], e[:, cols], preferred_element_type=F32)
        o_n = ot[:hd] * (1.0 / (ot[hd:hd + 1] + sink_term[:, cols]))
        for n in range(pairs_per_kv):
            t = c * pairs_per_kv + n
            pair = jnp.concatenate([o_n[:, 2 * n * wb:(2 * n + 1) * wb], o_n[:, (2 * n + 1) * wb:(2 * n + 2) * wb]],
                                   axis=0)
            o_ref[:, t * LANES:(t + 1) * LANES] = pair.T.astype(BF16)

    for c in range(SWA_KV_HEADS):
        k2p_ref[c] = k2_cur[c]
        vtp_ref[c] = vt_cur[c]


def _swa(proj, sinks, cos_tab, sin_tab, batch, seq, to_cast):
    nb = seq // WINDOW
    cast_in, cast_out, cast_shapes = _cast_specs(to_cast, batch * nb, lambda b, n: b * nb + n)
    eye = jnp.tile(jnp.eye(WINDOW, dtype=BF16), (SWA_HEADS // SWA_KV_HEADS, 1))
    rows = lambda b, n: b * nb + n
    return pl.pallas_call(
        _swa_kernel,
        out_shape=[jax.ShapeDtypeStruct((batch * seq, SWA_Q_W), BF16)] + cast_shapes,
        grid=(batch, nb),
        in_specs=[
            pl.BlockSpec(memory_space=pltpu.SMEM),
            pl.BlockSpec((WINDOW, SWA_Q_W), lambda b, n: (rows(b, n), COL_SQ // SWA_Q_W)),
            pl.BlockSpec((WINDOW, SWA_KV_W), lambda b, n: (rows(b, n), COL_SK // SWA_KV_W)),
            pl.BlockSpec((WINDOW, SWA_KV_W), lambda b, n: (rows(b, n), COL_SV // SWA_KV_W)),
            pl.BlockSpec((WINDOW, LANES), lambda b, n: (rows(b, n), 0)),
            pl.BlockSpec((WINDOW, LANES), lambda b, n: (rows(b, n), 0)),
            pl.BlockSpec((SWA_HEADS // SWA_KV_HEADS * WINDOW, WINDOW), lambda b, n: (0, 0)),
        ] + cast_in,
        out_specs=[pl.BlockSpec((WINDOW, SWA_Q_W), lambda b, n: (rows(b, n), 0))] + cast_out,
        scratch_shapes=[pltpu.VMEM((SWA_KV_HEADS, WINDOW, LANES), BF16),
                        pltpu.VMEM((SWA_KV_HEADS, SWA_VT_ROWS, WINDOW), BF16)],
        compiler_params=pltpu.CompilerParams(
            dimension_semantics=("arbitrary", "arbitrary"),
            vmem_limit_bytes=VMEM_LIMIT),
        name="swa",
    )(sinks, proj, proj, proj, cos_tab, sin_tab, eye, *to_cast)


def _outproj_kernel(x_ref, a_ref, b_ref, wa_ref, wb_ref, h_ref, *, tm, rc):
    for r in range(tm // rc):
        rs = pl.ds(r * rc, rc)
        acc = jnp.dot(a_ref[rs, :], wa_ref[...], preferred_element_type=F32)
        acc = acc + jnp.dot(b_ref[rs, :], wb_ref[...], preferred_element_type=F32)
        h_ref[rs, :] = x_ref[rs, :] + acc


def _outproj(x2, o_gla, o_swa, w, tm, rc):
    t = x2.shape[0]
    return pl.pallas_call(
        functools.partial(_outproj_kernel, tm=tm, rc=rc),
        out_shape=jax.ShapeDtypeStruct((t, D_MODEL), F32),
        grid=(t // tm,),
        in_specs=[
            pl.BlockSpec((tm, D_MODEL), lambda i: (i, 0)),
            pl.BlockSpec((tm, GLA_V_W), lambda i: (i, 0)),
            pl.BlockSpec((tm, SWA_Q_W), lambda i: (i, 0)),
            pl.BlockSpec((GLA_V_W, D_MODEL), lambda i: (0, 0), pipeline_mode=pl.Buffered(1)),
            pl.BlockSpec((SWA_Q_W, D_MODEL), lambda i: (GLA_V_W // SWA_Q_W, 0), pipeline_mode=pl.Buffered(1)),
        ],
        out_specs=pl.BlockSpec((tm, D_MODEL), lambda i: (i, 0)),
        compiler_params=pltpu.CompilerParams(
            dimension_semantics=("arbitrary",),
            vmem_limit_bytes=VMEM_LIMIT),
        name="outproj",
    )(x2, o_gla, o_swa, w, w)


def _ffn_kernel(h_ref, gf_ref, wg_ref, wu_ref, cw_ref, cb_ref, wd_ref, gl_ref, o_ref,
                hn_ref, carry_ref, *, tm, rc, seq):
    i = pl.program_id(0)
    j = pl.program_id(1)
    nj = pl.num_programs(1)

    def step(first, last):
        seq_start = (i * tm) % seq == 0
        prev = jnp.where(seq_start, 0.0, carry_ref[j])
        rows = lax.broadcasted_iota(jnp.int32, prev.shape, 0)
        cb = cb_ref[...]
        cw0, cw1, cw2 = cw_ref[0:1, :], cw_ref[1:2, :], cw_ref[2:3, :]

        for r in range(tm // rc):
            rs = pl.ds(r * rc, rc)
            if first:
                res = h_ref[rs, :]
                hn = _rms(res, gf_ref[...]).astype(BF16)
                hn_ref[rs, :] = hn
            else:
                res = o_ref[rs, :]
                hn = hn_ref[rs, :]
            gate = jnp.dot(hn, wg_ref[...], preferred_element_type=F32)
            up = jnp.dot(hn, wu_ref[...], preferred_element_type=F32)

            def shifted(d):
                rolled = pltpu.roll(gate, d, 0)
                top = jnp.where(rows < d, pltpu.roll(prev, d, 0), rolled[:SUBLANES])
                return jnp.concatenate([top, rolled[SUBLANES:]], axis=0)

            conv = cb + cw0 * shifted(2) + cw1 * shifted(1) + cw2 * gate
            act = (conv * jax.nn.sigmoid(conv) * up).astype(BF16)
            acc = res + jnp.dot(act, wd_ref[...], preferred_element_type=F32)
            o_ref[rs, :] = _rms(acc, gl_ref[...]) if last else acc
            prev = gate[rc - SUBLANES:, :]
        carry_ref[j] = prev

    pl.when(j == 0)(functools.partial(step, True, False))
    pl.when((j > 0) & (j < nj - 1))(functools.partial(step, False, False))
    pl.when(j == nj - 1)(functools.partial(step, False, True))


def _ffn(h, gf, wg, wu, cw, cb, wd, gl, seq, tm, tf, rc):
    t = h.shape[0]
    nj = D_FF // tf
    kern = functools.partial(_ffn_kernel, tm=tm, rc=rc, seq=seq)
    return pl.pallas_call(
        kern,
        out_shape=jax.ShapeDtypeStruct((t, D_MODEL), F32),
        grid=(t // tm, nj),
        in_specs=[
            pl.BlockSpec((tm, D_MODEL), lambda i, j: (i, 0)),
            pl.BlockSpec((1, D_MODEL), lambda i, j: (0, 0)),
            pl.BlockSpec((D_MODEL, tf), lambda i, j: (0, j)),
            pl.BlockSpec((D_MODEL, tf), lambda i, j: (0, j)),
            pl.BlockSpec((CONV_WIDTH, tf), lambda i, j: (0, j)),
            pl.BlockSpec((1, tf), lambda i, j: (0, j)),
            pl.BlockSpec((tf, D_MODEL), lambda i, j: (j, 0)),
            pl.BlockSpec((1, D_MODEL), lambda i, j: (0, 0)),
        ],
        out_specs=pl.BlockSpec((tm, D_MODEL), lambda i, j: (i, 0)),
        scratch_shapes=[pltpu.VMEM((tm, D_MODEL), BF16), pltpu.VMEM((nj, SUBLANES, tf), F32)],
        compiler_params=pltpu.CompilerParams(
            dimension_semantics=("arbitrary", "arbitrary"),
            vmem_limit_bytes=VMEM_LIMIT),
        name="convffn",
    )(h, gf, wg, wu, cw, cb, wd, gl)


def kernel(x, positions, attn_norm, w_in, w_a_up, b_a_up, gla_norm, sinks, w_out, ffn_norm,
           w_gate, w_up, conv_w, conv_b, w_down, final_norm):
    batch, seq, _ = x.shape
    t = batch * seq
    assert w_in.shape[0] == 1, "the final norm is fused into the single layer's FFN kernel"
    x2 = x.reshape(t, D_MODEL)
    pos3 = positions.reshape(t // WINDOW, 1, WINDOW)
    half = SWA_HEAD_DIM // 2
    freq_col = (ROPE_THETA ** (-jnp.arange(half, dtype=F32) / half)).reshape(half, 1)

    w_proj = _wprep(jnp.swapaxes(w_in[0], 0, 1), tk=256)
    wa = jnp.concatenate(
        [w_a_up[0], jnp.zeros((LANES - GLA_RANK, GLA_QK_W), F32)], axis=0).astype(BF16)

    proj, cos_tab, sin_tab, wg_bf, wu_bf = _inproj(x2, attn_norm[0].reshape(1, D_MODEL), w_proj, pos3, freq_col,
                                                   tm=512, rc=256, to_cast=[w_gate[0], w_up[0]])
    o_gla, wd_bf, wo_bf = _gla(proj, wa, b_a_up[0].reshape(1, GLA_QK_W), gla_norm[0].reshape(1, GLA_DV),
                               batch, seq, ts=1024, to_cast=[w_down[0], w_out[0]])
    o_swa, = _swa(proj, sinks[0], cos_tab, sin_tab, batch, seq, to_cast=[])
    h = _outproj(x2, o_gla, o_swa, wo_bf, tm=1024, rc=512)
    y = _ffn(h, ffn_norm[0].reshape(1, D_MODEL), wg_bf, wu_bf, conv_w[0], conv_b[0].reshape(1, D_FF), wd_bf,
             final_norm.reshape(1, D_MODEL), seq, tm=1024, tf=512, rc=512)
    return y.reshape(batch, seq, D_MODEL)
```

```python
import functools

import jax
import jax.numpy as jnp
from jax import lax
from jax.experimental import pallas as pl
from jax.experimental.pallas import tpu as pltpu

D_MODEL = 2048
GLA_HEADS = 4
GLA_DK = 128
GLA_DV = 256
GLA_RANK = 16
GLA_GATE_NORM = 16.0
GLA_CHUNK = 64
GLA_SUB = 256
SWA_HEADS = 16
SWA_KV_HEADS = 2
SWA_HEAD_DIM = 64
WINDOW = 128
ROPE_THETA = 10000.0
D_FF = 5632
CONV_WIDTH = 3
EPS = 1e-6
NEG_INF = -1e30
LOG2E = 1.4426950408889634
SWA_VT_ROWS = SWA_HEAD_DIM + 16
SWA_BLOCKS = 8

GLA_QK_W = GLA_HEADS * GLA_DK
GLA_V_W = GLA_HEADS * GLA_DV
SWA_Q_W = SWA_HEADS * SWA_HEAD_DIM
SWA_KV_W = SWA_KV_HEADS * SWA_HEAD_DIM

LANES = 128
SUBLANES = 8

COL_GQ = 0
COL_GK = COL_GQ + GLA_QK_W
COL_GV = COL_GK + GLA_QK_W
COL_GG = COL_GV + GLA_V_W
COL_SQ = COL_GG + GLA_V_W
COL_SK = COL_SQ + SWA_Q_W
COL_SV = COL_SK + SWA_KV_W
COL_GA = COL_SV + SWA_KV_W
PROJ_TN = 1536
PROJ_W = 3 * PROJ_TN

VMEM_LIMIT = 58 * 1024 * 1024

F32 = jnp.float32
BF16 = jnp.bfloat16

_NT = (((1,), (1,)), ((), ()))
_TN = (((0,), (0,)), ((), ()))


def _rms(x, gain):
    return x * lax.rsqrt(jnp.mean(x * x, axis=-1, keepdims=True) + EPS) * gain


def _cast_specs(weights, steps, step_index):
    in_specs, out_specs, out_shapes = [], [], []
    for w in weights:
        rows, cols = w.shape
        slab = rows // steps
        assert slab * steps == rows and slab % (2 * SUBLANES) == 0, (w.shape, steps)
        spec = pl.BlockSpec((slab, cols), lambda *g: (step_index(*g), 0))
        in_specs.append(spec)
        out_specs.append(spec)
        out_shapes.append(jax.ShapeDtypeStruct(w.shape, BF16))
    return in_specs, out_specs, out_shapes


def _cast_slabs(src_refs, dst_refs):
    for src, dst in zip(src_refs, dst_refs):
        dst[...] = src[...].astype(BF16)


def _wprep_kernel(wt_ref, o_ref):
    tk = wt_ref.shape[1]
    half = SWA_HEAD_DIM // 2
    o_ga = 2 * GLA_QK_W + 2 * GLA_V_W
    o_sq = o_ga + GLA_RANK

    def put(col, rows):
        o_ref[:, col:col + rows.shape[0]] = rows.T.astype(BF16)

    put(0, wt_ref[:o_ga, :])
    for i in range((SWA_Q_W + SWA_KV_W) // LANES):
        r0 = o_sq + i * LANES
        blk = [wt_ref[r0 + j * half:r0 + (j + 1) * half, :] for j in range(4)]
        put(COL_SQ + i * LANES, jnp.concatenate([blk[0], blk[2], blk[1], blk[3]], axis=0))
    put(COL_SV, wt_ref[o_sq + SWA_Q_W + SWA_KV_W:, :])
    keep = lax.broadcasted_iota(jnp.int32, (LANES, tk), 0) < GLA_RANK
    put(COL_GA, jnp.where(keep, wt_ref[o_ga:o_ga + LANES, :], 0.0))
    o_ref[:, COL_GA + LANES:] = jnp.zeros((tk, PROJ_W - COL_GA - LANES), BF16)


def _wprep(wt, tk):
    n, k = wt.shape
    return pl.pallas_call(
        _wprep_kernel,
        out_shape=jax.ShapeDtypeStruct((k, PROJ_W), BF16),
        grid=(k // tk,),
        in_specs=[pl.BlockSpec((n, tk), lambda i: (0, i))],
        out_specs=pl.BlockSpec((tk, PROJ_W), lambda i: (i, 0)),
        compiler_params=pltpu.CompilerParams(
            dimension_semantics=("arbitrary",),
            vmem_limit_bytes=VMEM_LIMIT),
        name="wprep",
    )(wt)


def _inproj_kernel(x_ref, g_ref, w_ref, pos_ref, freq_ref, *rest, tm, rc):
    n_cast = (len(rest) - 3) // 2
    o_ref, cos_ref, sin_ref = rest[n_cast:n_cast + 3]
    _cast_slabs(rest[:n_cast], rest[n_cast + 3:])

    half = SWA_HEAD_DIM // 2
    low = lax.broadcasted_iota(jnp.int32, (WINDOW, LANES), 1) < SWA_HEAD_DIM
    for b in range(tm // WINDOW):
        ang_t = freq_ref[...] * pos_ref[b].astype(F32)
        cs = jnp.concatenate([jnp.cos(ang_t)] * (LANES // half), axis=0).T
        sn = jnp.concatenate([jnp.sin(ang_t)] * (LANES // half), axis=0).T
        cos_ref[b * WINDOW:(b + 1) * WINDOW, :] = cs
        sin_ref[b * WINDOW:(b + 1) * WINDOW, :] = jnp.where(low, -sn, sn)

    for r in range(tm // rc):
        rs = pl.ds(r * rc, rc)
        u = _rms(x_ref[rs, :], g_ref[...]).astype(BF16)
        for c in range(PROJ_W // PROJ_TN):
            cols = pl.ds(c * PROJ_TN, PROJ_TN)
            o_ref[rs, cols] = jnp.dot(u, w_ref[:, cols], preferred_element_type=F32).astype(BF16)


def _inproj(x2, gain, w, pos3, freq_col, tm, rc, to_cast):
    t = x2.shape[0]
    cast_in, cast_out, cast_shapes = _cast_specs(to_cast, t // tm, lambda i: i)
    return pl.pallas_call(
        functools.partial(_inproj_kernel, tm=tm, rc=rc),
        out_shape=[jax.ShapeDtypeStruct((t, PROJ_W), BF16), jax.ShapeDtypeStruct((t, LANES), F32),
                   jax.ShapeDtypeStruct((t, LANES), F32)] + cast_shapes,
        grid=(t // tm,),
        in_specs=[
            pl.BlockSpec((tm, D_MODEL), lambda i: (i, 0)),
            pl.BlockSpec((1, D_MODEL), lambda i: (0, 0)),
            pl.BlockSpec((D_MODEL, PROJ_W), lambda i: (0, 0), pipeline_mode=pl.Buffered(1)),
            pl.BlockSpec((tm // WINDOW, 1, WINDOW), lambda i: (i, 0, 0)),
            pl.BlockSpec((SWA_HEAD_DIM // 2, 1), lambda i: (0, 0)),
        ] + cast_in,
        out_specs=[pl.BlockSpec((tm, PROJ_W), lambda i: (i, 0)), pl.BlockSpec((tm, LANES), lambda i: (i, 0)),
                   pl.BlockSpec((tm, LANES), lambda i: (i, 0))] + cast_out,
        compiler_params=pltpu.CompilerParams(
            dimension_semantics=("arbitrary",),
            vmem_limit_bytes=VMEM_LIMIT),
        name="inproj",
    )(x2, gain, w, pos3, freq_col, *to_cast)


def _gla_kernel(q_ref, k_ref, v_ref, g_ref, a_ref, wa_ref, ba_ref, gn_ref, tril_ref, mask_ref, *rest, ts):
    c = GLA_CHUNK
    nc = ts // c
    n_cast = (len(rest) - 2) // 2
    o_ref, s_ref = rest[n_cast], rest[-1]
    _cast_slabs(rest[:n_cast], rest[n_cast + 1:-1])
    heads = range(GLA_HEADS)
    ks = [slice(h * GLA_DK, (h + 1) * GLA_DK) for h in heads]
    vs = [slice(h * GLA_DV, (h + 1) * GLA_DV) for h in heads]

    @pl.when(pl.program_id(1) == 0)
    def _():
        s_ref[...] = jnp.zeros_like(s_ref)

    z = jnp.dot(a_ref[...], wa_ref[...], preferred_element_type=F32) + ba_ref[...]
    log_a = (jnp.minimum(z, 0.0) - jnp.log(1.0 + jnp.exp(-jnp.abs(z)))) * (1.0 / GLA_GATE_NORM)

    la_hi = log_a.astype(BF16)
    la_lo = (log_a - la_hi.astype(F32)).astype(BF16)
    la_hl = jnp.concatenate([la_hi, la_lo], axis=1)
    subs = [slice(s * GLA_SUB, (s + 1) * GLA_SUB) for s in range(ts // GLA_SUB)]
    cum2 = jnp.concatenate([jnp.dot(tril_ref[...], la_hl[rs], preferred_element_type=F32) for rs in subs], axis=0)
    bcum = cum2[:, :GLA_QK_W] + cum2[:, GLA_QK_W:]
    bcum3 = bcum.reshape(nc, c, GLA_QK_W)
    b_last = bcum3[:, c - 1:c, :]

    q = q_ref[...].astype(F32) * (GLA_DK ** -0.5)
    k = k_ref[...].astype(F32)
    q_e = (q * jnp.exp(bcum)).astype(BF16)
    k_e = (k * jnp.exp(-bcum)).astype(BF16)
    k_d = (k * jnp.exp(b_last - bcum3).reshape(ts, GLA_QK_W)).astype(BF16)
    decay = jnp.exp(b_last)
    v = [v_ref[:, vs[h]] for h in heads]

    keep = mask_ref[...] != 0.0

    for h in heads:
        o_intra = []
        for rs in subs:
            scores = lax.dot_general(q_e[rs, ks[h]], k_e[rs, ks[h]], _NT, preferred_element_type=F32)
            scores = jnp.where(keep, scores, 0.0).astype(BF16)
            o_intra.append(jnp.dot(scores, v[h][rs], preferred_element_type=F32))
        o_intra = jnp.concatenate(o_intra, axis=0)

        state_t = s_ref[h]
        o_inter = []
        for n in range(nc):
            rows = slice(n * c, (n + 1) * c)
            o_inter.append(lax.dot_general(q_e[rows, ks[h]], state_t.astype(BF16), _NT,
                                           preferred_element_type=F32))
            kv_t = lax.dot_general(v[h][rows], k_d[rows, ks[h]], _TN, preferred_element_type=F32)
            state_t = state_t * decay[n][:, ks[h]] + kv_t
        s_ref[h] = state_t

        o = _rms(o_intra + jnp.concatenate(o_inter, axis=0), gn_ref[...])
        gate = g_ref[:, vs[h]].astype(F32)
        o_ref[:, vs[h]] = (o * (gate * jax.nn.sigmoid(gate))).astype(BF16)


def _gla(proj, wa, ba, gn, batch, seq, ts, to_cast):
    nt = seq // ts
    cast_in, cast_out, cast_shapes = _cast_specs(to_cast, batch * nt, lambda b, i: b * nt + i)
    kern = functools.partial(_gla_kernel, ts=ts)
    rows = lambda b, i: b * nt + i
    idx = jnp.arange(GLA_SUB)
    causal = (idx[:, None] >= idx[None, :]) & (idx[:, None] // GLA_CHUNK == idx[None, :] // GLA_CHUNK)
    return pl.pallas_call(
        kern,
        out_shape=[jax.ShapeDtypeStruct((batch * seq, GLA_V_W), BF16)] + cast_shapes,
        grid=(batch, nt),
        in_specs=[
            pl.BlockSpec((ts, GLA_QK_W), lambda b, i: (rows(b, i), COL_GQ // GLA_QK_W)),
            pl.BlockSpec((ts, GLA_QK_W), lambda b, i: (rows(b, i), COL_GK // GLA_QK_W)),
            pl.BlockSpec((ts, GLA_V_W), lambda b, i: (rows(b, i), COL_GV // GLA_V_W)),
            pl.BlockSpec((ts, GLA_V_W), lambda b, i: (rows(b, i), COL_GG // GLA_V_W)),
            pl.BlockSpec((ts, LANES), lambda b, i: (rows(b, i), COL_GA // LANES)),
            pl.BlockSpec((LANES, GLA_QK_W), lambda b, i: (0, 0)),
            pl.BlockSpec((1, GLA_QK_W), lambda b, i: (0, 0)),
            pl.BlockSpec((1, GLA_DV), lambda b, i: (0, 0)),
            pl.BlockSpec((GLA_SUB, GLA_SUB), lambda b, i: (0, 0)),
            pl.BlockSpec((GLA_SUB, GLA_SUB), lambda b, i: (0, 0)),
        ] + cast_in,
        out_specs=[pl.BlockSpec((ts, GLA_V_W), lambda b, i: (rows(b, i), 0))] + cast_out,
        scratch_shapes=[pltpu.VMEM((GLA_HEADS, GLA_DV, GLA_DK), F32)],
        compiler_params=pltpu.CompilerParams(
            dimension_semantics=("arbitrary", "arbitrary"),
            vmem_limit_bytes=VMEM_LIMIT),
        name="gla",
    )(proj, proj, proj, proj, proj, wa, ba, gn, causal.astype(BF16), causal.astype(F32), *to_cast)


def _swa_kernel(sinks_ref, q_ref, k_ref, v_ref, cos_ref, sin_ref, eye_ref, *rest):
    n_cast = (len(rest) - 3) // 2
    o_ref = rest[n_cast]
    k2p_ref, vtp_ref = rest[-2:]
    _cast_slabs(rest[:n_cast], rest[n_cast + 1:-2])
    step = pl.program_id(1)
    wb = WINDOW
    hd = SWA_HEAD_DIM
    half = hd // 2
    nblk = SWA_BLOCKS
    rows = nblk * wb
    pairs_per_kv = SWA_Q_W // LANES // SWA_KV_HEADS
    kvs = range(SWA_KV_HEADS)
    lane = lax.broadcasted_iota(jnp.int32, (rows, LANES), 1)
    head_a = (lane % hd) < half

    @pl.when(step == 0)
    def _():
        k2p_ref[...] = jnp.zeros_like(k2p_ref)
        vtp_ref[...] = jnp.zeros_like(vtp_ref)

    cos = cos_ref[...]
    sin = sin_ref[...]

    def rope(t, cs, sn):
        return t * cs + pltpu.roll(t, hd, 1) * sn

    k_r = rope(k_ref[...].astype(F32), cos, sin)
    k2_cur = [jnp.where(head_a, k_r, pltpu.roll(k_r, half, 1)).astype(BF16),
              jnp.where(head_a, pltpu.roll(k_r, LANES - half, 1), k_r).astype(BF16)]
    k2_all = [jnp.concatenate([k2p_ref[c], k2_cur[c]], axis=0) for c in kvs]
    v_t = v_ref[...].astype(F32).T
    ones_rows = (lax.broadcasted_iota(jnp.int32, (SWA_VT_ROWS - hd, rows), 0) == 0).astype(F32)
    vt_cur = [jnp.concatenate([v_t[c * hd:(c + 1) * hd], ones_rows], axis=0).astype(BF16) for c in kvs]
    vt_all = [jnp.concatenate([vtp_ref[c], vt_cur[c]], axis=1) for c in kvs]

    ki = lax.broadcasted_iota(jnp.int32, (2 * wb, wb), 0)
    qi = lax.broadcasted_iota(jnp.int32, (2 * wb, wb), 1) + wb
    band = (ki <= qi) & (qi - ki < WINDOW)
    bias = jnp.where(band, 0.0, NEG_INF).astype(BF16)
    bias_first = jnp.where(band & ((step > 0) | (ki >= wb)), 0.0, NEG_INF).astype(BF16)

    qscale = (hd ** -0.5) * LOG2E
    cos_q = cos * qscale
    sin_q = sin * qscale

    n_tiles = SWA_Q_W // LANES
    lhs = []
    for t in range(n_tiles):
        q_r = rope(q_ref[:, t * LANES:(t + 1) * LANES].astype(F32), cos_q, sin_q)
        lhs.append([jnp.where(head_a, q_r, 0.0).astype(BF16), jnp.where(head_a, 0.0, q_r).astype(BF16)])
    st = []
    for x in range(nblk):
        for c in kvs:
            rows_x = slice(x * wb, (x + 1) * wb)
            lhs_xc = jnp.concatenate([lhs[t][i][rows_x] for t in range(c * pairs_per_kv, (c + 1) * pairs_per_kv)
                                      for i in range(2)], axis=0)
            lhs_xc = jnp.concatenate([lhs_xc, eye_ref[...]], axis=1)
            keys = jnp.concatenate([k2_all[c][x * wb:(x + 2) * wb], bias_first if x == 0 else bias], axis=1)
            st.append(lax.dot_general(keys, lhs_xc, _NT, preferred_element_type=F32))
    st = jnp.concatenate(st, axis=1)
    m = jnp.max(st, axis=0, keepdims=True)
    e = jnp.exp2(st - m).astype(BF16)
    sink = jnp.concatenate([jnp.full((1, wb), sinks_ref[i], F32) for i in range(SWA_HEADS)] * nblk, axis=1) * LOG2E
    sink_term = jnp.exp2(sink - m)
    cols_per_kv = 2 * pairs_per_kv * wb
    for x in range(nblk):
        for c in kvs:
            cols = slice((x * SWA_KV_HEADS + c) * cols_per_kv, (x * SWA_KV_HEADS + c + 1) * cols_per_kv)
            ot = jnp.dot(vt_all[c][:, x * wb:(x + 2) * wb], e[:, cols], preferred_element_type=F32)
            o_n = ot[:hd] * (1.0 / (ot[hd:hd + 1] + sink_term[:, cols]))
            for n in range(pairs_per_kv):
                t = c * pairs_per_kv + n
                pair = jnp.concatenate([o_n[:, 2 * n * wb:(2 * n + 1) * wb], o_n[:, (2 * n + 1) * wb:(2 * n + 2) * wb]],
                                       axis=0)
                o_ref[x * wb:(x + 1) * wb, t * LANES:(t + 1) * LANES] = pair.T.astype(BF16)

    for c in kvs:
        k2p_ref[c] = k2_cur[c][rows - wb:]
        vtp_ref[c] = vt_cur[c][:, rows - wb:]


def _swa(proj, sinks, cos_tab, sin_tab, batch, seq, to_cast):
    rows = SWA_BLOCKS * WINDOW
    ns = seq // rows
    cast_in, cast_out, cast_shapes = _cast_specs(to_cast, batch * ns, lambda b, n: b * ns + n)
    eye = jnp.tile(jnp.eye(WINDOW, dtype=BF16), (SWA_HEADS // SWA_KV_HEADS, 1))
    tile = lambda b, n: b * ns + n
    return pl.pallas_call(
        _swa_kernel,
        out_shape=[jax.ShapeDtypeStruct((batch * seq, SWA_Q_W), BF16)] + cast_shapes,
        grid=(batch, ns),
        in_specs=[
            pl.BlockSpec(memory_space=pltpu.SMEM),
            pl.BlockSpec((rows, SWA_Q_W), lambda b, n: (tile(b, n), COL_SQ // SWA_Q_W)),
            pl.BlockSpec((rows, SWA_KV_W), lambda b, n: (tile(b, n), COL_SK // SWA_KV_W)),
            pl.BlockSpec((rows, SWA_KV_W), lambda b, n: (tile(b, n), COL_SV // SWA_KV_W)),
            pl.BlockSpec((rows, LANES), lambda b, n: (tile(b, n), 0)),
            pl.BlockSpec((rows, LANES), lambda b, n: (tile(b, n), 0)),
            pl.BlockSpec((SWA_HEADS // SWA_KV_HEADS * WINDOW, WINDOW), lambda b, n: (0, 0)),
        ] + cast_in,
        out_specs=[pl.BlockSpec((rows, SWA_Q_W), lambda b, n: (tile(b, n), 0))] + cast_out,
        scratch_shapes=[pltpu.VMEM((SWA_KV_HEADS, WINDOW, LANES), BF16),
                        pltpu.VMEM((SWA_KV_HEADS, SWA_VT_ROWS, WINDOW), BF16)],
        compiler_params=pltpu.CompilerParams(
            dimension_semantics=("arbitrary", "arbitrary"),
            vmem_limit_bytes=VMEM_LIMIT),
        name="swa",
    )(sinks, proj, proj, proj, cos_tab, sin_tab, eye, *to_cast)


def _outproj_kernel(x_ref, a_ref, b_ref, wa_ref, wb_ref, h_ref, *, tm, rc):
    for r in range(tm // rc):
        rs = pl.ds(r * rc, rc)
        acc = jnp.dot(a_ref[rs, :], wa_ref[...], preferred_element_type=F32)
        acc = acc + jnp.dot(b_ref[rs, :], wb_ref[...], preferred_element_type=F32)
        h_ref[rs, :] = x_ref[rs, :] + acc


def _outproj(x2, o_gla, o_swa, w, tm, rc):
    t = x2.shape[0]
    return pl.pallas_call(
        functools.partial(_outproj_kernel, tm=tm, rc=rc),
        out_shape=jax.ShapeDtypeStruct((t, D_MODEL), F32),
        grid=(t // tm,),
        in_specs=[
            pl.BlockSpec((tm, D_MODEL), lambda i: (i, 0)),
            pl.BlockSpec((tm, GLA_V_W), lambda i: (i, 0)),
            pl.BlockSpec((tm, SWA_Q_W), lambda i: (i, 0)),
            pl.BlockSpec((GLA_V_W, D_MODEL), lambda i: (0, 0), pipeline_mode=pl.Buffered(1)),
            pl.BlockSpec((SWA_Q_W, D_MODEL), lambda i: (GLA_V_W // SWA_Q_W, 0), pipeline_mode=pl.Buffered(1)),
        ],
        out_specs=pl.BlockSpec((tm, D_MODEL), lambda i: (i, 0)),
        compiler_params=pltpu.CompilerParams(
            dimension_semantics=("arbitrary",),
            vmem_limit_bytes=VMEM_LIMIT),
        name="outproj",
    )(x2, o_gla, o_swa, w, w)


def _ffn_kernel(h_ref, gf_ref, wg_ref, wu_ref, cw_ref, cb_ref, wd_ref, gl_ref, o_ref,
                hn_ref, carry_ref, *, tm, rc, seq):
    i = pl.program_id(0)
    j = pl.program_id(1)
    nj = pl.num_programs(1)

    def step(first, last):
        seq_start = (i * tm) % seq == 0
        prev = jnp.where(seq_start, 0.0, carry_ref[j])
        rows = lax.broadcasted_iota(jnp.int32, prev.shape, 0)
        cb = cb_ref[...]
        cw0, cw1, cw2 = cw_ref[0:1, :], cw_ref[1:2, :], cw_ref[2:3, :]

        for r in range(tm // rc):
            rs = pl.ds(r * rc, rc)
            if first:
                res = h_ref[rs, :]
                hn = _rms(res, gf_ref[...]).astype(BF16)
                hn_ref[rs, :] = hn
            else:
                res = o_ref[rs, :]
                hn = hn_ref[rs, :]
            gate = jnp.dot(hn, wg_ref[...], preferred_element_type=F32)
            up = jnp.dot(hn, wu_ref[...], preferred_element_type=F32)

            def shifted(d):
                rolled = pltpu.roll(gate, d, 0)
                top = jnp.where(rows < d, pltpu.roll(prev, d, 0), rolled[:SUBLANES])
                return jnp.concatenate([top, rolled[SUBLANES:]], axis=0)

            conv = cb + cw0 * shifted(2) + cw1 * shifted(1) + cw2 * gate
            act = (conv * jax.nn.sigmoid(conv) * up).astype(BF16)
            acc = res + jnp.dot(act, wd_ref[...], preferred_element_type=F32)
            o_ref[rs, :] = _rms(acc, gl_ref[...]) if last else acc
            prev = gate[rc - SUBLANES:, :]
        carry_ref[j] = prev

    pl.when(j == 0)(functools.partial(step, True, False))
    pl.when((j > 0) & (j < nj - 1))(functools.partial(step, False, False))
    pl.when(j == nj - 1)(functools.partial(step, False, True))


def _ffn(h, gf, wg, wu, cw, cb, wd, gl, seq, tm, tf, rc):
    t = h.shape[0]
    nj = D_FF // tf
    kern = functools.partial(_ffn_kernel, tm=tm, rc=rc, seq=seq)
    return pl.pallas_call(
        kern,
        out_shape=jax.ShapeDtypeStruct((t, D_MODEL), F32),
        grid=(t // tm, nj),
        in_specs=[
            pl.BlockSpec((tm, D_MODEL), lambda i, j: (i, 0)),
            pl.BlockSpec((1, D_MODEL), lambda i, j: (0, 0)),
            pl.BlockSpec((D_MODEL, tf), lambda i, j: (0, j)),
            pl.BlockSpec((D_MODEL, tf), lambda i, j: (0, j)),
            pl.BlockSpec((CONV_WIDTH, tf), lambda i, j: (0, j)),
            pl.BlockSpec((1, tf), lambda i, j: (0, j)),
            pl.BlockSpec((tf, D_MODEL), lambda i, j: (j, 0)),
            pl.BlockSpec((1, D_MODEL), lambda i, j: (0, 0)),
        ],
        out_specs=pl.BlockSpec((tm, D_MODEL), lambda i, j: (i, 0)),
        scratch_shapes=[pltpu.VMEM((tm, D_MODEL), BF16), pltpu.VMEM((nj, SUBLANES, tf), F32)],
        compiler_params=pltpu.CompilerParams(
            dimension_semantics=("arbitrary", "arbitrary"),
            vmem_limit_bytes=VMEM_LIMIT),
        name="convffn",
    )(h, gf, wg, wu, cw, cb, wd, gl)


def kernel(x, positions, attn_norm, w_in, w_a_up, b_a_up, gla_norm, sinks, w_out, ffn_norm,
           w_gate, w_up, conv_w, conv_b, w_down, final_norm):
    batch, seq, _ = x.shape
    t = batch * seq
    assert w_in.shape[0] == 1, "the final norm is fused into the single layer's FFN kernel"
    x2 = x.reshape(t, D_MODEL)
    pos3 = positions.reshape(t // WINDOW, 1, WINDOW)
    half = SWA_HEAD_DIM // 2
    freq_col = (ROPE_THETA ** (-jnp.arange(half, dtype=F32) / half)).reshape(half, 1)

    w_proj = _wprep(jnp.swapaxes(w_in[0], 0, 1), tk=256)
    wa = jnp.concatenate(
        [w_a_up[0], jnp.zeros((LANES - GLA_RANK, GLA_QK_W), F32)], axis=0).astype(BF16)

    proj, cos_tab, sin_tab, wg_bf, wu_bf = _inproj(x2, attn_norm[0].reshape(1, D_MODEL), w_proj, pos3, freq_col,
                                                   tm=512, rc=256, to_cast=[w_gate[0], w_up[0]])
    o_gla, wd_bf, wo_bf = _gla(proj, wa, b_a_up[0].reshape(1, GLA_QK_W), gla_norm[0].reshape(1, GLA_DV),
                               batch, seq, ts=1024, to_cast=[w_down[0], w_out[0]])
    o_swa, = _swa(proj, sinks[0], cos_tab, sin_tab, batch, seq, to_cast=[])
    h = _outproj(x2, o_gla, o_swa, wo_bf, tm=1024, rc=512)
    y = _ffn(h, ffn_norm[0].reshape(1, D_MODEL), wg_bf, wu_bf, conv_w[0], conv_b[0].reshape(1, D_FF), wd_bf,
             final_norm.reshape(1, D_MODEL), seq, tm=1024, tf=512, rc=512)
    return y.reshape(batch, seq, D_MODEL)
```

```python
import functools

import jax
import jax.numpy as jnp
from jax import lax
from jax.experimental import pallas as pl
from jax.experimental.pallas import tpu as pltpu

D_MODEL = 2048
GLA_HEADS = 4
GLA_DK = 128
GLA_DV = 256
GLA_RANK = 16
GLA_GATE_NORM = 16.0
GLA_CHUNK = 64
GLA_SUB = 256
SWA_HEADS = 16
SWA_KV_HEADS = 2
SWA_HEAD_DIM = 64
WINDOW = 128
ROPE_THETA = 10000.0
D_FF = 5632
CONV_WIDTH = 3
EPS = 1e-6
NEG_INF = -1e30
LOG2E = 1.4426950408889634
SWA_VT_ROWS = SWA_HEAD_DIM + 16
SWA_BLOCKS = 4

GLA_QK_W = GLA_HEADS * GLA_DK
GLA_V_W = GLA_HEADS * GLA_DV
SWA_Q_W = SWA_HEADS * SWA_HEAD_DIM
SWA_KV_W = SWA_KV_HEADS * SWA_HEAD_DIM

LANES = 128
SUBLANES = 8

COL_GQ = 0
COL_GK = COL_GQ + GLA_QK_W
COL_GV = COL_GK + GLA_QK_W
COL_GG = COL_GV + GLA_V_W
COL_SQ = COL_GG + GLA_V_W
COL_SK = COL_SQ + SWA_Q_W
COL_SV = COL_SK + SWA_KV_W
COL_GA = COL_SV + SWA_KV_W
PROJ_TN = 1536
PROJ_W = 3 * PROJ_TN

VMEM_LIMIT = 58 * 1024 * 1024

F32 = jnp.float32
BF16 = jnp.bfloat16

_NT = (((1,), (1,)), ((), ()))
_TN = (((0,), (0,)), ((), ()))


def _rms(x, gain):
    return x * lax.rsqrt(jnp.mean(x * x, axis=-1, keepdims=True) + EPS) * gain


def _cast_specs(weights, steps, step_index):
    in_specs, out_specs, out_shapes = [], [], []
    for w in weights:
        rows, cols = w.shape
        slab = rows // steps
        assert slab * steps == rows and slab % (2 * SUBLANES) == 0, (w.shape, steps)
        spec = pl.BlockSpec((slab, cols), lambda *g: (step_index(*g), 0))
        in_specs.append(spec)
        out_specs.append(spec)
        out_shapes.append(jax.ShapeDtypeStruct(w.shape, BF16))
    return in_specs, out_specs, out_shapes


def _cast_slabs(src_refs, dst_refs):
    for src, dst in zip(src_refs, dst_refs):
        dst[...] = src[...].astype(BF16)


def _wprep_kernel(wt_ref, o_ref):
    tk = wt_ref.shape[1]
    half = SWA_HEAD_DIM // 2
    o_ga = 2 * GLA_QK_W + 2 * GLA_V_W
    o_sq = o_ga + GLA_RANK

    def put(col, rows):
        o_ref[:, col:col + rows.shape[0]] = rows.T.astype(BF16)

    put(0, wt_ref[:o_ga, :])
    for i in range((SWA_Q_W + SWA_KV_W) // LANES):
        r0 = o_sq + i * LANES
        blk = [wt_ref[r0 + j * half:r0 + (j + 1) * half, :] for j in range(4)]
        put(COL_SQ + i * LANES, jnp.concatenate([blk[0], blk[2], blk[1], blk[3]], axis=0))
    put(COL_SV, wt_ref[o_sq + SWA_Q_W + SWA_KV_W:, :])
    keep = lax.broadcasted_iota(jnp.int32, (LANES, tk), 0) < GLA_RANK
    put(COL_GA, jnp.where(keep, wt_ref[o_ga:o_ga + LANES, :], 0.0))
    o_ref[:, COL_GA + LANES:] = jnp.zeros((tk, PROJ_W - COL_GA - LANES), BF16)


def _wprep(wt, tk):
    n, k = wt.shape
    return pl.pallas_call(
        _wprep_kernel,
        out_shape=jax.ShapeDtypeStruct((k, PROJ_W), BF16),
        grid=(k // tk,),
        in_specs=[pl.BlockSpec((n, tk), lambda i: (0, i))],
        out_specs=pl.BlockSpec((tk, PROJ_W), lambda i: (i, 0)),
        compiler_params=pltpu.CompilerParams(
            dimension_semantics=("arbitrary",),
            vmem_limit_bytes=VMEM_LIMIT),
        name="wprep",
    )(wt)


def _inproj_kernel(x_ref, g_ref, w_ref, pos_ref, freq_ref, *rest, tm, rc):
    n_cast = (len(rest) - 3) // 2
    o_ref, cos_ref, sin_ref = rest[n_cast:n_cast + 3]
    _cast_slabs(rest[:n_cast], rest[n_cast + 3:])

    half = SWA_HEAD_DIM // 2
    low = lax.broadcasted_iota(jnp.int32, (WINDOW, LANES), 1) < SWA_HEAD_DIM
    for b in range(tm // WINDOW):
        ang_t = freq_ref[...] * pos_ref[b].astype(F32)
        cs = jnp.concatenate([jnp.cos(ang_t)] * (LANES // half), axis=0).T
        sn = jnp.concatenate([jnp.sin(ang_t)] * (LANES // half), axis=0).T
        cos_ref[b * WINDOW:(b + 1) * WINDOW, :] = cs
        sin_ref[b * WINDOW:(b + 1) * WINDOW, :] = jnp.where(low, -sn, sn)

    for r in range(tm // rc):
        rs = pl.ds(r * rc, rc)
        u = _rms(x_ref[rs, :], g_ref[...]).astype(BF16)
        for c in range(PROJ_W // PROJ_TN):
            cols = pl.ds(c * PROJ_TN, PROJ_TN)
            o_ref[rs, cols] = jnp.dot(u, w_ref[:, cols], preferred_element_type=F32).astype(BF16)


def _inproj(x2, gain, w, pos3, freq_col, tm, rc, to_cast):
    t = x2.shape[0]
    cast_in, cast_out, cast_shapes = _cast_specs(to_cast, t // tm, lambda i: i)
    return pl.pallas_call(
        functools.partial(_inproj_kernel, tm=tm, rc=rc),
        out_shape=[jax.ShapeDtypeStruct((t, PROJ_W), BF16), jax.ShapeDtypeStruct((t, LANES), F32),
                   jax.ShapeDtypeStruct((t, LANES), F32)] + cast_shapes,
        grid=(t // tm,),
        in_specs=[
            pl.BlockSpec((tm, D_MODEL), lambda i: (i, 0)),
            pl.BlockSpec((1, D_MODEL), lambda i: (0, 0)),
            pl.BlockSpec((D_MODEL, PROJ_W), lambda i: (0, 0), pipeline_mode=pl.Buffered(1)),
            pl.BlockSpec((tm // WINDOW, 1, WINDOW), lambda i: (i, 0, 0)),
            pl.BlockSpec((SWA_HEAD_DIM // 2, 1), lambda i: (0, 0)),
        ] + cast_in,
        out_specs=[pl.BlockSpec((tm, PROJ_W), lambda i: (i, 0)), pl.BlockSpec((tm, LANES), lambda i: (i, 0)),
                   pl.BlockSpec((tm, LANES), lambda i: (i, 0))] + cast_out,
        compiler_params=pltpu.CompilerParams(
            dimension_semantics=("arbitrary",),
            vmem_limit_bytes=VMEM_LIMIT),
        name="inproj",
    )(x2, gain, w, pos3, freq_col, *to_cast)


def _gla_kernel(q_ref, k_ref, v_ref, g_ref, a_ref, wa_ref, ba_ref, gn_ref, tril_ref, mask_ref, *rest, ts):
    c = GLA_CHUNK
    nc = ts // c
    n_cast = (len(rest) - 2) // 2
    o_ref, s_ref = rest[n_cast], rest[-1]
    _cast_slabs(rest[:n_cast], rest[n_cast + 1:-1])
    heads = range(GLA_HEADS)
    ks = [slice(h * GLA_DK, (h + 1) * GLA_DK) for h in heads]
    vs = [slice(h * GLA_DV, (h + 1) * GLA_DV) for h in heads]

    @pl.when(pl.program_id(1) == 0)
    def _():
        s_ref[...] = jnp.zeros_like(s_ref)

    z = jnp.dot(a_ref[...], wa_ref[...], preferred_element_type=F32) + ba_ref[...]
    log_a = (jnp.minimum(z, 0.0) - jnp.log(1.0 + jnp.exp(-jnp.abs(z)))) * (1.0 / GLA_GATE_NORM)

    la_hi = log_a.astype(BF16)
    la_lo = (log_a - la_hi.astype(F32)).astype(BF16)
    la_hl = jnp.concatenate([la_hi, la_lo], axis=1)
    subs = [slice(s * GLA_SUB, (s + 1) * GLA_SUB) for s in range(ts // GLA_SUB)]
    cum2 = jnp.concatenate([jnp.dot(tril_ref[...], la_hl[rs], preferred_element_type=F32) for rs in subs], axis=0)
    bcum = cum2[:, :GLA_QK_W] + cum2[:, GLA_QK_W:]
    bcum3 = bcum.reshape(nc, c, GLA_QK_W)
    b_last = bcum3[:, c - 1:c, :]

    q = q_ref[...].astype(F32) * (GLA_DK ** -0.5)
    k = k_ref[...].astype(F32)
    q_e = (q * jnp.exp(bcum)).astype(BF16)
    k_e = (k * jnp.exp(-bcum)).astype(BF16)
    k_d = (k * jnp.exp(b_last - bcum3).reshape(ts, GLA_QK_W)).astype(BF16)
    decay = jnp.exp(b_last)
    v = [v_ref[:, vs[h]] for h in heads]

    keep = mask_ref[...] != 0.0

    for h in heads:
        o_intra = []
        for rs in subs:
            scores = lax.dot_general(q_e[rs, ks[h]], k_e[rs, ks[h]], _NT, preferred_element_type=F32)
            scores = jnp.where(keep, scores, 0.0).astype(BF16)
            o_intra.append(jnp.dot(scores, v[h][rs], preferred_element_type=F32))
        o_intra = jnp.concatenate(o_intra, axis=0)

        state_t = s_ref[h]
        o_inter = []
        for n in range(nc):
            rows = slice(n * c, (n + 1) * c)
            o_inter.append(lax.dot_general(q_e[rows, ks[h]], state_t.astype(BF16), _NT,
                                           preferred_element_type=F32))
            kv_t = lax.dot_general(v[h][rows], k_d[rows, ks[h]], _TN, preferred_element_type=F32)
            state_t = state_t * decay[n][:, ks[h]] + kv_t
        s_ref[h] = state_t

        o = _rms(o_intra + jnp.concatenate(o_inter, axis=0), gn_ref[...])
        gate = g_ref[:, vs[h]].astype(F32)
        o_ref[:, vs[h]] = (o * (gate * jax.nn.sigmoid(gate))).astype(BF16)


def _gla(proj, wa, ba, gn, batch, seq, ts, to_cast):
    nt = seq // ts
    cast_in, cast_out, cast_shapes = _cast_specs(to_cast, batch * nt, lambda b, i: b * nt + i)
    kern = functools.partial(_gla_kernel, ts=ts)
    rows = lambda b, i: b * nt + i
    idx = jnp.arange(GLA_SUB)
    causal = (idx[:, None] >= idx[None, :]) & (idx[:, None] // GLA_CHUNK == idx[None, :] // GLA_CHUNK)
    return pl.pallas_call(
        kern,
        out_shape=[jax.ShapeDtypeStruct((batch * seq, GLA_V_W), BF16)] + cast_shapes,
        grid=(batch, nt),
        in_specs=[
            pl.BlockSpec((ts, GLA_QK_W), lambda b, i: (rows(b, i), COL_GQ // GLA_QK_W)),
            pl.BlockSpec((ts, GLA_QK_W), lambda b, i: (rows(b, i), COL_GK // GLA_QK_W)),
            pl.BlockSpec((ts, GLA_V_W), lambda b, i: (rows(b, i), COL_GV // GLA_V_W)),
            pl.BlockSpec((ts, GLA_V_W), lambda b, i: (rows(b, i), COL_GG // GLA_V_W)),
            pl.BlockSpec((ts, LANES), lambda b, i: (rows(b, i), COL_GA // LANES)),
            pl.BlockSpec((LANES, GLA_QK_W), lambda b, i: (0, 0)),
            pl.BlockSpec((1, GLA_QK_W), lambda b, i: (0, 0)),
            pl.BlockSpec((1, GLA_DV), lambda b, i: (0, 0)),
            pl.BlockSpec((GLA_SUB, GLA_SUB), lambda b, i: (0, 0)),
            pl.BlockSpec((GLA_SUB, GLA_SUB), lambda b, i: (0, 0)),
        ] + cast_in,
        out_specs=[pl.BlockSpec((ts, GLA_V_W), lambda b, i: (rows(b, i), 0))] + cast_out,
        scratch_shapes=[pltpu.VMEM((GLA_HEADS, GLA_DV, GLA_DK), F32)],
        compiler_params=pltpu.CompilerParams(
            dimension_semantics=("arbitrary", "arbitrary"),
            vmem_limit_bytes=VMEM_LIMIT),
        name="gla",
    )(proj, proj, proj, proj, proj, wa, ba, gn, causal.astype(BF16), causal.astype(F32), *to_cast)


def _swa_kernel(sinks_ref, q_ref, k_ref, v_ref, cos_ref, sin_ref, eye_ref, *rest):
    n_cast = (len(rest) - 3) // 2
    o_ref = rest[n_cast]
    k2p_ref, vtp_ref = rest[-2:]
    _cast_slabs(rest[:n_cast], rest[n_cast + 1:-2])
    step = pl.program_id(1)
    wb = WINDOW
    hd = SWA_HEAD_DIM
    half = hd // 2
    nblk = SWA_BLOCKS
    rows = nblk * wb
    pairs_per_kv = SWA_Q_W // LANES // SWA_KV_HEADS
    kvs = range(SWA_KV_HEADS)
    lane = lax.broadcasted_iota(jnp.int32, (rows, LANES), 1)
    head_a = (lane % hd) < half

    @pl.when(step == 0)
    def _():
        k2p_ref[...] = jnp.zeros_like(k2p_ref)
        vtp_ref[...] = jnp.zeros_like(vtp_ref)

    cos = cos_ref[...]
    sin = sin_ref[...]

    def rope(t, cs, sn):
        return t * cs + pltpu.roll(t, hd, 1) * sn

    k_r = rope(k_ref[...].astype(F32), cos, sin)
    k2_cur = [jnp.where(head_a, k_r, pltpu.roll(k_r, half, 1)).astype(BF16),
              jnp.where(head_a, pltpu.roll(k_r, LANES - half, 1), k_r).astype(BF16)]
    k2_all = [jnp.concatenate([k2p_ref[c], k2_cur[c]], axis=0) for c in kvs]
    v_t = v_ref[...].astype(F32).T
    ones_rows = (lax.broadcasted_iota(jnp.int32, (SWA_VT_ROWS - hd, rows), 0) == 0).astype(F32)
    vt_cur = [jnp.concatenate([v_t[c * hd:(c + 1) * hd], ones_rows], axis=0).astype(BF16) for c in kvs]
    vt_all = [jnp.concatenate([vtp_ref[c], vt_cur[c]], axis=1) for c in kvs]

    ki = lax.broadcasted_iota(jnp.int32, (2 * wb, wb), 0)
    qi = lax.broadcasted_iota(jnp.int32, (2 * wb, wb), 1) + wb
    band = (ki <= qi) & (qi - ki < WINDOW)
    bias = jnp.where(band, 0.0, NEG_INF).astype(BF16)
    bias_first = jnp.where(band & ((step > 0) | (ki >= wb)), 0.0, NEG_INF).astype(BF16)

    qscale = (hd ** -0.5) * LOG2E
    cos_q = cos * qscale
    sin_q = sin * qscale

    n_tiles = SWA_Q_W // LANES
    lhs = []
    for t in range(n_tiles):
        q_r = rope(q_ref[:, t * LANES:(t + 1) * LANES].astype(F32), cos_q, sin_q)
        lhs.append([jnp.where(head_a, q_r, 0.0).astype(BF16), jnp.where(head_a, 0.0, q_r).astype(BF16)])
    st = []
    for x in range(nblk):
        for c in kvs:
            rows_x = slice(x * wb, (x + 1) * wb)
            lhs_xc = jnp.concatenate([lhs[t][i][rows_x] for t in range(c * pairs_per_kv, (c + 1) * pairs_per_kv)
                                      for i in range(2)], axis=0)
            lhs_xc = jnp.concatenate([lhs_xc, eye_ref[...]], axis=1)
            keys = jnp.concatenate([k2_all[c][x * wb:(x + 2) * wb], bias_first if x == 0 else bias], axis=1)
            st.append(lax.dot_general(keys, lhs_xc, _NT, preferred_element_type=F32))
    st = jnp.concatenate(st, axis=1)
    m = jnp.max(st, axis=0, keepdims=True)
    e = jnp.exp2(st - m).astype(BF16)
    sink = jnp.concatenate([jnp.full((1, wb), sinks_ref[i], F32) for i in range(SWA_HEADS)] * nblk, axis=1) * LOG2E
    sink_term = jnp.exp2(sink - m)
    cols_per_kv = 2 * pairs_per_kv * wb
    for x in range(nblk):
        for c in kvs:
            cols = slice((x * SWA_KV_HEADS + c) * cols_per_kv, (x * SWA_KV_HEADS + c + 1) * cols_per_kv)
            ot = jnp.dot(vt_all[c][:, x * wb:(x + 2) * wb], e[:, cols], preferred_element_type=F32)
            o_n = ot[:hd] * (1.0 / (ot[hd:hd + 1] + sink_term[:, cols]))
            for n in range(pairs_per_kv):
                t = c * pairs_per_kv + n
                pair = jnp.concatenate([o_n[:, 2 * n * wb:(2 * n + 1) * wb], o_n[:, (2 * n + 1) * wb:(2 * n + 2) * wb]],
                                       axis=0)
                o_ref[x * wb:(x + 1) * wb, t * LANES:(t + 1) * LANES] = pair.T.astype(BF16)

    for c in kvs:
        k2p_ref[c] = k2_cur[c][rows - wb:]
        vtp_ref[c] = vt_cur[c][:, rows - wb:]


def _swa(proj, sinks, cos_tab, sin_tab, batch, seq, to_cast):
    rows = SWA_BLOCKS * WINDOW
    ns = seq // rows
    cast_in, cast_out, cast_shapes = _cast_specs(to_cast, batch * ns, lambda b, n: b * ns + n)
    eye = jnp.tile(jnp.eye(WINDOW, dtype=BF16), (SWA_HEADS // SWA_KV_HEADS, 1))
    tile = lambda b, n: b * ns + n
    return pl.pallas_call(
        _swa_kernel,
        out_shape=[jax.ShapeDtypeStruct((batch * seq, SWA_Q_W), BF16)] + cast_shapes,
        grid=(batch, ns),
        in_specs=[
            pl.BlockSpec(memory_space=pltpu.SMEM),
            pl.BlockSpec((rows, SWA_Q_W), lambda b, n: (tile(b, n), COL_SQ // SWA_Q_W)),
            pl.BlockSpec((rows, SWA_KV_W), lambda b, n: (tile(b, n), COL_SK // SWA_KV_W)),
            pl.BlockSpec((rows, SWA_KV_W), lambda b, n: (tile(b, n), COL_SV // SWA_KV_W)),
            pl.BlockSpec((rows, LANES), lambda b, n: (tile(b, n), 0)),
            pl.BlockSpec((rows, LANES), lambda b, n: (tile(b, n), 0)),
            pl.BlockSpec((SWA_HEADS // SWA_KV_HEADS * WINDOW, WINDOW), lambda b, n: (0, 0)),
        ] + cast_in,
        out_specs=[pl.BlockSpec((rows, SWA_Q_W), lambda b, n: (tile(b, n), 0))] + cast_out,
        scratch_shapes=[pltpu.VMEM((SWA_KV_HEADS, WINDOW, LANES), BF16),
                        pltpu.VMEM((SWA_KV_HEADS, SWA_VT_ROWS, WINDOW), BF16)],
        compiler_params=pltpu.CompilerParams(
            dimension_semantics=("arbitrary", "arbitrary"),
            vmem_limit_bytes=VMEM_LIMIT),
        name="swa",
    )(sinks, proj, proj, proj, cos_tab, sin_tab, eye, *to_cast)


def _outproj_kernel(x_ref, a_ref, b_ref, wa_ref, wb_ref, h_ref, *, tm, rc):
    for r in range(tm // rc):
        rs = pl.ds(r * rc, rc)
        acc = jnp.dot(a_ref[rs, :], wa_ref[...], preferred_element_type=F32)
        acc = acc + jnp.dot(b_ref[rs, :], wb_ref[...], preferred_element_type=F32)
        h_ref[rs, :] = x_ref[rs, :] + acc


def _outproj(x2, o_gla, o_swa, w, tm, rc):
    t = x2.shape[0]
    return pl.pallas_call(
        functools.partial(_outproj_kernel, tm=tm, rc=rc),
        out_shape=jax.ShapeDtypeStruct((t, D_MODEL), F32),
        grid=(t // tm,),
        in_specs=[
            pl.BlockSpec((tm, D_MODEL), lambda i: (i, 0)),
            pl.BlockSpec((tm, GLA_V_W), lambda i: (i, 0)),
            pl.BlockSpec((tm, SWA_Q_W), lambda i: (i, 0)),
            pl.BlockSpec((GLA_V_W, D_MODEL), lambda i: (0, 0), pipeline_mode=pl.Buffered(1)),
            pl.BlockSpec((SWA_Q_W, D_MODEL), lambda i: (GLA_V_W // SWA_Q_W, 0), pipeline_mode=pl.Buffered(1)),
        ],
        out_specs=pl.BlockSpec((tm, D_MODEL), lambda i: (i, 0)),
        compiler_params=pltpu.CompilerParams(
            dimension_semantics=("arbitrary",),
            vmem_limit_bytes=VMEM_LIMIT),
        name="outproj",
    )(x2, o_gla, o_swa, w, w)


def _ffn_kernel(h_ref, gf_ref, wg_ref, wu_ref, cw_ref, cb_ref, wd_ref, gl_ref, o_ref,
                hn_ref, carry_ref, *, tm, rc, seq):
    i = pl.program_id(0)
    j = pl.program_id(1)
    nj = pl.num_programs(1)

    def step(first, last):
        seq_start = (i * tm) % seq == 0
        prev = jnp.where(seq_start, 0.0, carry_ref[j])
        rows = lax.broadcasted_iota(jnp.int32, prev.shape, 0)
        cb = cb_ref[...]
        cw0, cw1, cw2 = cw_ref[0:1, :], cw_ref[1:2, :], cw_ref[2:3, :]

        for r in range(tm // rc):
            rs = pl.ds(r * rc, rc)
            if first:
                res = h_ref[rs, :]
                hn = _rms(res, gf_ref[...]).astype(BF16)
                hn_ref[rs, :] = hn
            else:
                res = o_ref[rs, :]
                hn = hn_ref[rs, :]
            gate = jnp.dot(hn, wg_ref[...], preferred_element_type=F32)
            up = jnp.dot(hn, wu_ref[...], preferred_element_type=F32)

            def shifted(d):
                rolled = pltpu.roll(gate, d, 0)
                top = jnp.where(rows < d, pltpu.roll(prev, d, 0), rolled[:SUBLANES])
                return jnp.concatenate([top, rolled[SUBLANES:]], axis=0)

            conv = cb + cw0 * shifted(2) + cw1 * shifted(1) + cw2 * gate
            act = (conv * jax.nn.sigmoid(conv) * up).astype(BF16)
            acc = res + jnp.dot(act, wd_ref[...], preferred_element_type=F32)
            o_ref[rs, :] = _rms(acc, gl_ref[...]) if last else acc
            prev = gate[rc - SUBLANES:, :]
        carry_ref[j] = prev

    pl.when(j == 0)(functools.partial(step, True, False))
    pl.when((j > 0) & (j < nj - 1))(functools.partial(step, False, False))
    pl.when(j == nj - 1)(functools.partial(step, False, True))


def _ffn(h, gf, wg, wu, cw, cb, wd, gl, seq, tm, tf, rc):
    t = h.shape[0]
    nj = D_FF // tf
    kern = functools.partial(_ffn_kernel, tm=tm, rc=rc, seq=seq)
    return pl.pallas_call(
        kern,
        out_shape=jax.ShapeDtypeStruct((t, D_MODEL), F32),
        grid=(t // tm, nj),
        in_specs=[
            pl.BlockSpec((tm, D_MODEL), lambda i, j: (i, 0)),
            pl.BlockSpec((1, D_MODEL), lambda i, j: (0, 0)),
            pl.BlockSpec((D_MODEL, tf), lambda i, j: (0, j)),
            pl.BlockSpec((D_MODEL, tf), lambda i, j: (0, j)),
            pl.BlockSpec((CONV_WIDTH, tf), lambda i, j: (0, j)),
            pl.BlockSpec((1, tf), lambda i, j: (0, j)),
            pl.BlockSpec((tf, D_MODEL), lambda i, j: (j, 0)),
            pl.BlockSpec((1, D_MODEL), lambda i, j: (0, 0)),
        ],
        out_specs=pl.BlockSpec((tm, D_MODEL), lambda i, j: (i, 0)),
        scratch_shapes=[pltpu.VMEM((tm, D_MODEL), BF16), pltpu.VMEM((nj, SUBLANES, tf), F32)],
        compiler_params=pltpu.CompilerParams(
            dimension_semantics=("arbitrary", "arbitrary"),
            vmem_limit_bytes=VMEM_LIMIT),
        name="convffn",
    )(h, gf, wg, wu, cw, cb, wd, gl)


def kernel(x, positions, attn_norm, w_in, w_a_up, b_a_up, gla_norm, sinks, w_out, ffn_norm,
           w_gate, w_up, conv_w, conv_b, w_down, final_norm):
    batch, seq, _ = x.shape
    t = batch * seq
    assert w_in.shape[0] == 1, "the final norm is fused into the single layer's FFN kernel"
    x2 = x.reshape(t, D_MODEL)
    pos3 = positions.reshape(t // WINDOW, 1, WINDOW)
    half = SWA_HEAD_DIM // 2
    freq_col = (ROPE_THETA ** (-jnp.arange(half, dtype=F32) / half)).reshape(half, 1)

    w_proj = _wprep(jnp.swapaxes(w_in[0], 0, 1), tk=256)
    wa = jnp.concatenate(
        [w_a_up[0], jnp.zeros((LANES - GLA_RANK, GLA_QK_W), F32)], axis=0).astype(BF16)

    proj, cos_tab, sin_tab, wg_bf, wu_bf = _inproj(x2, attn_norm[0].reshape(1, D_MODEL), w_proj, pos3, freq_col,
                                                   tm=512, rc=256, to_cast=[w_gate[0], w_up[0]])
    o_gla, wd_bf, wo_bf = _gla(proj, wa, b_a_up[0].reshape(1, GLA_QK_W), gla_norm[0].reshape(1, GLA_DV),
                               batch, seq, ts=512, to_cast=[w_down[0], w_out[0]])
    o_swa, = _swa(proj, sinks[0], cos_tab, sin_tab, batch, seq, to_cast=[])
    h = _outproj(x2, o_gla, o_swa, wo_bf, tm=1024, rc=512)
    y = _ffn(h, ffn_norm[0].reshape(1, D_MODEL), wg_bf, wu_bf, conv_w[0], conv_b[0].reshape(1, D_FF), wd_bf,
             final_norm.reshape(1, D_MODEL), seq, tm=1024, tf=512, rc=512)
    return y.reshape(batch, seq, D_MODEL)
```

```python
import functools

import jax
import jax.numpy as jnp
from jax import lax
from jax.experimental import pallas as pl
from jax.experimental.pallas import tpu as pltpu

D_MODEL = 2048
GLA_HEADS = 4
GLA_DK = 128
GLA_DV = 256
GLA_RANK = 16
GLA_GATE_NORM = 16.0
GLA_CHUNK = 64
GLA_SUB = 256
SWA_HEADS = 16
SWA_KV_HEADS = 2
SWA_HEAD_DIM = 64
WINDOW = 128
ROPE_THETA = 10000.0
D_FF = 5632
CONV_WIDTH = 3
EPS = 1e-6
NEG_INF = -1e30
LOG2E = 1.4426950408889634
SWA_VT_ROWS = SWA_HEAD_DIM + 16
SWA_BLOCKS = 4

GLA_QK_W = GLA_HEADS * GLA_DK
GLA_V_W = GLA_HEADS * GLA_DV
SWA_Q_W = SWA_HEADS * SWA_HEAD_DIM
SWA_KV_W = SWA_KV_HEADS * SWA_HEAD_DIM

LANES = 128
SUBLANES = 8

COL_GQ = 0
COL_GK = COL_GQ + GLA_QK_W
COL_GV = COL_GK + GLA_QK_W
COL_GG = COL_GV + GLA_V_W
COL_SQ = COL_GG + GLA_V_W
COL_SK = COL_SQ + SWA_Q_W
COL_SV = COL_SK + SWA_KV_W
COL_GA = COL_SV + SWA_KV_W
PROJ_TN = 1536
PROJ_W = 3 * PROJ_TN

VMEM_LIMIT = 58 * 1024 * 1024

F32 = jnp.float32
BF16 = jnp.bfloat16

_NT = (((1,), (1,)), ((), ()))
_TN = (((0,), (0,)), ((), ()))


def _rms(x, gain):
    return x * lax.rsqrt(jnp.mean(x * x, axis=-1, keepdims=True) + EPS) * gain


def _cast_specs(weights, steps, step_index):
    in_specs, out_specs, out_shapes = [], [], []
    for w in weights:
        rows, cols = w.shape
        slab = rows // steps
        assert slab * steps == rows and slab % (2 * SUBLANES) == 0, (w.shape, steps)
        spec = pl.BlockSpec((slab, cols), lambda *g: (step_index(*g), 0))
        in_specs.append(spec)
        out_specs.append(spec)
        out_shapes.append(jax.ShapeDtypeStruct(w.shape, BF16))
    return in_specs, out_specs, out_shapes


def _cast_slabs(src_refs, dst_refs):
    for src, dst in zip(src_refs, dst_refs):
        dst[...] = src[...].astype(BF16)


def _wprep_kernel(wt_ref, o_ref):
    tk = wt_ref.shape[1]
    half = SWA_HEAD_DIM // 2
    o_ga = 2 * GLA_QK_W + 2 * GLA_V_W
    o_sq = o_ga + GLA_RANK

    def put(col, rows):
        o_ref[:, col:col + rows.shape[0]] = rows.T.astype(BF16)

    put(0, wt_ref[:o_ga, :])
    for i in range((SWA_Q_W + SWA_KV_W) // LANES):
        r0 = o_sq + i * LANES
        blk = [wt_ref[r0 + j * half:r0 + (j + 1) * half, :] for j in range(4)]
        put(COL_SQ + i * LANES, jnp.concatenate([blk[0], blk[2], blk[1], blk[3]], axis=0))
    put(COL_SV, wt_ref[o_sq + SWA_Q_W + SWA_KV_W:, :])
    keep = lax.broadcasted_iota(jnp.int32, (LANES, tk), 0) < GLA_RANK
    put(COL_GA, jnp.where(keep, wt_ref[o_ga:o_ga + LANES, :], 0.0))
    o_ref[:, COL_GA + LANES:] = jnp.zeros((tk, PROJ_W - COL_GA - LANES), BF16)


def _wprep(wt, tk):
    n, k = wt.shape
    return pl.pallas_call(
        _wprep_kernel,
        out_shape=jax.ShapeDtypeStruct((k, PROJ_W), BF16),
        grid=(k // tk,),
        in_specs=[pl.BlockSpec((n, tk), lambda i: (0, i))],
        out_specs=pl.BlockSpec((tk, PROJ_W), lambda i: (i, 0)),
        compiler_params=pltpu.CompilerParams(
            dimension_semantics=("arbitrary",),
            vmem_limit_bytes=VMEM_LIMIT),
        name="wprep",
    )(wt)


def _inproj_kernel(x_ref, g_ref, w_ref, pos_ref, freq_ref, *rest, tm, rc):
    n_cast = (len(rest) - 3) // 2
    o_ref, cos_ref, sin_ref = rest[n_cast:n_cast + 3]
    _cast_slabs(rest[:n_cast], rest[n_cast + 3:])

    half = SWA_HEAD_DIM // 2
    low = lax.broadcasted_iota(jnp.int32, (WINDOW, LANES), 1) < SWA_HEAD_DIM
    for b in range(tm // WINDOW):
        ang_t = freq_ref[...] * pos_ref[b].astype(F32)
        cs = jnp.concatenate([jnp.cos(ang_t)] * (LANES // half), axis=0).T
        sn = jnp.concatenate([jnp.sin(ang_t)] * (LANES // half), axis=0).T
        cos_ref[b * WINDOW:(b + 1) * WINDOW, :] = cs
        sin_ref[b * WINDOW:(b + 1) * WINDOW, :] = jnp.where(low, -sn, sn)

    for r in range(tm // rc):
        rs = pl.ds(r * rc, rc)
        u = _rms(x_ref[rs, :], g_ref[...]).astype(BF16)
        for c in range(PROJ_W // PROJ_TN):
            cols = pl.ds(c * PROJ_TN, PROJ_TN)
            o_ref[rs, cols] = jnp.dot(u, w_ref[:, cols], preferred_element_type=F32).astype(BF16)


def _inproj(x2, gain, w, pos3, freq_col, tm, rc, to_cast):
    t = x2.shape[0]
    cast_in, cast_out, cast_shapes = _cast_specs(to_cast, t // tm, lambda i: i)
    return pl.pallas_call(
        functools.partial(_inproj_kernel, tm=tm, rc=rc),
        out_shape=[jax.ShapeDtypeStruct((t, PROJ_W), BF16), jax.ShapeDtypeStruct((t, LANES), F32),
                   jax.ShapeDtypeStruct((t, LANES), F32)] + cast_shapes,
        grid=(t // tm,),
        in_specs=[
            pl.BlockSpec((tm, D_MODEL), lambda i: (i, 0)),
            pl.BlockSpec((1, D_MODEL), lambda i: (0, 0)),
            pl.BlockSpec((D_MODEL, PROJ_W), lambda i: (0, 0), pipeline_mode=pl.Buffered(1)),
            pl.BlockSpec((tm // WINDOW, 1, WINDOW), lambda i: (i, 0, 0)),
            pl.BlockSpec((SWA_HEAD_DIM // 2, 1), lambda i: (0, 0)),
        ] + cast_in,
        out_specs=[pl.BlockSpec((tm, PROJ_W), lambda i: (i, 0)), pl.BlockSpec((tm, LANES), lambda i: (i, 0)),
                   pl.BlockSpec((tm, LANES), lambda i: (i, 0))] + cast_out,
        compiler_params=pltpu.CompilerParams(
            dimension_semantics=("arbitrary",),
            vmem_limit_bytes=VMEM_LIMIT),
        name="inproj",
    )(x2, gain, w, pos3, freq_col, *to_cast)


def _gla_kernel(q_ref, k_ref, v_ref, g_ref, a_ref, wa_ref, ba_ref, gn_ref, tril_ref, mask_ref, *rest, ts):
    c = GLA_CHUNK
    nc = ts // c
    n_cast = (len(rest) - 2) // 2
    o_ref, s_ref = rest[n_cast], rest[-1]
    _cast_slabs(rest[:n_cast], rest[n_cast + 1:-1])
    heads = range(GLA_HEADS)
    ks = [slice(h * GLA_DK, (h + 1) * GLA_DK) for h in heads]
    vs = [slice(h * GLA_DV, (h + 1) * GLA_DV) for h in heads]

    @pl.when(pl.program_id(1) == 0)
    def _():
        s_ref[...] = jnp.zeros_like(s_ref)

    z = jnp.dot(a_ref[...], wa_ref[...], preferred_element_type=F32) + ba_ref[...]
    log_a = (jnp.minimum(z, 0.0) - jnp.log(1.0 + jnp.exp(-jnp.abs(z)))) * (1.0 / GLA_GATE_NORM)

    la_hi = log_a.astype(BF16)
    la_lo = (log_a - la_hi.astype(F32)).astype(BF16)
    la_hl = jnp.concatenate([la_hi, la_lo], axis=1)
    subs = [slice(s * GLA_SUB, (s + 1) * GLA_SUB) for s in range(ts // GLA_SUB)]
    cum2 = jnp.concatenate([jnp.dot(tril_ref[...], la_hl[rs], preferred_element_type=F32) for rs in subs], axis=0)
    bcum = cum2[:, :GLA_QK_W] + cum2[:, GLA_QK_W:]
    bcum3 = bcum.reshape(nc, c, GLA_QK_W)
    b_last = bcum3[:, c - 1:c, :]

    q = q_ref[...].astype(F32) * (GLA_DK ** -0.5)
    k = k_ref[...].astype(F32)
    q_e = (q * jnp.exp(bcum)).astype(BF16)
    k_e = (k * jnp.exp(-bcum)).astype(BF16)
    k_d = (k * jnp.exp(b_last - bcum3).reshape(ts, GLA_QK_W)).astype(BF16)
    decay = jnp.exp(b_last)
    v = [v_ref[:, vs[h]] for h in heads]

    keep = mask_ref[...] != 0.0

    for h in heads:
        o_intra = []
        for rs in subs:
            scores = lax.dot_general(q_e[rs, ks[h]], k_e[rs, ks[h]], _NT, preferred_element_type=F32)
            scores = jnp.where(keep, scores, 0.0).astype(BF16)
            o_intra.append(jnp.dot(scores, v[h][rs], preferred_element_type=F32))
        o_intra = jnp.concatenate(o_intra, axis=0)

        state_t = s_ref[h]
        o_inter = []
        for n in range(nc):
            rows = slice(n * c, (n + 1) * c)
            o_inter.append(lax.dot_general(q_e[rows, ks[h]], state_t.astype(BF16), _NT,
                                           preferred_element_type=F32))
            kv_t = lax.dot_general(v[h][rows], k_d[rows, ks[h]], _TN, preferred_element_type=F32)
            state_t = state_t * decay[n][:, ks[h]] + kv_t
        s_ref[h] = state_t

        o = _rms(o_intra + jnp.concatenate(o_inter, axis=0), gn_ref[...])
        gate = g_ref[:, vs[h]].astype(F32)
        o_ref[:, vs[h]] = (o * (gate * jax.nn.sigmoid(gate))).astype(BF16)


def _gla(proj, wa, ba, gn, batch, seq, ts, to_cast):
    nt = seq // ts
    cast_in, cast_out, cast_shapes = _cast_specs(to_cast, batch * nt, lambda b, i: b * nt + i)
    kern = functools.partial(_gla_kernel, ts=ts)
    rows = lambda b, i: b * nt + i
    idx = jnp.arange(GLA_SUB)
    causal = (idx[:, None] >= idx[None, :]) & (idx[:, None] // GLA_CHUNK == idx[None, :] // GLA_CHUNK)
    return pl.pallas_call(
        kern,
        out_shape=[jax.ShapeDtypeStruct((batch * seq, GLA_V_W), BF16)] + cast_shapes,
        grid=(batch, nt),
        in_specs=[
            pl.BlockSpec((ts, GLA_QK_W), lambda b, i: (rows(b, i), COL_GQ // GLA_QK_W)),
            pl.BlockSpec((ts, GLA_QK_W), lambda b, i: (rows(b, i), COL_GK // GLA_QK_W)),
            pl.BlockSpec((ts, GLA_V_W), lambda b, i: (rows(b, i), COL_GV // GLA_V_W)),
            pl.BlockSpec((ts, GLA_V_W), lambda b, i: (rows(b, i), COL_GG // GLA_V_W)),
            pl.BlockSpec((ts, LANES), lambda b, i: (rows(b, i), COL_GA // LANES)),
            pl.BlockSpec((LANES, GLA_QK_W), lambda b, i: (0, 0)),
            pl.BlockSpec((1, GLA_QK_W), lambda b, i: (0, 0)),
            pl.BlockSpec((1, GLA_DV), lambda b, i: (0, 0)),
            pl.BlockSpec((GLA_SUB, GLA_SUB), lambda b, i: (0, 0)),
            pl.BlockSpec((GLA_SUB, GLA_SUB), lambda b, i: (0, 0)),
        ] + cast_in,
        out_specs=[pl.BlockSpec((ts, GLA_V_W), lambda b, i: (rows(b, i), 0))] + cast_out,
        scratch_shapes=[pltpu.VMEM((GLA_HEADS, GLA_DV, GLA_DK), F32)],
        compiler_params=pltpu.CompilerParams(
            dimension_semantics=("arbitrary", "arbitrary"),
            vmem_limit_bytes=VMEM_LIMIT),
        name="gla",
    )(proj, proj, proj, proj, proj, wa, ba, gn, causal.astype(BF16), causal.astype(F32), *to_cast)


def _swa_kernel(sinks_ref, q_ref, k_ref, v_ref, cos_ref, sin_ref, eye_ref, *rest):
    n_cast = (len(rest) - 3) // 2
    o_ref = rest[n_cast]
    k2p_ref, vtp_ref = rest[-2:]
    _cast_slabs(rest[:n_cast], rest[n_cast + 1:-2])
    step = pl.program_id(1)
    wb = WINDOW
    hd = SWA_HEAD_DIM
    half = hd // 2
    nblk = SWA_BLOCKS
    rows = nblk * wb
    pairs_per_kv = SWA_Q_W // LANES // SWA_KV_HEADS
    kvs = range(SWA_KV_HEADS)
    lane = lax.broadcasted_iota(jnp.int32, (rows, LANES), 1)
    head_a = (lane % hd) < half

    @pl.when(step == 0)
    def _():
        k2p_ref[...] = jnp.zeros_like(k2p_ref)
        vtp_ref[...] = jnp.zeros_like(vtp_ref)

    cos = cos_ref[...]
    sin = sin_ref[...]

    def rope(t, cs, sn):
        return t * cs + pltpu.roll(t, hd, 1) * sn

    k_r = rope(k_ref[...].astype(F32), cos, sin)
    k2_cur = [jnp.where(head_a, k_r, pltpu.roll(k_r, half, 1)).astype(BF16),
              jnp.where(head_a, pltpu.roll(k_r, LANES - half, 1), k_r).astype(BF16)]
    k2_all = [jnp.concatenate([k2p_ref[c], k2_cur[c]], axis=0) for c in kvs]
    v_t = v_ref[...].astype(F32).T
    ones_rows = (lax.broadcasted_iota(jnp.int32, (SWA_VT_ROWS - hd, rows), 0) == 0).astype(F32)
    vt_cur = [jnp.concatenate([v_t[c * hd:(c + 1) * hd], ones_rows], axis=0).astype(BF16) for c in kvs]
    vt_all = [jnp.concatenate([vtp_ref[c], vt_cur[c]], axis=1) for c in kvs]

    ki = lax.broadcasted_iota(jnp.int32, (2 * wb, wb), 0)
    qi = lax.broadcasted_iota(jnp.int32, (2 * wb, wb), 1) + wb
    band = (ki <= qi) & (qi - ki < WINDOW)
    bias = jnp.where(band, 0.0, NEG_INF).astype(BF16)
    bias_first = jnp.where(band & ((step > 0) | (ki >= wb)), 0.0, NEG_INF).astype(BF16)

    qscale = (hd ** -0.5) * LOG2E
    cos_q = cos * qscale
    sin_q = sin * qscale

    n_tiles = SWA_Q_W // LANES
    lhs = []
    for t in range(n_tiles):
        q_r = rope(q_ref[:, t * LANES:(t + 1) * LANES].astype(F32), cos_q, sin_q)
        lhs.append([jnp.where(head_a, q_r, 0.0).astype(BF16), jnp.where(head_a, 0.0, q_r).astype(BF16)])
    st = []
    for x in range(nblk):
        for c in kvs:
            rows_x = slice(x * wb, (x + 1) * wb)
            lhs_xc = jnp.concatenate([lhs[t][i][rows_x] for t in range(c * pairs_per_kv, (c + 1) * pairs_per_kv)
                                      for i in range(2)], axis=0)
            lhs_xc = jnp.concatenate([lhs_xc, eye_ref[...]], axis=1)
            keys = jnp.concatenate([k2_all[c][x * wb:(x + 2) * wb], bias_first if x == 0 else bias], axis=1)
            st.append(lax.dot_general(keys, lhs_xc, _NT, preferred_element_type=F32))
    st = jnp.concatenate(st, axis=1)
    m = jnp.max(st, axis=0, keepdims=True)
    e = jnp.exp2(st - m).astype(BF16)
    sink = jnp.concatenate([jnp.full((1, wb), sinks_ref[i], F32) for i in range(SWA_HEADS)] * nblk, axis=1) * LOG2E
    sink_term = jnp.exp2(sink - m)
    cols_per_kv = 2 * pairs_per_kv * wb
    for x in range(nblk):
        for c in kvs:
            cols = slice((x * SWA_KV_HEADS + c) * cols_per_kv, (x * SWA_KV_HEADS + c + 1) * cols_per_kv)
            ot = jnp.dot(vt_all[c][:, x * wb:(x + 2) * wb], e[:, cols], preferred_element_type=F32)
            o_n = ot[:hd] * (1.0 / (ot[hd:hd + 1] + sink_term[:, cols]))
            for n in range(pairs_per_kv):
                t = c * pairs_per_kv + n
                pair = jnp.concatenate([o_n[:, 2 * n * wb:(2 * n + 1) * wb], o_n[:, (2 * n + 1) * wb:(2 * n + 2) * wb]],
                                       axis=0)
                o_ref[x * wb:(x + 1) * wb, t * LANES:(t + 1) * LANES] = pair.T.astype(BF16)

    for c in kvs:
        k2p_ref[c] = k2_cur[c][rows - wb:]
        vtp_ref[c] = vt_cur[c][:, rows - wb:]


def _swa(proj, sinks, cos_tab, sin_tab, batch, seq, to_cast):
    rows = SWA_BLOCKS * WINDOW
    ns = seq // rows
    cast_in, cast_out, cast_shapes = _cast_specs(to_cast, batch * ns, lambda b, n: b * ns + n)
    eye = jnp.tile(jnp.eye(WINDOW, dtype=BF16), (SWA_HEADS // SWA_KV_HEADS, 1))
    tile = lambda b, n: b * ns + n
    return pl.pallas_call(
        _swa_kernel,
        out_shape=[jax.ShapeDtypeStruct((batch * seq, SWA_Q_W), BF16)] + cast_shapes,
        grid=(batch, ns),
        in_specs=[
            pl.BlockSpec(memory_space=pltpu.SMEM),
            pl.BlockSpec((rows, SWA_Q_W), lambda b, n: (tile(b, n), COL_SQ // SWA_Q_W)),
            pl.BlockSpec((rows, SWA_KV_W), lambda b, n: (tile(b, n), COL_SK // SWA_KV_W)),
            pl.BlockSpec((rows, SWA_KV_W), lambda b, n: (tile(b, n), COL_SV // SWA_KV_W)),
            pl.BlockSpec((rows, LANES), lambda b, n: (tile(b, n), 0)),
            pl.BlockSpec((rows, LANES), lambda b, n: (tile(b, n), 0)),
            pl.BlockSpec((SWA_HEADS // SWA_KV_HEADS * WINDOW, WINDOW), lambda b, n: (0, 0)),
        ] + cast_in,
        out_specs=[pl.BlockSpec((rows, SWA_Q_W), lambda b, n: (tile(b, n), 0))] + cast_out,
        scratch_shapes=[pltpu.VMEM((SWA_KV_HEADS, WINDOW, LANES), BF16),
                        pltpu.VMEM((SWA_KV_HEADS, SWA_VT_ROWS, WINDOW), BF16)],
        compiler_params=pltpu.CompilerParams(
            dimension_semantics=("arbitrary", "arbitrary"),
            vmem_limit_bytes=VMEM_LIMIT),
        name="swa",
    )(sinks, proj, proj, proj, cos_tab, sin_tab, eye, *to_cast)


def _outproj_kernel(x_ref, a_ref, b_ref, wa_ref, wb_ref, h_ref, *, tm, rc):
    for r in range(tm // rc):
        rs = pl.ds(r * rc, rc)
        acc = jnp.dot(a_ref[rs, :], wa_ref[...], preferred_element_type=F32)
        acc = acc + jnp.dot(b_ref[rs, :], wb_ref[...], preferred_element_type=F32)
        h_ref[rs, :] = x_ref[rs, :] + acc


def _outproj(x2, o_gla, o_swa, w, tm, rc):
    t = x2.shape[0]
    return pl.pallas_call(
        functools.partial(_outproj_kernel, tm=tm, rc=rc),
        out_shape=jax.ShapeDtypeStruct((t, D_MODEL), F32),
        grid=(t // tm,),
        in_specs=[
            pl.BlockSpec((tm, D_MODEL), lambda i: (i, 0)),
            pl.BlockSpec((tm, GLA_V_W), lambda i: (i, 0)),
            pl.BlockSpec((tm, SWA_Q_W), lambda i: (i, 0)),
            pl.BlockSpec((GLA_V_W, D_MODEL), lambda i: (0, 0), pipeline_mode=pl.Buffered(1)),
            pl.BlockSpec((SWA_Q_W, D_MODEL), lambda i: (GLA_V_W // SWA_Q_W, 0), pipeline_mode=pl.Buffered(1)),
        ],
        out_specs=pl.BlockSpec((tm, D_MODEL), lambda i: (i, 0)),
        compiler_params=pltpu.CompilerParams(
            dimension_semantics=("arbitrary",),
            vmem_limit_bytes=VMEM_LIMIT),
        name="outproj",
    )(x2, o_gla, o_swa, w, w)


def _ffn_kernel(h_ref, gf_ref, wg_ref, wu_ref, cw_ref, cb_ref, wd_ref, gl_ref, o_ref,
                hn_ref, carry_ref, *, tm, rc, seq):
    i = pl.program_id(0)
    j = pl.program_id(1)
    nj = pl.num_programs(1)

    def step(first, last):
        seq_start = (i * tm) % seq == 0
        prev = jnp.where(seq_start, 0.0, carry_ref[j])
        rows = lax.broadcasted_iota(jnp.int32, prev.shape, 0)
        cb = cb_ref[...]
        cw0, cw1, cw2 = cw_ref[0:1, :], cw_ref[1:2, :], cw_ref[2:3, :]

        for r in range(tm // rc):
            rs = pl.ds(r * rc, rc)
            if first:
                res = h_ref[rs, :]
                hn = _rms(res, gf_ref[...]).astype(BF16)
                hn_ref[rs, :] = hn
            else:
                res = o_ref[rs, :]
                hn = hn_ref[rs, :]
            gate = jnp.dot(hn, wg_ref[...], preferred_element_type=F32)
            up = jnp.dot(hn, wu_ref[...], preferred_element_type=F32)

            def shifted(d):
                rolled = pltpu.roll(gate, d, 0)
                top = jnp.where(rows < d, pltpu.roll(prev, d, 0), rolled[:SUBLANES])
                return jnp.concatenate([top, rolled[SUBLANES:]], axis=0)

            conv = cb + cw0 * shifted(2) + cw1 * shifted(1) + cw2 * gate
            act = (conv * jax.nn.sigmoid(conv) * up).astype(BF16)
            acc = res + jnp.dot(act, wd_ref[...], preferred_element_type=F32)
            o_ref[rs, :] = _rms(acc, gl_ref[...]) if last else acc
            prev = gate[rc - SUBLANES:, :]
        carry_ref[j] = prev

    pl.when(j == 0)(functools.partial(step, True, False))
    pl.when((j > 0) & (j < nj - 1))(functools.partial(step, False, False))
    pl.when(j == nj - 1)(functools.partial(step, False, True))


def _ffn(h, gf, wg, wu, cw, cb, wd, gl, seq, tm, tf, rc):
    t = h.shape[0]
    nj = D_FF // tf
    kern = functools.partial(_ffn_kernel, tm=tm, rc=rc, seq=seq)
    return pl.pallas_call(
        kern,
        out_shape=jax.ShapeDtypeStruct((t, D_MODEL), F32),
        grid=(t // tm, nj),
        in_specs=[
            pl.BlockSpec((tm, D_MODEL), lambda i, j: (i, 0)),
            pl.BlockSpec((1, D_MODEL), lambda i, j: (0, 0)),
            pl.BlockSpec((D_MODEL, tf), lambda i, j: (0, j)),
            pl.BlockSpec((D_MODEL, tf), lambda i, j: (0, j)),
            pl.BlockSpec((CONV_WIDTH, tf), lambda i, j: (0, j)),
            pl.BlockSpec((1, tf), lambda i, j: (0, j)),
            pl.BlockSpec((tf, D_MODEL), lambda i, j: (j, 0)),
            pl.BlockSpec((1, D_MODEL), lambda i, j: (0, 0)),
        ],
        out_specs=pl.BlockSpec((tm, D_MODEL), lambda i, j: (i, 0)),
        scratch_shapes=[pltpu.VMEM((tm, D_MODEL), BF16), pltpu.VMEM((nj, SUBLANES, tf), F32)],
        compiler_params=pltpu.CompilerParams(
            dimension_semantics=("arbitrary", "arbitrary"),
            vmem_limit_bytes=VMEM_LIMIT),
        name="convffn",
    )(h, gf, wg, wu, cw, cb, wd, gl)


def kernel(x, positions, attn_norm, w_in, w_a_up, b_a_up, gla_norm, sinks, w_out, ffn_norm,
           w_gate, w_up, conv_w, conv_b, w_down, final_norm):
    batch, seq, _ = x.shape
    t = batch * seq
    assert w_in.shape[0] == 1, "the final norm is fused into the single layer's FFN kernel"
    x2 = x.reshape(t, D_MODEL)
    pos3 = positions.reshape(t // WINDOW, 1, WINDOW)
    half = SWA_HEAD_DIM // 2
    freq_col = (ROPE_THETA ** (-jnp.arange(half, dtype=F32) / half)).reshape(half, 1)

    w_proj = _wprep(jnp.swapaxes(w_in[0], 0, 1), tk=256)
    wa = jnp.concatenate(
        [w_a_up[0], jnp.zeros((LANES - GLA_RANK, GLA_QK_W), F32)], axis=0).astype(BF16)

    proj, cos_tab, sin_tab, wg_bf, wu_bf = _inproj(x2, attn_norm[0].reshape(1, D_MODEL), w_proj, pos3, freq_col,
                                                   tm=512, rc=128, to_cast=[w_gate[0], w_up[0]])
    o_gla, wd_bf, wo_bf = _gla(proj, wa, b_a_up[0].reshape(1, GLA_QK_W), gla_norm[0].reshape(1, GLA_DV),
                               batch, seq, ts=1024, to_cast=[w_down[0], w_out[0]])
    o_swa, = _swa(proj, sinks[0], cos_tab, sin_tab, batch, seq, to_cast=[])
    h = _outproj(x2, o_gla, o_swa, wo_bf, tm=1024, rc=512)
    y = _ffn(h, ffn_norm[0].reshape(1, D_MODEL), wg_bf, wu_bf, conv_w[0], conv_b[0].reshape(1, D_FF), wd_bf,
             final_norm.reshape(1, D_MODEL), seq, tm=1024, tf=512, rc=512)
    return y.reshape(batch, seq, D_MODEL)
```

```python
import functools

import jax
import jax.numpy as jnp
from jax import lax
from jax.experimental import pallas as pl
from jax.experimental.pallas import tpu as pltpu

D_MODEL = 2048
GLA_HEADS = 4
GLA_DK = 128
GLA_DV = 256
GLA_RANK = 16
GLA_GATE_NORM = 16.0
GLA_CHUNK = 64
GLA_SUB = 256
SWA_HEADS = 16
SWA_KV_HEADS = 2
SWA_HEAD_DIM = 64
WINDOW = 128
ROPE_THETA = 10000.0
D_FF = 5632
CONV_WIDTH = 3
EPS = 1e-6
NEG_INF = -1e30
LOG2E = 1.4426950408889634
SWA_VT_ROWS = SWA_HEAD_DIM + 16
SWA_BLOCKS = 4

GLA_QK_W = GLA_HEADS * GLA_DK
GLA_V_W = GLA_HEADS * GLA_DV
SWA_Q_W = SWA_HEADS * SWA_HEAD_DIM
SWA_KV_W = SWA_KV_HEADS * SWA_HEAD_DIM

LANES = 128
SUBLANES = 8

COL_GQ = 0
COL_GK = COL_GQ + GLA_QK_W
COL_GV = COL_GK + GLA_QK_W
COL_GG = COL_GV + GLA_V_W
COL_SQ = COL_GG + GLA_V_W
COL_SK = COL_SQ + SWA_Q_W
COL_SV = COL_SK + SWA_KV_W
COL_GA = COL_SV + SWA_KV_W
PROJ_TN = 1536
PROJ_W = 3 * PROJ_TN

VMEM_LIMIT = 58 * 1024 * 1024

F32 = jnp.float32
BF16 = jnp.bfloat16

_NT = (((1,), (1,)), ((), ()))
_TN = (((0,), (0,)), ((), ()))


def _rms(x, gain):
    return x * lax.rsqrt(jnp.mean(x * x, axis=-1, keepdims=True) + EPS) * gain


def _cast_specs(weights, steps, step_index):
    in_specs, out_specs, out_shapes = [], [], []
    for w in weights:
        rows, cols = w.shape
        slab = rows // steps
        assert slab * steps == rows and slab % (2 * SUBLANES) == 0, (w.shape, steps)
        spec = pl.BlockSpec((slab, cols), lambda *g: (step_index(*g), 0))
        in_specs.append(spec)
        out_specs.append(spec)
        out_shapes.append(jax.ShapeDtypeStruct(w.shape, BF16))
    return in_specs, out_specs, out_shapes


def _cast_slabs(src_refs, dst_refs):
    for src, dst in zip(src_refs, dst_refs):
        dst[...] = src[...].astype(BF16)


def _wprep_kernel(wt_ref, o_ref):
    tk = wt_ref.shape[1]
    half = SWA_HEAD_DIM // 2
    o_ga = 2 * GLA_QK_W + 2 * GLA_V_W
    o_sq = o_ga + GLA_RANK

    def put(col, rows):
        o_ref[:, col:col + rows.shape[0]] = rows.T.astype(BF16)

    put(0, wt_ref[:o_ga, :])
    for i in range((SWA_Q_W + SWA_KV_W) // LANES):
        r0 = o_sq + i * LANES
        blk = [wt_ref[r0 + j * half:r0 + (j + 1) * half, :] for j in range(4)]
        put(COL_SQ + i * LANES, jnp.concatenate([blk[0], blk[2], blk[1], blk[3]], axis=0))
    put(COL_SV, wt_ref[o_sq + SWA_Q_W + SWA_KV_W:, :])
    keep = lax.broadcasted_iota(jnp.int32, (LANES, tk), 0) < GLA_RANK
    put(COL_GA, jnp.where(keep, wt_ref[o_ga:o_ga + LANES, :], 0.0))
    o_ref[:, COL_GA + LANES:] = jnp.zeros((tk, PROJ_W - COL_GA - LANES), BF16)


def _wprep(wt, tk):
    n, k = wt.shape
    return pl.pallas_call(
        _wprep_kernel,
        out_shape=jax.ShapeDtypeStruct((k, PROJ_W), BF16),
        grid=(k // tk,),
        in_specs=[pl.BlockSpec((n, tk), lambda i: (0, i))],
        out_specs=pl.BlockSpec((tk, PROJ_W), lambda i: (i, 0)),
        compiler_params=pltpu.CompilerParams(
            dimension_semantics=("arbitrary",),
            vmem_limit_bytes=VMEM_LIMIT),
        name="wprep",
    )(wt)


def _inproj_kernel(x_ref, g_ref, w_ref, pos_ref, freq_ref, *rest, tm, rc):
    n_cast = (len(rest) - 3) // 2
    o_ref, cos_ref, sin_ref = rest[n_cast:n_cast + 3]
    _cast_slabs(rest[:n_cast], rest[n_cast + 3:])

    half = SWA_HEAD_DIM // 2
    low = lax.broadcasted_iota(jnp.int32, (WINDOW, LANES), 1) < SWA_HEAD_DIM
    for b in range(tm // WINDOW):
        ang_t = freq_ref[...] * pos_ref[b].astype(F32)
        cs = jnp.concatenate([jnp.cos(ang_t)] * (LANES // half), axis=0).T
        sn = jnp.concatenate([jnp.sin(ang_t)] * (LANES // half), axis=0).T
        cos_ref[b * WINDOW:(b + 1) * WINDOW, :] = cs
        sin_ref[b * WINDOW:(b + 1) * WINDOW, :] = jnp.where(low, -sn, sn)

    for r in range(tm // rc):
        rs = pl.ds(r * rc, rc)
        u = _rms(x_ref[rs, :], g_ref[...]).astype(BF16)
        o_ref[rs, :] = jnp.dot(u, w_ref[...], preferred_element_type=F32).astype(BF16)


def _inproj(x2, gain, w, pos3, freq_col, tm, rc, to_cast):
    t = x2.shape[0]
    cast_in, cast_out, cast_shapes = _cast_specs(to_cast, t // tm, lambda i: i)
    return pl.pallas_call(
        functools.partial(_inproj_kernel, tm=tm, rc=rc),
        out_shape=[jax.ShapeDtypeStruct((t, PROJ_W), BF16), jax.ShapeDtypeStruct((t, LANES), F32),
                   jax.ShapeDtypeStruct((t, LANES), F32)] + cast_shapes,
        grid=(t // tm,),
        in_specs=[
            pl.BlockSpec((tm, D_MODEL), lambda i: (i, 0)),
            pl.BlockSpec((1, D_MODEL), lambda i: (0, 0)),
            pl.BlockSpec((D_MODEL, PROJ_W), lambda i: (0, 0), pipeline_mode=pl.Buffered(1)),
            pl.BlockSpec((tm // WINDOW, 1, WINDOW), lambda i: (i, 0, 0)),
            pl.BlockSpec((SWA_HEAD_DIM // 2, 1), lambda i: (0, 0)),
        ] + cast_in,
        out_specs=[pl.BlockSpec((tm, PROJ_W), lambda i: (i, 0)), pl.BlockSpec((tm, LANES), lambda i: (i, 0)),
                   pl.BlockSpec((tm, LANES), lambda i: (i, 0))] + cast_out,
        compiler_params=pltpu.CompilerParams(
            dimension_semantics=("arbitrary",),
            vmem_limit_bytes=VMEM_LIMIT),
        name="inproj",
    )(x2, gain, w, pos3, freq_col, *to_cast)


def _gla_kernel(q_ref, k_ref, v_ref, g_ref, a_ref, wa_ref, ba_ref, gn_ref, tril_ref, mask_ref, *rest, ts):
    c = GLA_CHUNK
    nc = ts // c
    n_cast = (len(rest) - 2) // 2
    o_ref, s_ref = rest[n_cast], rest[-1]
    _cast_slabs(rest[:n_cast], rest[n_cast + 1:-1])
    heads = range(GLA_HEADS)
    ks = [slice(h * GLA_DK, (h + 1) * GLA_DK) for h in heads]
    vs = [slice(h * GLA_DV, (h + 1) * GLA_DV) for h in heads]

    @pl.when(pl.program_id(1) == 0)
    def _():
        s_ref[...] = jnp.zeros_like(s_ref)

    z = jnp.dot(a_ref[...], wa_ref[...], preferred_element_type=F32) + ba_ref[...]
    log_a = (jnp.minimum(z, 0.0) - jnp.log(1.0 + jnp.exp(-jnp.abs(z)))) * (1.0 / GLA_GATE_NORM)

    la_hi = log_a.astype(BF16)
    la_lo = (log_a - la_hi.astype(F32)).astype(BF16)
    la_hl = jnp.concatenate([la_hi, la_lo], axis=1)
    subs = [slice(s * GLA_SUB, (s + 1) * GLA_SUB) for s in range(ts // GLA_SUB)]
    cum2 = jnp.concatenate([jnp.dot(tril_ref[...], la_hl[rs], preferred_element_type=F32) for rs in subs], axis=0)
    bcum = cum2[:, :GLA_QK_W] + cum2[:, GLA_QK_W:]
    bcum3 = bcum.reshape(nc, c, GLA_QK_W)
    b_last = bcum3[:, c - 1:c, :]

    q = q_ref[...].astype(F32) * (GLA_DK ** -0.5)
    k = k_ref[...].astype(F32)
    q_e = (q * jnp.exp(bcum)).astype(BF16)
    k_e = (k * jnp.exp(-bcum)).astype(BF16)
    k_d = (k * jnp.exp(b_last - bcum3).reshape(ts, GLA_QK_W)).astype(BF16)
    decay = jnp.exp(b_last)
    v = [v_ref[:, vs[h]] for h in heads]

    keep = mask_ref[...] != 0.0

    for h in heads:
        o_intra = []
        for rs in subs:
            scores = lax.dot_general(q_e[rs, ks[h]], k_e[rs, ks[h]], _NT, preferred_element_type=F32)
            scores = jnp.where(keep, scores, 0.0).astype(BF16)
            o_intra.append(jnp.dot(scores, v[h][rs], preferred_element_type=F32))
        o_intra = jnp.concatenate(o_intra, axis=0)

        state_t = s_ref[h]
        o_inter = []
        for n in range(nc):
            rows = slice(n * c, (n + 1) * c)
            o_inter.append(lax.dot_general(q_e[rows, ks[h]], state_t.astype(BF16), _NT,
                                           preferred_element_type=F32))
            kv_t = lax.dot_general(v[h][rows], k_d[rows, ks[h]], _TN, preferred_element_type=F32)
            state_t = state_t * decay[n][:, ks[h]] + kv_t
        s_ref[h] = state_t

        o = _rms(o_intra + jnp.concatenate(o_inter, axis=0), gn_ref[...])
        gate = g_ref[:, vs[h]].astype(F32)
        o_ref[:, vs[h]] = (o * (gate * jax.nn.sigmoid(gate))).astype(BF16)


def _gla(proj, wa, ba, gn, batch, seq, ts, to_cast):
    nt = seq // ts
    cast_in, cast_out, cast_shapes = _cast_specs(to_cast, batch * nt, lambda b, i: b * nt + i)
    kern = functools.partial(_gla_kernel, ts=ts)
    rows = lambda b, i: b * nt + i
    idx = jnp.arange(GLA_SUB)
    causal = (idx[:, None] >= idx[None, :]) & (idx[:, None] // GLA_CHUNK == idx[None, :] // GLA_CHUNK)
    return pl.pallas_call(
        kern,
        out_shape=[jax.ShapeDtypeStruct((batch * seq, GLA_V_W), BF16)] + cast_shapes,
        grid=(batch, nt),
        in_specs=[
            pl.BlockSpec((ts, GLA_QK_W), lambda b, i: (rows(b, i), COL_GQ // GLA_QK_W)),
            pl.BlockSpec((ts, GLA_QK_W), lambda b, i: (rows(b, i), COL_GK // GLA_QK_W)),
            pl.BlockSpec((ts, GLA_V_W), lambda b, i: (rows(b, i), COL_GV // GLA_V_W)),
            pl.BlockSpec((ts, GLA_V_W), lambda b, i: (rows(b, i), COL_GG // GLA_V_W)),
            pl.BlockSpec((ts, LANES), lambda b, i: (rows(b, i), COL_GA // LANES)),
            pl.BlockSpec((LANES, GLA_QK_W), lambda b, i: (0, 0)),
            pl.BlockSpec((1, GLA_QK_W), lambda b, i: (0, 0)),
            pl.BlockSpec((1, GLA_DV), lambda b, i: (0, 0)),
            pl.BlockSpec((GLA_SUB, GLA_SUB), lambda b, i: (0, 0)),
            pl.BlockSpec((GLA_SUB, GLA_SUB), lambda b, i: (0, 0)),
        ] + cast_in,
        out_specs=[pl.BlockSpec((ts, GLA_V_W), lambda b, i: (rows(b, i), 0))] + cast_out,
        scratch_shapes=[pltpu.VMEM((GLA_HEADS, GLA_DV, GLA_DK), F32)],
        compiler_params=pltpu.CompilerParams(
            dimension_semantics=("arbitrary", "arbitrary"),
            vmem_limit_bytes=VMEM_LIMIT),
        name="gla",
    )(proj, proj, proj, proj, proj, wa, ba, gn, causal.astype(BF16), causal.astype(F32), *to_cast)


def _swa_kernel(sinks_ref, q_ref, k_ref, v_ref, cos_ref, sin_ref, eye_ref, *rest):
    n_cast = (len(rest) - 3) // 2
    o_ref = rest[n_cast]
    k2p_ref, vtp_ref = rest[-2:]
    _cast_slabs(rest[:n_cast], rest[n_cast + 1:-2])
    step = pl.program_id(1)
    wb = WINDOW
    hd = SWA_HEAD_DIM
    half = hd // 2
    nblk = SWA_BLOCKS
    rows = nblk * wb
    pairs_per_kv = SWA_Q_W // LANES // SWA_KV_HEADS
    kvs = range(SWA_KV_HEADS)
    lane = lax.broadcasted_iota(jnp.int32, (rows, LANES), 1)
    head_a = (lane % hd) < half

    @pl.when(step == 0)
    def _():
        k2p_ref[...] = jnp.zeros_like(k2p_ref)
        vtp_ref[...] = jnp.zeros_like(vtp_ref)

    cos = cos_ref[...]
    sin = sin_ref[...]

    def rope(t, cs, sn):
        return t * cs + pltpu.roll(t, hd, 1) * sn

    k_r = rope(k_ref[...].astype(F32), cos, sin)
    k2_cur = [jnp.where(head_a, k_r, pltpu.roll(k_r, half, 1)).astype(BF16),
              jnp.where(head_a, pltpu.roll(k_r, LANES - half, 1), k_r).astype(BF16)]
    k2_all = [jnp.concatenate([k2p_ref[c], k2_cur[c]], axis=0) for c in kvs]
    v_t = v_ref[...].astype(F32).T
    ones_rows = (lax.broadcasted_iota(jnp.int32, (SWA_VT_ROWS - hd, rows), 0) == 0).astype(F32)
    vt_cur = [jnp.concatenate([v_t[c * hd:(c + 1) * hd], ones_rows], axis=0).astype(BF16) for c in kvs]
    vt_all = [jnp.concatenate([vtp_ref[c], vt_cur[c]], axis=1) for c in kvs]

    ki = lax.broadcasted_iota(jnp.int32, (2 * wb, wb), 0)
    qi = lax.broadcasted_iota(jnp.int32, (2 * wb, wb), 1) + wb
    band = (ki <= qi) & (qi - ki < WINDOW)
    bias = jnp.where(band, 0.0, NEG_INF).astype(BF16)
    bias_first = jnp.where(band & ((step > 0) | (ki >= wb)), 0.0, NEG_INF).astype(BF16)

    qscale = (hd ** -0.5) * LOG2E
    cos_q = cos * qscale
    sin_q = sin * qscale

    n_tiles = SWA_Q_W // LANES
    lhs = []
    for t in range(n_tiles):
        q_r = rope(q_ref[:, t * LANES:(t + 1) * LANES].astype(F32), cos_q, sin_q)
        lhs.append([jnp.where(head_a, q_r, 0.0).astype(BF16), jnp.where(head_a, 0.0, q_r).astype(BF16)])
    st = []
    for x in range(nblk):
        for c in kvs:
            rows_x = slice(x * wb, (x + 1) * wb)
            lhs_xc = jnp.concatenate([lhs[t][i][rows_x] for t in range(c * pairs_per_kv, (c + 1) * pairs_per_kv)
                                      for i in range(2)], axis=0)
            lhs_xc = jnp.concatenate([lhs_xc, eye_ref[...]], axis=1)
            keys = jnp.concatenate([k2_all[c][x * wb:(x + 2) * wb], bias_first if x == 0 else bias], axis=1)
            st.append(lax.dot_general(keys, lhs_xc, _NT, preferred_element_type=F32))
    st = jnp.concatenate(st, axis=1)
    m = jnp.max(st, axis=0, keepdims=True)
    e = jnp.exp2(st - m).astype(BF16)
    sink = jnp.concatenate([jnp.full((1, wb), sinks_ref[i], F32) for i in range(SWA_HEADS)] * nblk, axis=1) * LOG2E
    sink_term = jnp.exp2(sink - m)
    cols_per_kv = 2 * pairs_per_kv * wb
    for x in range(nblk):
        for c in kvs:
            cols = slice((x * SWA_KV_HEADS + c) * cols_per_kv, (x * SWA_KV_HEADS + c + 1) * cols_per_kv)
            ot = jnp.dot(vt_all[c][:, x * wb:(x + 2) * wb], e[:, cols], preferred_element_type=F32)
            o_n = ot[:hd] * (1.0 / (ot[hd:hd + 1] + sink_term[:, cols]))
            for n in range(pairs_per_kv):
                t = c * pairs_per_kv + n
                pair = jnp.concatenate([o_n[:, 2 * n * wb:(2 * n + 1) * wb], o_n[:, (2 * n + 1) * wb:(2 * n + 2) * wb]],
                                       axis=0)
                o_ref[x * wb:(x + 1) * wb, t * LANES:(t + 1) * LANES] = pair.T.astype(BF16)

    for c in kvs:
        k2p_ref[c] = k2_cur[c][rows - wb:]
        vtp_ref[c] = vt_cur[c][:, rows - wb:]


def _swa(proj, sinks, cos_tab, sin_tab, batch, seq, to_cast):
    rows = SWA_BLOCKS * WINDOW
    ns = seq // rows
    cast_in, cast_out, cast_shapes = _cast_specs(to_cast, batch * ns, lambda b, n: b * ns + n)
    eye = jnp.tile(jnp.eye(WINDOW, dtype=BF16), (SWA_HEADS // SWA_KV_HEADS, 1))
    tile = lambda b, n: b * ns + n
    return pl.pallas_call(
        _swa_kernel,
        out_shape=[jax.ShapeDtypeStruct((batch * seq, SWA_Q_W), BF16)] + cast_shapes,
        grid=(batch, ns),
        in_specs=[
            pl.BlockSpec(memory_space=pltpu.SMEM),
            pl.BlockSpec((rows, SWA_Q_W), lambda b, n: (tile(b, n), COL_SQ // SWA_Q_W)),
            pl.BlockSpec((rows, SWA_KV_W), lambda b, n: (tile(b, n), COL_SK // SWA_KV_W)),
            pl.BlockSpec((rows, SWA_KV_W), lambda b, n: (tile(b, n), COL_SV // SWA_KV_W)),
            pl.BlockSpec((rows, LANES), lambda b, n: (tile(b, n), 0)),
            pl.BlockSpec((rows, LANES), lambda b, n: (tile(b, n), 0)),
            pl.BlockSpec((SWA_HEADS // SWA_KV_HEADS * WINDOW, WINDOW), lambda b, n: (0, 0)),
        ] + cast_in,
        out_specs=[pl.BlockSpec((rows, SWA_Q_W), lambda b, n: (tile(b, n), 0))] + cast_out,
        scratch_shapes=[pltpu.VMEM((SWA_KV_HEADS, WINDOW, LANES), BF16),
                        pltpu.VMEM((SWA_KV_HEADS, SWA_VT_ROWS, WINDOW), BF16)],
        compiler_params=pltpu.CompilerParams(
            dimension_semantics=("arbitrary", "arbitrary"),
            vmem_limit_bytes=VMEM_LIMIT),
        name="swa",
    )(sinks, proj, proj, proj, cos_tab, sin_tab, eye, *to_cast)


def _outproj_kernel(x_ref, a_ref, b_ref, wa_ref, wb_ref, h_ref, *, tm, rc):
    for r in range(tm // rc):
        rs = pl.ds(r * rc, rc)
        acc = jnp.dot(a_ref[rs, :], wa_ref[...], preferred_element_type=F32)
        acc = acc + jnp.dot(b_ref[rs, :], wb_ref[...], preferred_element_type=F32)
        h_ref[rs, :] = x_ref[rs, :] + acc


def _outproj(x2, o_gla, o_swa, w, tm, rc):
    t = x2.shape[0]
    return pl.pallas_call(
        functools.partial(_outproj_kernel, tm=tm, rc=rc),
        out_shape=jax.ShapeDtypeStruct((t, D_MODEL), F32),
        grid=(t // tm,),
        in_specs=[
            pl.BlockSpec((tm, D_MODEL), lambda i: (i, 0)),
            pl.BlockSpec((tm, GLA_V_W), lambda i: (i, 0)),
            pl.BlockSpec((tm, SWA_Q_W), lambda i: (i, 0)),
            pl.BlockSpec((GLA_V_W, D_MODEL), lambda i: (0, 0), pipeline_mode=pl.Buffered(1)),
            pl.BlockSpec((SWA_Q_W, D_MODEL), lambda i: (GLA_V_W // SWA_Q_W, 0), pipeline_mode=pl.Buffered(1)),
        ],
        out_specs=pl.BlockSpec((tm, D_MODEL), lambda i: (i, 0)),
        compiler_params=pltpu.CompilerParams(
            dimension_semantics=("arbitrary",),
            vmem_limit_bytes=VMEM_LIMIT),
        name="outproj",
    )(x2, o_gla, o_swa, w, w)


def _ffn_kernel(h_ref, gf_ref, wg_ref, wu_ref, cw_ref, cb_ref, wd_ref, gl_ref, o_ref,
                hn_ref, carry_ref, *, tm, rc, seq):
    i = pl.program_id(0)
    j = pl.program_id(1)
    nj = pl.num_programs(1)

    def step(first, last):
        seq_start = (i * tm) % seq == 0
        prev = jnp.where(seq_start, 0.0, carry_ref[j])
        rows = lax.broadcasted_iota(jnp.int32, prev.shape, 0)
        cb = cb_ref[...]
        cw0, cw1, cw2 = cw_ref[0:1, :], cw_ref[1:2, :], cw_ref[2:3, :]

        for r in range(tm // rc):
            rs = pl.ds(r * rc, rc)
            if first:
                res = h_ref[rs, :]
                hn = _rms(res, gf_ref[...]).astype(BF16)
                hn_ref[rs, :] = hn
            else:
                res = o_ref[rs, :]
                hn = hn_ref[rs, :]
            gate = jnp.dot(hn, wg_ref[...], preferred_element_type=F32)
            up = jnp.dot(hn, wu_ref[...], preferred_element_type=F32)

            def shifted(d):
                rolled = pltpu.roll(gate, d, 0)
                top = jnp.where(rows < d, pltpu.roll(prev, d, 0), rolled[:SUBLANES])
                return jnp.concatenate([top, rolled[SUBLANES:]], axis=0)

            conv = cb + cw0 * shifted(2) + cw1 * shifted(1) + cw2 * gate
            act = (conv * jax.nn.sigmoid(conv) * up).astype(BF16)
            acc = res + jnp.dot(act, wd_ref[...], preferred_element_type=F32)
            o_ref[rs, :] = _rms(acc, gl_ref[...]) if last else acc
            prev = gate[rc - SUBLANES:, :]
        carry_ref[j] = prev

    pl.when(j == 0)(functools.partial(step, True, False))
    pl.when((j > 0) & (j < nj - 1))(functools.partial(step, False, False))
    pl.when(j == nj - 1)(functools.partial(step, False, True))


def _ffn(h, gf, wg, wu, cw, cb, wd, gl, seq, tm, tf, rc):
    t = h.shape[0]
    nj = D_FF // tf
    kern = functools.partial(_ffn_kernel, tm=tm, rc=rc, seq=seq)
    return pl.pallas_call(
        kern,
        out_shape=jax.ShapeDtypeStruct((t, D_MODEL), F32),
        grid=(t // tm, nj),
        in_specs=[
            pl.BlockSpec((tm, D_MODEL), lambda i, j: (i, 0)),
            pl.BlockSpec((1, D_MODEL), lambda i, j: (0, 0)),
            pl.BlockSpec((D_MODEL, tf), lambda i, j: (0, j)),
            pl.BlockSpec((D_MODEL, tf), lambda i, j: (0, j)),
            pl.BlockSpec((CONV_WIDTH, tf), lambda i, j: (0, j)),
            pl.BlockSpec((1, tf), lambda i, j: (0, j)),
            pl.BlockSpec((tf, D_MODEL), lambda i, j: (j, 0)),
            pl.BlockSpec((1, D_MODEL), lambda i, j: (0, 0)),
        ],
        out_specs=pl.BlockSpec((tm, D_MODEL), lambda i, j: (i, 0)),
        scratch_shapes=[pltpu.VMEM((tm, D_MODEL), BF16), pltpu.VMEM((nj, SUBLANES, tf), F32)],
        compiler_params=pltpu.CompilerParams(
            dimension_semantics=("arbitrary", "arbitrary"),
            vmem_limit_bytes=VMEM_LIMIT),
        name="convffn",
    )(h, gf, wg, wu, cw, cb, wd, gl)


def kernel(x, positions, attn_norm, w_in, w_a_up, b_a_up, gla_norm, sinks, w_out, ffn_norm,
           w_gate, w_up, conv_w, conv_b, w_down, final_norm):
    batch, seq, _ = x.shape
    t = batch * seq
    assert w_in.shape[0] == 1, "the final norm is fused into the single layer's FFN kernel"
    x2 = x.reshape(t, D_MODEL)
    pos3 = positions.reshape(t // WINDOW, 1, WINDOW)
    half = SWA_HEAD_DIM // 2
    freq_col = (ROPE_THETA ** (-jnp.arange(half, dtype=F32) / half)).reshape(half, 1)

    w_proj = _wprep(jnp.swapaxes(w_in[0], 0, 1), tk=256)
    wa = jnp.concatenate(
        [w_a_up[0], jnp.zeros((LANES - GLA_RANK, GLA_QK_W), F32)], axis=0).astype(BF16)

    proj, cos_tab, sin_tab, wg_bf, wu_bf = _inproj(x2, attn_norm[0].reshape(1, D_MODEL), w_proj, pos3, freq_col,
                                                   tm=512, rc=128, to_cast=[w_gate[0], w_up[0]])
    o_gla, wd_bf, wo_bf = _gla(proj, wa, b_a_up[0].reshape(1, GLA_QK_W), gla_norm[0].reshape(1, GLA_DV),
                               batch, seq, ts=1024, to_cast=[w_down[0], w_out[0]])
    o_swa, = _swa(proj, sinks[0], cos_tab, sin_tab, batch, seq, to_cast=[])
    h = _outproj(x2, o_gla, o_swa, wo_bf, tm=1024, rc=512)
    y = _ffn(h, ffn_norm[0].reshape(1, D_MODEL), wg_bf, wu_bf, conv_w[0], conv_b[0].reshape(1, D_FF), wd_bf,
             final_norm.reshape(1, D_MODEL), seq, tm=1024, tf=512, rc=512)
    return y.reshape(batch, seq, D_MODEL)
```

```python
import functools

import jax
import jax.numpy as jnp
from jax import lax
from jax.experimental import pallas as pl
from jax.experimental.pallas import tpu as pltpu

D_MODEL = 2048
GLA_HEADS = 4
GLA_DK = 128
GLA_DV = 256
GLA_RANK = 16
GLA_GATE_NORM = 16.0
GLA_CHUNK = 64
GLA_SUB = 256
SWA_HEADS = 16
SWA_KV_HEADS = 2
SWA_HEAD_DIM = 64
WINDOW = 128
ROPE_THETA = 10000.0
D_FF = 5632
CONV_WIDTH = 3
EPS = 1e-6
NEG_INF = -1e30
LOG2E = 1.4426950408889634
SWA_VT_ROWS = SWA_HEAD_DIM + 16
SWA_BLOCKS = 2

GLA_QK_W = GLA_HEADS * GLA_DK
GLA_V_W = GLA_HEADS * GLA_DV
SWA_Q_W = SWA_HEADS * SWA_HEAD_DIM
SWA_KV_W = SWA_KV_HEADS * SWA_HEAD_DIM

LANES = 128
SUBLANES = 8

COL_GQ = 0
COL_GK = COL_GQ + GLA_QK_W
COL_GV = COL_GK + GLA_QK_W
COL_GG = COL_GV + GLA_V_W
COL_SQ = COL_GG + GLA_V_W
COL_SK = COL_SQ + SWA_Q_W
COL_SV = COL_SK + SWA_KV_W
COL_GA = COL_SV + SWA_KV_W
PROJ_TN = 1536
PROJ_W = 3 * PROJ_TN

VMEM_LIMIT = 58 * 1024 * 1024

F32 = jnp.float32
BF16 = jnp.bfloat16

_NT = (((1,), (1,)), ((), ()))
_TN = (((0,), (0,)), ((), ()))


def _rms(x, gain):
    return x * lax.rsqrt(jnp.mean(x * x, axis=-1, keepdims=True) + EPS) * gain


def _cast_specs(weights, steps, step_index):
    in_specs, out_specs, out_shapes = [], [], []
    for w in weights:
        rows, cols = w.shape
        slab = rows // steps
        assert slab * steps == rows and slab % (2 * SUBLANES) == 0, (w.shape, steps)
        spec = pl.BlockSpec((slab, cols), lambda *g: (step_index(*g), 0))
        in_specs.append(spec)
        out_specs.append(spec)
        out_shapes.append(jax.ShapeDtypeStruct(w.shape, BF16))
    return in_specs, out_specs, out_shapes


def _cast_slabs(src_refs, dst_refs):
    for src, dst in zip(src_refs, dst_refs):
        dst[...] = src[...].astype(BF16)


def _wprep_kernel(wt_ref, o_ref):
    tk = wt_ref.shape[1]
    half = SWA_HEAD_DIM // 2
    o_ga = 2 * GLA_QK_W + 2 * GLA_V_W
    o_sq = o_ga + GLA_RANK

    def put(col, rows):
        o_ref[:, col:col + rows.shape[0]] = rows.T.astype(BF16)

    put(0, wt_ref[:o_ga, :])
    for i in range((SWA_Q_W + SWA_KV_W) // LANES):
        r0 = o_sq + i * LANES
        blk = [wt_ref[r0 + j * half:r0 + (j + 1) * half, :] for j in range(4)]
        put(COL_SQ + i * LANES, jnp.concatenate([blk[0], blk[2], blk[1], blk[3]], axis=0))
    put(COL_SV, wt_ref[o_sq + SWA_Q_W + SWA_KV_W:, :])
    keep = lax.broadcasted_iota(jnp.int32, (LANES, tk), 0) < GLA_RANK
    put(COL_GA, jnp.where(keep, wt_ref[o_ga:o_ga + LANES, :], 0.0))
    o_ref[:, COL_GA + LANES:] = jnp.zeros((tk, PROJ_W - COL_GA - LANES), BF16)


def _wprep(wt, tk):
    n, k = wt.shape
    return pl.pallas_call(
        _wprep_kernel,
        out_shape=jax.ShapeDtypeStruct((k, PROJ_W), BF16),
        grid=(k // tk,),
        in_specs=[pl.BlockSpec((n, tk), lambda i: (0, i))],
        out_specs=pl.BlockSpec((tk, PROJ_W), lambda i: (i, 0)),
        compiler_params=pltpu.CompilerParams(
            dimension_semantics=("arbitrary",),
            vmem_limit_bytes=VMEM_LIMIT),
        name="wprep",
    )(wt)


def _inproj_kernel(x_ref, g_ref, w_ref, pos_ref, freq_ref, *rest, tm, rc):
    n_cast = (len(rest) - 3) // 2
    o_ref, cos_ref, sin_ref = rest[n_cast:n_cast + 3]
    _cast_slabs(rest[:n_cast], rest[n_cast + 3:])

    half = SWA_HEAD_DIM // 2
    low = lax.broadcasted_iota(jnp.int32, (WINDOW, LANES), 1) < SWA_HEAD_DIM
    for b in range(tm // WINDOW):
        ang_t = freq_ref[...] * pos_ref[b].astype(F32)
        cs = jnp.concatenate([jnp.cos(ang_t)] * (LANES // half), axis=0).T
        sn = jnp.concatenate([jnp.sin(ang_t)] * (LANES // half), axis=0).T
        cos_ref[b * WINDOW:(b + 1) * WINDOW, :] = cs
        sin_ref[b * WINDOW:(b + 1) * WINDOW, :] = jnp.where(low, -sn, sn)

    for r in range(tm // rc):
        rs = pl.ds(r * rc, rc)
        u = _rms(x_ref[rs, :], g_ref[...]).astype(BF16)
        o_ref[rs, :] = jnp.dot(u, w_ref[...], preferred_element_type=F32).astype(BF16)


def _inproj(x2, gain, w, pos3, freq_col, tm, rc, to_cast):
    t = x2.shape[0]
    cast_in, cast_out, cast_shapes = _cast_specs(to_cast, t // tm, lambda i: i)
    return pl.pallas_call(
        functools.partial(_inproj_kernel, tm=tm, rc=rc),
        out_shape=[jax.ShapeDtypeStruct((t, PROJ_W), BF16), jax.ShapeDtypeStruct((t, LANES), F32),
                   jax.ShapeDtypeStruct((t, LANES), F32)] + cast_shapes,
        grid=(t // tm,),
        in_specs=[
            pl.BlockSpec((tm, D_MODEL), lambda i: (i, 0)),
            pl.BlockSpec((1, D_MODEL), lambda i: (0, 0)),
            pl.BlockSpec((D_MODEL, PROJ_W), lambda i: (0, 0), pipeline_mode=pl.Buffered(1)),
            pl.BlockSpec((tm // WINDOW, 1, WINDOW), lambda i: (i, 0, 0)),
            pl.BlockSpec((SWA_HEAD_DIM // 2, 1), lambda i: (0, 0)),
        ] + cast_in,
        out_specs=[pl.BlockSpec((tm, PROJ_W), lambda i: (i, 0)), pl.BlockSpec((tm, LANES), lambda i: (i, 0)),
                   pl.BlockSpec((tm, LANES), lambda i: (i, 0))] + cast_out,
        compiler_params=pltpu.CompilerParams(
            dimension_semantics=("arbitrary",),
            vmem_limit_bytes=VMEM_LIMIT),
        name="inproj",
    )(x2, gain, w, pos3, freq_col, *to_cast)


def _gla_kernel(q_ref, k_ref, v_ref, g_ref, a_ref, wa_ref, ba_ref, gn_ref, tril_ref, mask_ref, *rest, ts):
    c = GLA_CHUNK
    nc = ts // c
    n_cast = (len(rest) - 2) // 2
    o_ref, s_ref = rest[n_cast], rest[-1]
    _cast_slabs(rest[:n_cast], rest[n_cast + 1:-1])
    heads = range(GLA_HEADS)
    ks = [slice(h * GLA_DK, (h + 1) * GLA_DK) for h in heads]
    vs = [slice(h * GLA_DV, (h + 1) * GLA_DV) for h in heads]

    @pl.when(pl.program_id(1) == 0)
    def _():
        s_ref[...] = jnp.zeros_like(s_ref)

    z = jnp.dot(a_ref[...], wa_ref[...], preferred_element_type=F32) + ba_ref[...]
    log_a = (jnp.minimum(z, 0.0) - jnp.log(1.0 + jnp.exp(-jnp.abs(z)))) * (1.0 / GLA_GATE_NORM)

    la_hi = log_a.astype(BF16)
    la_lo = (log_a - la_hi.astype(F32)).astype(BF16)
    la_hl = jnp.concatenate([la_hi, la_lo], axis=1)
    subs = [slice(s * GLA_SUB, (s + 1) * GLA_SUB) for s in range(ts // GLA_SUB)]
    cum2 = jnp.concatenate([jnp.dot(tril_ref[...], la_hl[rs], preferred_element_type=F32) for rs in subs], axis=0)
    bcum = cum2[:, :GLA_QK_W] + cum2[:, GLA_QK_W:]
    bcum3 = bcum.reshape(nc, c, GLA_QK_W)
    b_last = bcum3[:, c - 1:c, :]

    q = q_ref[...].astype(F32) * (GLA_DK ** -0.5)
    k = k_ref[...].astype(F32)
    q_e = (q * jnp.exp(bcum)).astype(BF16)
    k_e = (k * jnp.exp(-bcum)).astype(BF16)
    k_d = (k * jnp.exp(b_last - bcum3).reshape(ts, GLA_QK_W)).astype(BF16)
    decay = jnp.exp(b_last)
    v = [v_ref[:, vs[h]] for h in heads]

    keep = mask_ref[...] != 0.0

    for h in heads:
        o_intra = []
        for rs in subs:
            scores = lax.dot_general(q_e[rs, ks[h]], k_e[rs, ks[h]], _NT, preferred_element_type=F32)
            scores = jnp.where(keep, scores, 0.0).astype(BF16)
            o_intra.append(jnp.dot(scores, v[h][rs], preferred_element_type=F32))
        o_intra = jnp.concatenate(o_intra, axis=0)

        state_t = s_ref[h]
        o_inter = []
        for n in range(nc):
            rows = slice(n * c, (n + 1) * c)
            o_inter.append(lax.dot_general(q_e[rows, ks[h]], state_t.astype(BF16), _NT,
                                           preferred_element_type=F32))
            kv_t = lax.dot_general(v[h][rows], k_d[rows, ks[h]], _TN, preferred_element_type=F32)
            state_t = state_t * decay[n][:, ks[h]] + kv_t
        s_ref[h] = state_t

        o = _rms(o_intra + jnp.concatenate(o_inter, axis=0), gn_ref[...])
        gate = g_ref[:, vs[h]].astype(F32)
        o_ref[:, vs[h]] = (o * (gate * jax.nn.sigmoid(gate))).astype(BF16)


def _gla(proj, wa, ba, gn, batch, seq, ts, to_cast):
    nt = seq // ts
    cast_in, cast_out, cast_shapes = _cast_specs(to_cast, batch * nt, lambda b, i: b * nt + i)
    kern = functools.partial(_gla_kernel, ts=ts)
    rows = lambda b, i: b * nt + i
    idx = jnp.arange(GLA_SUB)
    causal = (idx[:, None] >= idx[None, :]) & (idx[:, None] // GLA_CHUNK == idx[None, :] // GLA_CHUNK)
    return pl.pallas_call(
        kern,
        out_shape=[jax.ShapeDtypeStruct((batch * seq, GLA_V_W), BF16)] + cast_shapes,
        grid=(batch, nt),
        in_specs=[
            pl.BlockSpec((ts, GLA_QK_W), lambda b, i: (rows(b, i), COL_GQ // GLA_QK_W)),
            pl.BlockSpec((ts, GLA_QK_W), lambda b, i: (rows(b, i), COL_GK // GLA_QK_W)),
            pl.BlockSpec((ts, GLA_V_W), lambda b, i: (rows(b, i), COL_GV // GLA_V_W)),
            pl.BlockSpec((ts, GLA_V_W), lambda b, i: (rows(b, i), COL_GG // GLA_V_W)),
            pl.BlockSpec((ts, LANES), lambda b, i: (rows(b, i), COL_GA // LANES)),
            pl.BlockSpec((LANES, GLA_QK_W), lambda b, i: (0, 0)),
            pl.BlockSpec((1, GLA_QK_W), lambda b, i: (0, 0)),
            pl.BlockSpec((1, GLA_DV), lambda b, i: (0, 0)),
            pl.BlockSpec((GLA_SUB, GLA_SUB), lambda b, i: (0, 0)),
            pl.BlockSpec((GLA_SUB, GLA_SUB), lambda b, i: (0, 0)),
        ] + cast_in,
        out_specs=[pl.BlockSpec((ts, GLA_V_W), lambda b, i: (rows(b, i), 0))] + cast_out,
        scratch_shapes=[pltpu.VMEM((GLA_HEADS, GLA_DV, GLA_DK), F32)],
        compiler_params=pltpu.CompilerParams(
            dimension_semantics=("arbitrary", "arbitrary"),
            vmem_limit_bytes=VMEM_LIMIT),
        name="gla",
    )(proj, proj, proj, proj, proj, wa, ba, gn, causal.astype(BF16), causal.astype(F32), *to_cast)


def _swa_kernel(sinks_ref, q_ref, k_ref, v_ref, cos_ref, sin_ref, eye_ref, *rest):
    n_cast = (len(rest) - 3) // 2
    o_ref = rest[n_cast]
    k2p_ref, vtp_ref = rest[-2:]
    _cast_slabs(rest[:n_cast], rest[n_cast + 1:-2])
    step = pl.program_id(1)
    wb = WINDOW
    hd = SWA_HEAD_DIM
    half = hd // 2
    nblk = SWA_BLOCKS
    rows = nblk * wb
    pairs_per_kv = SWA_Q_W // LANES // SWA_KV_HEADS
    kvs = range(SWA_KV_HEADS)
    lane = lax.broadcasted_iota(jnp.int32, (rows, LANES), 1)
    head_a = (lane % hd) < half

    @pl.when(step == 0)
    def _():
        k2p_ref[...] = jnp.zeros_like(k2p_ref)
        vtp_ref[...] = jnp.zeros_like(vtp_ref)

    cos = cos_ref[...]
    sin = sin_ref[...]

    def rope(t, cs, sn):
        return t * cs + pltpu.roll(t, hd, 1) * sn

    k_r = rope(k_ref[...].astype(F32), cos, sin)
    k2_cur = [jnp.where(head_a, k_r, pltpu.roll(k_r, half, 1)).astype(BF16),
              jnp.where(head_a, pltpu.roll(k_r, LANES - half, 1), k_r).astype(BF16)]
    k2_all = [jnp.concatenate([k2p_ref[c], k2_cur[c]], axis=0) for c in kvs]
    v_t = v_ref[...].astype(F32).T
    ones_rows = (lax.broadcasted_iota(jnp.int32, (SWA_VT_ROWS - hd, rows), 0) == 0).astype(F32)
    vt_cur = [jnp.concatenate([v_t[c * hd:(c + 1) * hd], ones_rows], axis=0).astype(BF16) for c in kvs]
    vt_all = [jnp.concatenate([vtp_ref[c], vt_cur[c]], axis=1) for c in kvs]

    ki = lax.broadcasted_iota(jnp.int32, (2 * wb, wb), 0)
    qi = lax.broadcasted_iota(jnp.int32, (2 * wb, wb), 1) + wb
    band = (ki <= qi) & (qi - ki < WINDOW)
    bias = jnp.where(band, 0.0, NEG_INF).astype(BF16)
    bias_first = jnp.where(band & ((step > 0) | (ki >= wb)), 0.0, NEG_INF).astype(BF16)

    qscale = (hd ** -0.5) * LOG2E
    cos_q = cos * qscale
    sin_q = sin * qscale

    n_tiles = SWA_Q_W // LANES
    lhs = []
    for t in range(n_tiles):
        q_r = rope(q_ref[:, t * LANES:(t + 1) * LANES].astype(F32), cos_q, sin_q)
        lhs.append([jnp.where(head_a, q_r, 0.0).astype(BF16), jnp.where(head_a, 0.0, q_r).astype(BF16)])
    st = []
    for x in range(nblk):
        for c in kvs:
            rows_x = slice(x * wb, (x + 1) * wb)
            lhs_xc = jnp.concatenate([lhs[t][i][rows_x] for t in range(c * pairs_per_kv, (c + 1) * pairs_per_kv)
                                      for i in range(2)], axis=0)
            lhs_xc = jnp.concatenate([lhs_xc, eye_ref[...]], axis=1)
            keys = jnp.concatenate([k2_all[c][x * wb:(x + 2) * wb], bias_first if x == 0 else bias], axis=1)
            st.append(lax.dot_general(keys, lhs_xc, _NT, preferred_element_type=F32))
    st = jnp.concatenate(st, axis=1)
    m = jnp.max(st, axis=0, keepdims=True)
    e = jnp.exp2(st - m).astype(BF16)
    sink = jnp.concatenate([jnp.full((1, wb), sinks_ref[i], F32) for i in range(SWA_HEADS)] * nblk, axis=1) * LOG2E
    sink_term = jnp.exp2(sink - m)
    cols_per_kv = 2 * pairs_per_kv * wb
    for x in range(nblk):
        for c in kvs:
            cols = slice((x * SWA_KV_HEADS + c) * cols_per_kv, (x * SWA_KV_HEADS + c + 1) * cols_per_kv)
            ot = jnp.dot(vt_all[c][:, x * wb:(x + 2) * wb], e[:, cols], preferred_element_type=F32)
            o_n = ot[:hd] * (1.0 / (ot[hd:hd + 1] + sink_term[:, cols]))
            for n in range(pairs_per_kv):
                t = c * pairs_per_kv + n
                pair = jnp.concatenate([o_n[:, 2 * n * wb:(2 * n + 1) * wb], o_n[:, (2 * n + 1) * wb:(2 * n + 2) * wb]],
                                       axis=0)
                o_ref[x * wb:(x + 1) * wb, t * LANES:(t + 1) * LANES] = pair.T.astype(BF16)

    for c in kvs:
        k2p_ref[c] = k2_cur[c][rows - wb:]
        vtp_ref[c] = vt_cur[c][:, rows - wb:]


def _swa(proj, sinks, cos_tab, sin_tab, batch, seq, to_cast):
    rows = SWA_BLOCKS * WINDOW
    ns = seq // rows
    cast_in, cast_out, cast_shapes = _cast_specs(to_cast, batch * ns, lambda b, n: b * ns + n)
    eye = jnp.tile(jnp.eye(WINDOW, dtype=BF16), (SWA_HEADS // SWA_KV_HEADS, 1))
    tile = lambda b, n: b * ns + n
    return pl.pallas_call(
        _swa_kernel,
        out_shape=[jax.ShapeDtypeStruct((batch * seq, SWA_Q_W), BF16)] + cast_shapes,
        grid=(batch, ns),
        in_specs=[
            pl.BlockSpec(memory_space=pltpu.SMEM),
            pl.BlockSpec((rows, SWA_Q_W), lambda b, n: (tile(b, n), COL_SQ // SWA_Q_W)),
            pl.BlockSpec((rows, SWA_KV_W), lambda b, n: (tile(b, n), COL_SK // SWA_KV_W)),
            pl.BlockSpec((rows, SWA_KV_W), lambda b, n: (tile(b, n), COL_SV // SWA_KV_W)),
            pl.BlockSpec((rows, LANES), lambda b, n: (tile(b, n), 0)),
            pl.BlockSpec((rows, LANES), lambda b, n: (tile(b, n), 0)),
            pl.BlockSpec((SWA_HEADS // SWA_KV_HEADS * WINDOW, WINDOW), lambda b, n: (0, 0)),
        ] + cast_in,
        out_specs=[pl.BlockSpec((rows, SWA_Q_W), lambda b, n: (tile(b, n), 0))] + cast_out,
        scratch_shapes=[pltpu.VMEM((SWA_KV_HEADS, WINDOW, LANES), BF16),
                        pltpu.VMEM((SWA_KV_HEADS, SWA_VT_ROWS, WINDOW), BF16)],
        compiler_params=pltpu.CompilerParams(
            dimension_semantics=("arbitrary", "arbitrary"),
            vmem_limit_bytes=VMEM_LIMIT),
        name="swa",
    )(sinks, proj, proj, proj, cos_tab, sin_tab, eye, *to_cast)


def _outproj_kernel(x_ref, a_ref, b_ref, wa_ref, wb_ref, h_ref, *, tm, rc):
    for r in range(tm // rc):
        rs = pl.ds(r * rc, rc)
        acc = jnp.dot(a_ref[rs, :], wa_ref[...], preferred_element_type=F32)
        acc = acc + jnp.dot(b_ref[rs, :], wb_ref[...], preferred_element_type=F32)
        h_ref[rs, :] = x_ref[rs, :] + acc


def _outproj(x2, o_gla, o_swa, w, tm, rc):
    t = x2.shape[0]
    return pl.pallas_call(
        functools.partial(_outproj_kernel, tm=tm, rc=rc),
        out_shape=jax.ShapeDtypeStruct((t, D_MODEL), F32),
        grid=(t // tm,),
        in_specs=[
            pl.BlockSpec((tm, D_MODEL), lambda i: (i, 0)),
            pl.BlockSpec((tm, GLA_V_W), lambda i: (i, 0)),
            pl.BlockSpec((tm, SWA_Q_W), lambda i: (i, 0)),
            pl.BlockSpec((GLA_V_W, D_MODEL), lambda i: (0, 0), pipeline_mode=pl.Buffered(1)),
            pl.BlockSpec((SWA_Q_W, D_MODEL), lambda i: (GLA_V_W // SWA_Q_W, 0), pipeline_mode=pl.Buffered(1)),
        ],
        out_specs=pl.BlockSpec((tm, D_MODEL), lambda i: (i, 0)),
        compiler_params=pltpu.CompilerParams(
            dimension_semantics=("arbitrary",),
            vmem_limit_bytes=VMEM_LIMIT),
        name="outproj",
    )(x2, o_gla, o_swa, w, w)


def _ffn_kernel(h_ref, gf_ref, wg_ref, wu_ref, cw_ref, cb_ref, wd_ref, gl_ref, o_ref,
                hn_ref, carry_ref, *, tm, rc, seq):
    i = pl.program_id(0)
    j = pl.program_id(1)
    nj = pl.num_programs(1)

    def step(first, last):
        seq_start = (i * tm) % seq == 0
        prev = jnp.where(seq_start, 0.0, carry_ref[j])
        rows = lax.broadcasted_iota(jnp.int32, prev.shape, 0)
        cb = cb_ref[...]
        cw0, cw1, cw2 = cw_ref[0:1, :], cw_ref[1:2, :], cw_ref[2:3, :]

        for r in range(tm // rc):
            rs = pl.ds(r * rc, rc)
            if first:
                res = h_ref[rs, :]
                hn = _rms(res, gf_ref[...]).astype(BF16)
                hn_ref[rs, :] = hn
            else:
                res = o_ref[rs, :]
                hn = hn_ref[rs, :]
            gate = jnp.dot(hn, wg_ref[...], preferred_element_type=F32)
            up = jnp.dot(hn, wu_ref[...], preferred_element_type=F32)

            def shifted(d):
                rolled = pltpu.roll(gate, d, 0)
                top = jnp.where(rows < d, pltpu.roll(prev, d, 0), rolled[:SUBLANES])
                return jnp.concatenate([top, rolled[SUBLANES:]], axis=0)

            conv = cb + cw0 * shifted(2) + cw1 * shifted(1) + cw2 * gate
            act = (conv * jax.nn.sigmoid(conv) * up).astype(BF16)
            acc = res + jnp.dot(act, wd_ref[...], preferred_element_type=F32)
            o_ref[rs, :] = _rms(acc, gl_ref[...]) if last else acc
            prev = gate[rc - SUBLANES:, :]
        carry_ref[j] = prev

    pl.when(j == 0)(functools.partial(step, True, False))
    pl.when((j > 0) & (j < nj - 1))(functools.partial(step, False, False))
    pl.when(j == nj - 1)(functools.partial(step, False, True))


def _ffn(h, gf, wg, wu, cw, cb, wd, gl, seq, tm, tf, rc):
    t = h.shape[0]
    nj = D_FF // tf
    kern = functools.partial(_ffn_kernel, tm=tm, rc=rc, seq=seq)
    return pl.pallas_call(
        kern,
        out_shape=jax.ShapeDtypeStruct((t, D_MODEL), F32),
        grid=(t // tm, nj),
        in_specs=[
            pl.BlockSpec((tm, D_MODEL), lambda i, j: (i, 0)),
            pl.BlockSpec((1, D_MODEL), lambda i, j: (0, 0)),
            pl.BlockSpec((D_MODEL, tf), lambda i, j: (0, j)),
            pl.BlockSpec((D_MODEL, tf), lambda i, j: (0, j)),
            pl.BlockSpec((CONV_WIDTH, tf), lambda i, j: (0, j)),
            pl.BlockSpec((1, tf), lambda i, j: (0, j)),
            pl.BlockSpec((tf, D_MODEL), lambda i, j: (j, 0)),
            pl.BlockSpec((1, D_MODEL), lambda i, j: (0, 0)),
        ],
        out_specs=pl.BlockSpec((tm, D_MODEL), lambda i, j: (i, 0)),
        scratch_shapes=[pltpu.VMEM((tm, D_MODEL), BF16), pltpu.VMEM((nj, SUBLANES, tf), F32)],
        compiler_params=pltpu.CompilerParams(
            dimension_semantics=("arbitrary", "arbitrary"),
            vmem_limit_bytes=VMEM_LIMIT),
        name="convffn",
    )(h, gf, wg, wu, cw, cb, wd, gl)


def kernel(x, positions, attn_norm, w_in, w_a_up, b_a_up, gla_norm, sinks, w_out, ffn_norm,
           w_gate, w_up, conv_w, conv_b, w_down, final_norm):
    batch, seq, _ = x.shape
    t = batch * seq
    assert w_in.shape[0] == 1, "the final norm is fused into the single layer's FFN kernel"
    x2 = x.reshape(t, D_MODEL)
    pos3 = positions.reshape(t // WINDOW, 1, WINDOW)
    half = SWA_HEAD_DIM // 2
    freq_col = (ROPE_THETA ** (-jnp.arange(half, dtype=F32) / half)).reshape(half, 1)

    w_proj = _wprep(jnp.swapaxes(w_in[0], 0, 1), tk=256)
    wa = jnp.concatenate(
        [w_a_up[0], jnp.zeros((LANES - GLA_RANK, GLA_QK_W), F32)], axis=0).astype(BF16)

    proj, cos_tab, sin_tab, wg_bf, wu_bf = _inproj(x2, attn_norm[0].reshape(1, D_MODEL), w_proj, pos3, freq_col,
                                                   tm=512, rc=128, to_cast=[w_gate[0], w_up[0]])
    o_gla, wd_bf, wo_bf = _gla(proj, wa, b_a_up[0].reshape(1, GLA_QK_W), gla_norm[0].reshape(1, GLA_DV),
                               batch, seq, ts=1024, to_cast=[w_down[0], w_out[0]])
    o_swa, = _swa(proj, sinks[0], cos_tab, sin_tab, batch, seq, to_cast=[])
    h = _outproj(x2, o_gla, o_swa, wo_bf, tm=1024, rc=512)
    y = _ffn(h, ffn_norm[0].reshape(1, D_MODEL), wg_bf, wu_bf, conv_w[0], conv_b[0].reshape(1, D_FF), wd_bf,
             final_norm.reshape(1, D_MODEL), seq, tm=1024, tf=512, rc=512)
    return y.reshape(batch, seq, D_MODEL)
```

```python
import functools

import jax
import jax.numpy as jnp
from jax import lax
from jax.experimental import pallas as pl
from jax.experimental.pallas import tpu as pltpu

D_MODEL = 2048
GLA_HEADS = 4
GLA_DK = 128
GLA_DV = 256
GLA_RANK = 16
GLA_GATE_NORM = 16.0
GLA_CHUNK = 64
GLA_SUB = 128
SWA_HEADS = 16
SWA_KV_HEADS = 2
SWA_HEAD_DIM = 64
WINDOW = 128
ROPE_THETA = 10000.0
D_FF = 5632
CONV_WIDTH = 3
EPS = 1e-6
NEG_INF = -1e30
LOG2E = 1.4426950408889634
SWA_VT_ROWS = SWA_HEAD_DIM + 16
SWA_BLOCKS = 4

GLA_QK_W = GLA_HEADS * GLA_DK
GLA_V_W = GLA_HEADS * GLA_DV
SWA_Q_W = SWA_HEADS * SWA_HEAD_DIM
SWA_KV_W = SWA_KV_HEADS * SWA_HEAD_DIM

LANES = 128
SUBLANES = 8

COL_GQ = 0
COL_GK = COL_GQ + GLA_QK_W
COL_GV = COL_GK + GLA_QK_W
COL_GG = COL_GV + GLA_V_W
COL_SQ = COL_GG + GLA_V_W
COL_SK = COL_SQ + SWA_Q_W
COL_SV = COL_SK + SWA_KV_W
COL_GA = COL_SV + SWA_KV_W
PROJ_TN = 1536
PROJ_W = 3 * PROJ_TN

VMEM_LIMIT = 58 * 1024 * 1024

F32 = jnp.float32
BF16 = jnp.bfloat16

_NT = (((1,), (1,)), ((), ()))
_TN = (((0,), (0,)), ((), ()))


def _rms(x, gain):
    return x * lax.rsqrt(jnp.mean(x * x, axis=-1, keepdims=True) + EPS) * gain


def _cast_specs(weights, steps, step_index):
    in_specs, out_specs, out_shapes = [], [], []
    for w in weights:
        rows, cols = w.shape
        slab = rows // steps
        assert slab * steps == rows and slab % (2 * SUBLANES) == 0, (w.shape, steps)
        spec = pl.BlockSpec((slab, cols), lambda *g: (step_index(*g), 0))
        in_specs.append(spec)
        out_specs.append(spec)
        out_shapes.append(jax.ShapeDtypeStruct(w.shape, BF16))
    return in_specs, out_specs, out_shapes


def _cast_slabs(src_refs, dst_refs):
    for src, dst in zip(src_refs, dst_refs):
        dst[...] = src[...].astype(BF16)


def _wprep_kernel(wt_ref, o_ref):
    tk = wt_ref.shape[1]
    half = SWA_HEAD_DIM // 2
    o_ga = 2 * GLA_QK_W + 2 * GLA_V_W
    o_sq = o_ga + GLA_RANK

    def put(col, rows):
        o_ref[:, col:col + rows.shape[0]] = rows.T.astype(BF16)

    put(0, wt_ref[:o_ga, :])
    for i in range((SWA_Q_W + SWA_KV_W) // LANES):
        r0 = o_sq + i * LANES
        blk = [wt_ref[r0 + j * half:r0 + (j + 1) * half, :] for j in range(4)]
        put(COL_SQ + i * LANES, jnp.concatenate([blk[0], blk[2], blk[1], blk[3]], axis=0))
    put(COL_SV, wt_ref[o_sq + SWA_Q_W + SWA_KV_W:, :])
    keep = lax.broadcasted_iota(jnp.int32, (LANES, tk), 0) < GLA_RANK
    put(COL_GA, jnp.where(keep, wt_ref[o_ga:o_ga + LANES, :], 0.0))
    o_ref[:, COL_GA + LANES:] = jnp.zeros((tk, PROJ_W - COL_GA - LANES), BF16)


def _wprep(wt, tk):
    n, k = wt.shape
    return pl.pallas_call(
        _wprep_kernel,
        out_shape=jax.ShapeDtypeStruct((k, PROJ_W), BF16),
        grid=(k // tk,),
        in_specs=[pl.BlockSpec((n, tk), lambda i: (0, i))],
        out_specs=pl.BlockSpec((tk, PROJ_W), lambda i: (i, 0)),
        compiler_params=pltpu.CompilerParams(
            dimension_semantics=("arbitrary",),
            vmem_limit_bytes=VMEM_LIMIT),
        name="wprep",
    )(wt)


def _inproj_kernel(x_ref, g_ref, w_ref, pos_ref, freq_ref, *rest, tm, rc):
    n_cast = (len(rest) - 3) // 2
    o_ref, cos_ref, sin_ref = rest[n_cast:n_cast + 3]
    _cast_slabs(rest[:n_cast], rest[n_cast + 3:])

    half = SWA_HEAD_DIM // 2
    low = lax.broadcasted_iota(jnp.int32, (WINDOW, LANES), 1) < SWA_HEAD_DIM
    for b in range(tm // WINDOW):
        ang_t = freq_ref[...] * pos_ref[b].astype(F32)
        cs = jnp.concatenate([jnp.cos(ang_t)] * (LANES // half), axis=0).T
        sn = jnp.concatenate([jnp.sin(ang_t)] * (LANES // half), axis=0).T
        cos_ref[b * WINDOW:(b + 1) * WINDOW, :] = cs
        sin_ref[b * WINDOW:(b + 1) * WINDOW, :] = jnp.where(low, -sn, sn)

    for r in range(tm // rc):
        rs = pl.ds(r * rc, rc)
        u = _rms(x_ref[rs, :], g_ref[...]).astype(BF16)
        o_ref[rs, :] = jnp.dot(u, w_ref[...], preferred_element_type=F32).astype(BF16)


def _inproj(x2, gain, w, pos3, freq_col, tm, rc, to_cast):
    t = x2.shape[0]
    cast_in, cast_out, cast_shapes = _cast_specs(to_cast, t // tm, lambda i: i)
    return pl.pallas_call(
        functools.partial(_inproj_kernel, tm=tm, rc=rc),
        out_shape=[jax.ShapeDtypeStruct((t, PROJ_W), BF16), jax.ShapeDtypeStruct((t, LANES), F32),
                   jax.ShapeDtypeStruct((t, LANES), F32)] + cast_shapes,
        grid=(t // tm,),
        in_specs=[
            pl.BlockSpec((tm, D_MODEL), lambda i: (i, 0)),
            pl.BlockSpec((1, D_MODEL), lambda i: (0, 0)),
            pl.BlockSpec((D_MODEL, PROJ_W), lambda i: (0, 0), pipeline_mode=pl.Buffered(1)),
            pl.BlockSpec((tm // WINDOW, 1, WINDOW), lambda i: (i, 0, 0)),
            pl.BlockSpec((SWA_HEAD_DIM // 2, 1), lambda i: (0, 0)),
        ] + cast_in,
        out_specs=[pl.BlockSpec((tm, PROJ_W), lambda i: (i, 0)), pl.BlockSpec((tm, LANES), lambda i: (i, 0)),
                   pl.BlockSpec((tm, LANES), lambda i: (i, 0))] + cast_out,
        compiler_params=pltpu.CompilerParams(
            dimension_semantics=("arbitrary",),
            vmem_limit_bytes=VMEM_LIMIT),
        name="inproj",
    )(x2, gain, w, pos3, freq_col, *to_cast)


def _gla_kernel(q_ref, k_ref, v_ref, g_ref, a_ref, wa_ref, ba_ref, gn_ref, tril_ref, mask_ref, *rest, ts):
    c = GLA_CHUNK
    nc = ts // c
    n_cast = (len(rest) - 2) // 2
    o_ref, s_ref = rest[n_cast], rest[-1]
    _cast_slabs(rest[:n_cast], rest[n_cast + 1:-1])
    heads = range(GLA_HEADS)
    ks = [slice(h * GLA_DK, (h + 1) * GLA_DK) for h in heads]
    vs = [slice(h * GLA_DV, (h + 1) * GLA_DV) for h in heads]

    @pl.when(pl.program_id(1) == 0)
    def _():
        s_ref[...] = jnp.zeros_like(s_ref)

    z = jnp.dot(a_ref[...], wa_ref[...], preferred_element_type=F32) + ba_ref[...]
    log_a = (jnp.minimum(z, 0.0) - jnp.log(1.0 + jnp.exp(-jnp.abs(z)))) * (1.0 / GLA_GATE_NORM)

    la_hi = log_a.astype(BF16)
    la_lo = (log_a - la_hi.astype(F32)).astype(BF16)
    la_hl = jnp.concatenate([la_hi, la_lo], axis=1)
    subs = [slice(s * GLA_SUB, (s + 1) * GLA_SUB) for s in range(ts // GLA_SUB)]
    cum2 = jnp.concatenate([jnp.dot(tril_ref[...], la_hl[rs], preferred_element_type=F32) for rs in subs], axis=0)
    bcum = cum2[:, :GLA_QK_W] + cum2[:, GLA_QK_W:]
    bcum3 = bcum.reshape(nc, c, GLA_QK_W)
    b_last = bcum3[:, c - 1:c, :]

    q = q_ref[...].astype(F32) * (GLA_DK ** -0.5)
    k = k_ref[...].astype(F32)
    q_e = (q * jnp.exp(bcum)).astype(BF16)
    k_e = (k * jnp.exp(-bcum)).astype(BF16)
    k_d = (k * jnp.exp(b_last - bcum3).reshape(ts, GLA_QK_W)).astype(BF16)
    decay = jnp.exp(b_last)
    v = [v_ref[:, vs[h]] for h in heads]

    keep = mask_ref[...] != 0.0

    for h in heads:
        o_intra = []
        for rs in subs:
            scores = lax.dot_general(q_e[rs, ks[h]], k_e[rs, ks[h]], _NT, preferred_element_type=F32)
            scores = jnp.where(keep, scores, 0.0).astype(BF16)
            o_intra.append(jnp.dot(scores, v[h][rs], preferred_element_type=F32))
        o_intra = jnp.concatenate(o_intra, axis=0)

        state_t = s_ref[h]
        o_inter = []
        for n in range(nc):
            rows = slice(n * c, (n + 1) * c)
            o_inter.append(lax.dot_general(q_e[rows, ks[h]], state_t.astype(BF16), _NT,
                                           preferred_element_type=F32))
            kv_t = lax.dot_general(v[h][rows], k_d[rows, ks[h]], _TN, preferred_element_type=F32)
            state_t = state_t * decay[n][:, ks[h]] + kv_t
        s_ref[h] = state_t

        o = _rms(o_intra + jnp.concatenate(o_inter, axis=0), gn_ref[...])
        gate = g_ref[:, vs[h]].astype(F32)
        o_ref[:, vs[h]] = (o * (gate * jax.nn.sigmoid(gate))).astype(BF16)


def _gla(proj, wa, ba, gn, batch, seq, ts, to_cast):
    nt = seq // ts
    cast_in, cast_out, cast_shapes = _cast_specs(to_cast, batch * nt, lambda b, i: b * nt + i)
    kern = functools.partial(_gla_kernel, ts=ts)
    rows = lambda b, i: b * nt + i
    idx = jnp.arange(GLA_SUB)
    causal = (idx[:, None] >= idx[None, :]) & (idx[:, None] // GLA_CHUNK == idx[None, :] // GLA_CHUNK)
    return pl.pallas_call(
        kern,
        out_shape=[jax.ShapeDtypeStruct((batch * seq, GLA_V_W), BF16)] + cast_shapes,
        grid=(batch, nt),
        in_specs=[
            pl.BlockSpec((ts, GLA_QK_W), lambda b, i: (rows(b, i), COL_GQ // GLA_QK_W)),
            pl.BlockSpec((ts, GLA_QK_W), lambda b, i: (rows(b, i), COL_GK // GLA_QK_W)),
            pl.BlockSpec((ts, GLA_V_W), lambda b, i: (rows(b, i), COL_GV // GLA_V_W)),
            pl.BlockSpec((ts, GLA_V_W), lambda b, i: (rows(b, i), COL_GG // GLA_V_W)),
            pl.BlockSpec((ts, LANES), lambda b, i: (rows(b, i), COL_GA // LANES)),
            pl.BlockSpec((LANES, GLA_QK_W), lambda b, i: (0, 0)),
            pl.BlockSpec((1, GLA_QK_W), lambda b, i: (0, 0)),
            pl.BlockSpec((1, GLA_DV), lambda b, i: (0, 0)),
            pl.BlockSpec((GLA_SUB, GLA_SUB), lambda b, i: (0, 0)),
            pl.BlockSpec((GLA_SUB, GLA_SUB), lambda b, i: (0, 0)),
        ] + cast_in,
        out_specs=[pl.BlockSpec((ts, GLA_V_W), lambda b, i: (rows(b, i), 0))] + cast_out,
        scratch_shapes=[pltpu.VMEM((GLA_HEADS, GLA_DV, GLA_DK), F32)],
        compiler_params=pltpu.CompilerParams(
            dimension_semantics=("arbitrary", "arbitrary"),
            vmem_limit_bytes=VMEM_LIMIT),
        name="gla",
    )(proj, proj, proj, proj, proj, wa, ba, gn, causal.astype(BF16), causal.astype(F32), *to_cast)


def _swa_kernel(sinks_ref, q_ref, k_ref, v_ref, cos_ref, sin_ref, eye_ref, *rest):
    n_cast = (len(rest) - 3) // 2
    o_ref = rest[n_cast]
    k2p_ref, vtp_ref = rest[-2:]
    _cast_slabs(rest[:n_cast], rest[n_cast + 1:-2])
    step = pl.program_id(1)
    wb = WINDOW
    hd = SWA_HEAD_DIM
    half = hd // 2
    nblk = SWA_BLOCKS
    rows = nblk * wb
    pairs_per_kv = SWA_Q_W // LANES // SWA_KV_HEADS
    kvs = range(SWA_KV_HEADS)
    lane = lax.broadcasted_iota(jnp.int32, (rows, LANES), 1)
    head_a = (lane % hd) < half

    @pl.when(step == 0)
    def _():
        k2p_ref[...] = jnp.zeros_like(k2p_ref)
        vtp_ref[...] = jnp.zeros_like(vtp_ref)

    cos = cos_ref[...]
    sin = sin_ref[...]

    def rope(t, cs, sn):
        return t * cs + pltpu.roll(t, hd, 1) * sn

    k_r = rope(k_ref[...].astype(F32), cos, sin)
    k2_cur = [jnp.where(head_a, k_r, pltpu.roll(k_r, half, 1)).astype(BF16),
              jnp.where(head_a, pltpu.roll(k_r, LANES - half, 1), k_r).astype(BF16)]
    k2_all = [jnp.concatenate([k2p_ref[c], k2_cur[c]], axis=0) for c in kvs]
    v_t = v_ref[...].astype(F32).T
    ones_rows = (lax.broadcasted_iota(jnp.int32, (SWA_VT_ROWS - hd, rows), 0) == 0).astype(F32)
    vt_cur = [jnp.concatenate([v_t[c * hd:(c + 1) * hd], ones_rows], axis=0).astype(BF16) for c in kvs]
    vt_all = [jnp.concatenate([vtp_ref[c], vt_cur[c]], axis=1) for c in kvs]

    ki = lax.broadcasted_iota(jnp.int32, (2 * wb, wb), 0)
    qi = lax.broadcasted_iota(jnp.int32, (2 * wb, wb), 1) + wb
    band = (ki <= qi) & (qi - ki < WINDOW)
    bias = jnp.where(band, 0.0, NEG_INF).astype(BF16)
    bias_first = jnp.where(band & ((step > 0) | (ki >= wb)), 0.0, NEG_INF).astype(BF16)

    qscale = (hd ** -0.5) * LOG2E
    cos_q = cos * qscale
    sin_q = sin * qscale

    n_tiles = SWA_Q_W // LANES
    lhs = []
    for t in range(n_tiles):
        q_r = rope(q_ref[:, t * LANES:(t + 1) * LANES].astype(F32), cos_q, sin_q)
        lhs.append([jnp.where(head_a, q_r, 0.0).astype(BF16), jnp.where(head_a, 0.0, q_r).astype(BF16)])
    st = []
    for x in range(nblk):
        for c in kvs:
            rows_x = slice(x * wb, (x + 1) * wb)
            lhs_xc = jnp.concatenate([lhs[t][i][rows_x] for t in range(c * pairs_per_kv, (c + 1) * pairs_per_kv)
                                      for i in range(2)], axis=0)
            lhs_xc = jnp.concatenate([lhs_xc, eye_ref[...]], axis=1)
            keys = jnp.concatenate([k2_all[c][x * wb:(x + 2) * wb], bias_first if x == 0 else bias], axis=1)
            st.append(lax.dot_general(keys, lhs_xc, _NT, preferred_element_type=F32))
    st = jnp.concatenate(st, axis=1)
    m = jnp.max(st, axis=0, keepdims=True)
    e = jnp.exp2(st - m).astype(BF16)
    sink = jnp.concatenate([jnp.full((1, wb), sinks_ref[i], F32) for i in range(SWA_HEADS)] * nblk, axis=1) * LOG2E
    sink_term = jnp.exp2(sink - m)
    cols_per_kv = 2 * pairs_per_kv * wb
    for x in range(nblk):
        for c in kvs:
            cols = slice((x * SWA_KV_HEADS + c) * cols_per_kv, (x * SWA_KV_HEADS + c + 1) * cols_per_kv)
            ot = jnp.dot(vt_all[c][:, x * wb:(x + 2) * wb], e[:, cols], preferred_element_type=F32)
            o_n = ot[:hd] * (1.0 / (ot[hd:hd + 1] + sink_term[:, cols]))
            for n in range(pairs_per_kv):
                t = c * pairs_per_kv + n
                pair = jnp.concatenate([o_n[:, 2 * n * wb:(2 * n + 1) * wb], o_n[:, (2 * n + 1) * wb:(2 * n + 2) * wb]],
                                       axis=0)
                o_ref[x * wb:(x + 1) * wb, t * LANES:(t + 1) * LANES] = pair.T.astype(BF16)

    for c in kvs:
        k2p_ref[c] = k2_cur[c][rows - wb:]
        vtp_ref[c] = vt_cur[c][:, rows - wb:]


def _swa(proj, sinks, cos_tab, sin_tab, batch, seq, to_cast):
    rows = SWA_BLOCKS * WINDOW
    ns = seq // rows
    cast_in, cast_out, cast_shapes = _cast_specs(to_cast, batch * ns, lambda b, n: b * ns + n)
    eye = jnp.tile(jnp.eye(WINDOW, dtype=BF16), (SWA_HEADS // SWA_KV_HEADS, 1))
    tile = lambda b, n: b * ns + n
    return pl.pallas_call(
        _swa_kernel,
        out_shape=[jax.ShapeDtypeStruct((batch * seq, SWA_Q_W), BF16)] + cast_shapes,
        grid=(batch, ns),
        in_specs=[
            pl.BlockSpec(memory_space=pltpu.SMEM),
            pl.BlockSpec((rows, SWA_Q_W), lambda b, n: (tile(b, n), COL_SQ // SWA_Q_W)),
            pl.BlockSpec((rows, SWA_KV_W), lambda b, n: (tile(b, n), COL_SK // SWA_KV_W)),
            pl.BlockSpec((rows, SWA_KV_W), lambda b, n: (tile(b, n), COL_SV // SWA_KV_W)),
            pl.BlockSpec((rows, LANES), lambda b, n: (tile(b, n), 0)),
            pl.BlockSpec((rows, LANES), lambda b, n: (tile(b, n), 0)),
            pl.BlockSpec((SWA_HEADS // SWA_KV_HEADS * WINDOW, WINDOW), lambda b, n: (0, 0)),
        ] + cast_in,
        out_specs=[pl.BlockSpec((rows, SWA_Q_W), lambda b, n: (tile(b, n), 0))] + cast_out,
        scratch_shapes=[pltpu.VMEM((SWA_KV_HEADS, WINDOW, LANES), BF16),
                        pltpu.VMEM((SWA_KV_HEADS, SWA_VT_ROWS, WINDOW), BF16)],
        compiler_params=pltpu.CompilerParams(
            dimension_semantics=("arbitrary", "arbitrary"),
            vmem_limit_bytes=VMEM_LIMIT),
        name="swa",
    )(sinks, proj, proj, proj, cos_tab, sin_tab, eye, *to_cast)


def _outproj_kernel(x_ref, a_ref, b_ref, wa_ref, wb_ref, h_ref, *, tm, rc):
    for r in range(tm // rc):
        rs = pl.ds(r * rc, rc)
        acc = jnp.dot(a_ref[rs, :], wa_ref[...], preferred_element_type=F32)
        acc = acc + jnp.dot(b_ref[rs, :], wb_ref[...], preferred_element_type=F32)
        h_ref[rs, :] = x_ref[rs, :] + acc


def _outproj(x2, o_gla, o_swa, w, tm, rc):
    t = x2.shape[0]
    return pl.pallas_call(
        functools.partial(_outproj_kernel, tm=tm, rc=rc),
        out_shape=jax.ShapeDtypeStruct((t, D_MODEL), F32),
        grid=(t // tm,),
        in_specs=[
            pl.BlockSpec((tm, D_MODEL), lambda i: (i, 0)),
            pl.BlockSpec((tm, GLA_V_W), lambda i: (i, 0)),
            pl.BlockSpec((tm, SWA_Q_W), lambda i: (i, 0)),
            pl.BlockSpec((GLA_V_W, D_MODEL), lambda i: (0, 0), pipeline_mode=pl.Buffered(1)),
            pl.BlockSpec((SWA_Q_W, D_MODEL), lambda i: (GLA_V_W // SWA_Q_W, 0), pipeline_mode=pl.Buffered(1)),
        ],
        out_specs=pl.BlockSpec((tm, D_MODEL), lambda i: (i, 0)),
        compiler_params=pltpu.CompilerParams(
            dimension_semantics=("arbitrary",),
            vmem_limit_bytes=VMEM_LIMIT),
        name="outproj",
    )(x2, o_gla, o_swa, w, w)


def _ffn_kernel(h_ref, gf_ref, wg_ref, wu_ref, cw_ref, cb_ref, wd_ref, gl_ref, o_ref,
                hn_ref, carry_ref, *, tm, rc, seq):
    i = pl.program_id(0)
    j = pl.program_id(1)
    nj = pl.num_programs(1)

    def step(first, last):
        seq_start = (i * tm) % seq == 0
        prev = jnp.where(seq_start, 0.0, carry_ref[j])
        rows = lax.broadcasted_iota(jnp.int32, prev.shape, 0)
        cb = cb_ref[...]
        cw0, cw1, cw2 = cw_ref[0:1, :], cw_ref[1:2, :], cw_ref[2:3, :]

        for r in range(tm // rc):
            rs = pl.ds(r * rc, rc)
            if first:
                res = h_ref[rs, :]
                hn = _rms(res, gf_ref[...]).astype(BF16)
                hn_ref[rs, :] = hn
            else:
                res = o_ref[rs, :]
                hn = hn_ref[rs, :]
            gate = jnp.dot(hn, wg_ref[...], preferred_element_type=F32)
            up = jnp.dot(hn, wu_ref[...], preferred_element_type=F32)

            def shifted(d):
                rolled = pltpu.roll(gate, d, 0)
                top = jnp.where(rows < d, pltpu.roll(prev, d, 0), rolled[:SUBLANES])
                return jnp.concatenate([top, rolled[SUBLANES:]], axis=0)

            conv = cb + cw0 * shifted(2) + cw1 * shifted(1) + cw2 * gate
            act = (conv * jax.nn.sigmoid(conv) * up).astype(BF16)
            acc = res + jnp.dot(act, wd_ref[...], preferred_element_type=F32)
            o_ref[rs, :] = _rms(acc, gl_ref[...]) if last else acc
            prev = gate[rc - SUBLANES:, :]
        carry_ref[j] = prev

    pl.when(j == 0)(functools.partial(step, True, False))
    pl.when((j > 0) & (j < nj - 1))(functools.partial(step, False, False))
    pl.when(j == nj - 1)(functools.partial(step, False, True))


def _ffn(h, gf, wg, wu, cw, cb, wd, gl, seq, tm, tf, rc):
    t = h.shape[0]
    nj = D_FF // tf
    kern = functools.partial(_ffn_kernel, tm=tm, rc=rc, seq=seq)
    return pl.pallas_call(
        kern,
        out_shape=jax.ShapeDtypeStruct((t, D_MODEL), F32),
        grid=(t // tm, nj),
        in_specs=[
            pl.BlockSpec((tm, D_MODEL), lambda i, j: (i, 0)),
            pl.BlockSpec((1, D_MODEL), lambda i, j: (0, 0)),
            pl.BlockSpec((D_MODEL, tf), lambda i, j: (0, j)),
            pl.BlockSpec((D_MODEL, tf), lambda i, j: (0, j)),
            pl.BlockSpec((CONV_WIDTH, tf), lambda i, j: (0, j)),
            pl.BlockSpec((1, tf), lambda i, j: (0, j)),
            pl.BlockSpec((tf, D_MODEL), lambda i, j: (j, 0)),
            pl.BlockSpec((1, D_MODEL), lambda i, j: (0, 0)),
        ],
        out_specs=pl.BlockSpec((tm, D_MODEL), lambda i, j: (i, 0)),
        scratch_shapes=[pltpu.VMEM((tm, D_MODEL), BF16), pltpu.VMEM((nj, SUBLANES, tf), F32)],
        compiler_params=pltpu.CompilerParams(
            dimension_semantics=("arbitrary", "arbitrary"),
            vmem_limit_bytes=VMEM_LIMIT),
        name="convffn",
    )(h, gf, wg, wu, cw, cb, wd, gl)


def kernel(x, positions, attn_norm, w_in, w_a_up, b_a_up, gla_norm, sinks, w_out, ffn_norm,
           w_gate, w_up, conv_w, conv_b, w_down, final_norm):
    batch, seq, _ = x.shape
    t = batch * seq
    assert w_in.shape[0] == 1, "the final norm is fused into the single layer's FFN kernel"
    x2 = x.reshape(t, D_MODEL)
    pos3 = positions.reshape(t // WINDOW, 1, WINDOW)
    half = SWA_HEAD_DIM // 2
    freq_col = (ROPE_THETA ** (-jnp.arange(half, dtype=F32) / half)).reshape(half, 1)

    w_proj = _wprep(jnp.swapaxes(w_in[0], 0, 1), tk=256)
    wa = jnp.concatenate(
        [w_a_up[0], jnp.zeros((LANES - GLA_RANK, GLA_QK_W), F32)], axis=0).astype(BF16)

    proj, cos_tab, sin_tab = _inproj(x2, attn_norm[0].reshape(1, D_MODEL), w_proj, pos3, freq_col,
                                     tm=512, rc=128, to_cast=[])
    o_gla, wd_bf, wo_bf = _gla(proj, wa, b_a_up[0].reshape(1, GLA_QK_W), gla_norm[0].reshape(1, GLA_DV),
                               batch, seq, ts=1024, to_cast=[w_down[0], w_out[0]])
    o_swa, wg_bf, wu_bf = _swa(proj, sinks[0], cos_tab, sin_tab, batch, seq, to_cast=[w_gate[0], w_up[0]])
    h = _outproj(x2, o_gla, o_swa, wo_bf, tm=1024, rc=512)
    y = _ffn(h, ffn_norm[0].reshape(1, D_MODEL), wg_bf, wu_bf, conv_w[0], conv_b[0].reshape(1, D_FF), wd_bf,
             final_norm.reshape(1, D_MODEL), seq, tm=1024, tf=512, rc=512)
    return y.reshape(batch, seq, D_MODEL)
```

```python
import functools

import jax
import jax.numpy as jnp
from jax import lax
from jax.experimental import pallas as pl
from jax.experimental.pallas import tpu as pltpu

D_MODEL = 2048
GLA_HEADS = 4
GLA_DK = 128
GLA_DV = 256
GLA_RANK = 16
GLA_GATE_NORM = 16.0
GLA_CHUNK = 64
GLA_SUB = 128
SWA_HEADS = 16
SWA_KV_HEADS = 2
SWA_HEAD_DIM = 64
WINDOW = 128
ROPE_THETA = 10000.0
D_FF = 5632
CONV_WIDTH = 3
EPS = 1e-6
NEG_INF = -1e30
LOG2E = 1.4426950408889634
SWA_VT_ROWS = SWA_HEAD_DIM + 16
SWA_BLOCKS = 4

GLA_QK_W = GLA_HEADS * GLA_DK
GLA_V_W = GLA_HEADS * GLA_DV
SWA_Q_W = SWA_HEADS * SWA_HEAD_DIM
SWA_KV_W = SWA_KV_HEADS * SWA_HEAD_DIM

LANES = 128
SUBLANES = 8

COL_GQ = 0
COL_GK = COL_GQ + GLA_QK_W
COL_GV = COL_GK + GLA_QK_W
COL_GG = COL_GV + GLA_V_W
COL_SQ = COL_GG + GLA_V_W
COL_SK = COL_SQ + SWA_Q_W
COL_SV = COL_SK + SWA_KV_W
COL_GA = COL_SV + SWA_KV_W
PROJ_TN = 1536
PROJ_W = 3 * PROJ_TN

VMEM_LIMIT = 58 * 1024 * 1024

F32 = jnp.float32
BF16 = jnp.bfloat16

_NT = (((1,), (1,)), ((), ()))
_TN = (((0,), (0,)), ((), ()))


def _rms(x, gain):
    return x * lax.rsqrt(jnp.mean(x * x, axis=-1, keepdims=True) + EPS) * gain


def _cast_specs(weights, steps, step_index):
    in_specs, out_specs, out_shapes = [], [], []
    for w in weights:
        rows, cols = w.shape
        slab = rows // steps
        assert slab * steps == rows and slab % (2 * SUBLANES) == 0, (w.shape, steps)
        spec = pl.BlockSpec((slab, cols), lambda *g: (step_index(*g), 0))
        in_specs.append(spec)
        out_specs.append(spec)
        out_shapes.append(jax.ShapeDtypeStruct(w.shape, BF16))
    return in_specs, out_specs, out_shapes


def _cast_slabs(src_refs, dst_refs):
    for src, dst in zip(src_refs, dst_refs):
        dst[...] = src[...].astype(BF16)


def _wprep_kernel(wt_ref, o_ref):
    tk = wt_ref.shape[1]
    half = SWA_HEAD_DIM // 2
    o_ga = 2 * GLA_QK_W + 2 * GLA_V_W
    o_sq = o_ga + GLA_RANK

    def put(col, rows):
        o_ref[:, col:col + rows.shape[0]] = rows.T.astype(BF16)

    put(0, wt_ref[:o_ga, :])
    for i in range((SWA_Q_W + SWA_KV_W) // LANES):
        r0 = o_sq + i * LANES
        blk = [wt_ref[r0 + j * half:r0 + (j + 1) * half, :] for j in range(4)]
        put(COL_SQ + i * LANES, jnp.concatenate([blk[0], blk[2], blk[1], blk[3]], axis=0))
    put(COL_SV, wt_ref[o_sq + SWA_Q_W + SWA_KV_W:, :])
    keep = lax.broadcasted_iota(jnp.int32, (LANES, tk), 0) < GLA_RANK
    put(COL_GA, jnp.where(keep, wt_ref[o_ga:o_ga + LANES, :], 0.0))
    o_ref[:, COL_GA + LANES:] = jnp.zeros((tk, PROJ_W - COL_GA - LANES), BF16)


def _wprep(wt, tk):
    n, k = wt.shape
    return pl.pallas_call(
        _wprep_kernel,
        out_shape=jax.ShapeDtypeStruct((k, PROJ_W), BF16),
        grid=(k // tk,),
        in_specs=[pl.BlockSpec((n, tk), lambda i: (0, i))],
        out_specs=pl.BlockSpec((tk, PROJ_W), lambda i: (i, 0)),
        compiler_params=pltpu.CompilerParams(
            dimension_semantics=("arbitrary",),
            vmem_limit_bytes=VMEM_LIMIT),
        name="wprep",
    )(wt)


def _inproj_kernel(x_ref, g_ref, w_ref, pos_ref, freq_ref, *rest, tm, rc):
    n_cast = (len(rest) - 3) // 2
    o_ref, cos_ref, sin_ref = rest[n_cast:n_cast + 3]
    _cast_slabs(rest[:n_cast], rest[n_cast + 3:])

    half = SWA_HEAD_DIM // 2
    low = lax.broadcasted_iota(jnp.int32, (WINDOW, LANES), 1) < SWA_HEAD_DIM
    for b in range(tm // WINDOW):
        ang_t = freq_ref[...] * pos_ref[b].astype(F32)
        cs = jnp.concatenate([jnp.cos(ang_t)] * (LANES // half), axis=0).T
        sn = jnp.concatenate([jnp.sin(ang_t)] * (LANES // half), axis=0).T
        cos_ref[b * WINDOW:(b + 1) * WINDOW, :] = cs
        sin_ref[b * WINDOW:(b + 1) * WINDOW, :] = jnp.where(low, -sn, sn)

    for r in range(tm // rc):
        rs = pl.ds(r * rc, rc)
        u = _rms(x_ref[rs, :], g_ref[...]).astype(BF16)
        o_ref[rs, :] = jnp.dot(u, w_ref[...], preferred_element_type=F32).astype(BF16)


def _inproj(x2, gain, w, pos3, freq_col, tm, rc, to_cast):
    t = x2.shape[0]
    cast_in, cast_out, cast_shapes = _cast_specs(to_cast, t // tm, lambda i: i)
    return pl.pallas_call(
        functools.partial(_inproj_kernel, tm=tm, rc=rc),
        out_shape=[jax.ShapeDtypeStruct((t, PROJ_W), BF16), jax.ShapeDtypeStruct((t, LANES), F32),
                   jax.ShapeDtypeStruct((t, LANES), F32)] + cast_shapes,
        grid=(t // tm,),
        in_specs=[
            pl.BlockSpec((tm, D_MODEL), lambda i: (i, 0)),
            pl.BlockSpec((1, D_MODEL), lambda i: (0, 0)),
            pl.BlockSpec((D_MODEL, PROJ_W), lambda i: (0, 0), pipeline_mode=pl.Buffered(1)),
            pl.BlockSpec((tm // WINDOW, 1, WINDOW), lambda i: (i, 0, 0)),
            pl.BlockSpec((SWA_HEAD_DIM // 2, 1), lambda i: (0, 0)),
        ] + cast_in,
        out_specs=[pl.BlockSpec((tm, PROJ_W), lambda i: (i, 0)), pl.BlockSpec((tm, LANES), lambda i: (i, 0)),
                   pl.BlockSpec((tm, LANES), lambda i: (i, 0))] + cast_out,
        compiler_params=pltpu.CompilerParams(
            dimension_semantics=("arbitrary",),
            vmem_limit_bytes=VMEM_LIMIT),
        name="inproj",
    )(x2, gain, w, pos3, freq_col, *to_cast)


def _gla_kernel(q_ref, k_ref, v_ref, g_ref, a_ref, wa_ref, ba_ref, gn_ref, tril_ref, mask_ref, *rest, ts):
    c = GLA_CHUNK
    nc = ts // c
    n_cast = (len(rest) - 2) // 2
    o_ref, s_ref = rest[n_cast], rest[-1]
    _cast_slabs(rest[:n_cast], rest[n_cast + 1:-1])
    heads = range(GLA_HEADS)
    ks = [slice(h * GLA_DK, (h + 1) * GLA_DK) for h in heads]
    vs = [slice(h * GLA_DV, (h + 1) * GLA_DV) for h in heads]

    @pl.when(pl.program_id(1) == 0)
    def _():
        s_ref[...] = jnp.zeros_like(s_ref)

    z = jnp.dot(a_ref[...], wa_ref[...], preferred_element_type=F32) + ba_ref[...]
    log_a = (jnp.minimum(z, 0.0) - jnp.log(1.0 + jnp.exp(-jnp.abs(z)))) * (1.0 / GLA_GATE_NORM)

    la_hi = log_a.astype(BF16)
    la_lo = (log_a - la_hi.astype(F32)).astype(BF16)
    la_hl = jnp.concatenate([la_hi, la_lo], axis=1)
    subs = [slice(s * GLA_SUB, (s + 1) * GLA_SUB) for s in range(ts // GLA_SUB)]
    cum2 = jnp.concatenate([jnp.dot(tril_ref[...], la_hl[rs], preferred_element_type=F32) for rs in subs], axis=0)
    bcum = cum2[:, :GLA_QK_W] + cum2[:, GLA_QK_W:]
    bcum3 = bcum.reshape(nc, c, GLA_QK_W)
    b_last = bcum3[:, c - 1:c, :]

    q = q_ref[...].astype(F32) * (GLA_DK ** -0.5)
    k = k_ref[...].astype(F32)
    q_e = (q * jnp.exp(bcum)).astype(BF16)
    k_e = (k * jnp.exp(-bcum)).astype(BF16)
    k_d = (k * jnp.exp(b_last - bcum3).reshape(ts, GLA_QK_W)).astype(BF16)
    decay = jnp.exp(b_last)
    v = [v_ref[:, vs[h]] for h in heads]

    keep = mask_ref[...] != 0.0

    for h in heads:
        o_intra = []
        for rs in subs:
            scores = lax.dot_general(q_e[rs, ks[h]], k_e[rs, ks[h]], _NT, preferred_element_type=F32)
            scores = jnp.where(keep, scores, 0.0).astype(BF16)
            o_intra.append(jnp.dot(scores, v[h][rs], preferred_element_type=F32))
        o_intra = jnp.concatenate(o_intra, axis=0)

        state_t = s_ref[h]
        o_inter = []
        for n in range(nc):
            rows = slice(n * c, (n + 1) * c)
            o_inter.append(lax.dot_general(q_e[rows, ks[h]], state_t.astype(BF16), _NT,
                                           preferred_element_type=F32))
            kv_t = lax.dot_general(v[h][rows], k_d[rows, ks[h]], _TN, preferred_element_type=F32)
            state_t = state_t * decay[n][:, ks[h]] + kv_t
        s_ref[h] = state_t

        o = _rms(o_intra + jnp.concatenate(o_inter, axis=0), gn_ref[...])
        gate = g_ref[:, vs[h]].astype(F32)
        o_ref[:, vs[h]] = (o * (gate * jax.nn.sigmoid(gate))).astype(BF16)


def _gla(proj, wa, ba, gn, batch, seq, ts, to_cast):
    nt = seq // ts
    cast_in, cast_out, cast_shapes = _cast_specs(to_cast, batch * nt, lambda b, i: b * nt + i)
    kern = functools.partial(_gla_kernel, ts=ts)
    rows = lambda b, i: b * nt + i
    idx = jnp.arange(GLA_SUB)
    causal = (idx[:, None] >= idx[None, :]) & (idx[:, None] // GLA_CHUNK == idx[None, :] // GLA_CHUNK)
    return pl.pallas_call(
        kern,
        out_shape=[jax.ShapeDtypeStruct((batch * seq, GLA_V_W), BF16)] + cast_shapes,
        grid=(batch, nt),
        in_specs=[
            pl.BlockSpec((ts, GLA_QK_W), lambda b, i: (rows(b, i), COL_GQ // GLA_QK_W)),
            pl.BlockSpec((ts, GLA_QK_W), lambda b, i: (rows(b, i), COL_GK // GLA_QK_W)),
            pl.BlockSpec((ts, GLA_V_W), lambda b, i: (rows(b, i), COL_GV // GLA_V_W)),
            pl.BlockSpec((ts, GLA_V_W), lambda b, i: (rows(b, i), COL_GG // GLA_V_W)),
            pl.BlockSpec((ts, LANES), lambda b, i: (rows(b, i), COL_GA // LANES)),
            pl.BlockSpec((LANES, GLA_QK_W), lambda b, i: (0, 0)),
            pl.BlockSpec((1, GLA_QK_W), lambda b, i: (0, 0)),
            pl.BlockSpec((1, GLA_DV), lambda b, i: (0, 0)),
            pl.BlockSpec((GLA_SUB, GLA_SUB), lambda b, i: (0, 0)),
            pl.BlockSpec((GLA_SUB, GLA_SUB), lambda b, i: (0, 0)),
        ] + cast_in,
        out_specs=[pl.BlockSpec((ts, GLA_V_W), lambda b, i: (rows(b, i), 0))] + cast_out,
        scratch_shapes=[pltpu.VMEM((GLA_HEADS, GLA_DV, GLA_DK), F32)],
        compiler_params=pltpu.CompilerParams(
            dimension_semantics=("arbitrary", "arbitrary"),
            vmem_limit_bytes=VMEM_LIMIT),
        name="gla",
    )(proj, proj, proj, proj, proj, wa, ba, gn, causal.astype(BF16), causal.astype(F32), *to_cast)


def _swa_kernel(sinks_ref, q_ref, k_ref, v_ref, cos_ref, sin_ref, eye_ref, o_ref, k2p_ref, vtp_ref):
    step = pl.program_id(1)
    wb = WINDOW
    hd = SWA_HEAD_DIM
    half = hd // 2
    nblk = SWA_BLOCKS
    rows = nblk * wb
    pairs_per_kv = SWA_Q_W // LANES // SWA_KV_HEADS
    kvs = range(SWA_KV_HEADS)
    lane = lax.broadcasted_iota(jnp.int32, (rows, LANES), 1)
    head_a = (lane % hd) < half

    @pl.when(step == 0)
    def _():
        k2p_ref[...] = jnp.zeros_like(k2p_ref)
        vtp_ref[...] = jnp.zeros_like(vtp_ref)

    cos = cos_ref[...]
    sin = sin_ref[...]

    def rope(t, cs, sn):
        return t * cs + pltpu.roll(t, hd, 1) * sn

    k_r = rope(k_ref[...].astype(F32), cos, sin)
    k2_cur = [jnp.where(head_a, k_r, pltpu.roll(k_r, half, 1)).astype(BF16),
              jnp.where(head_a, pltpu.roll(k_r, LANES - half, 1), k_r).astype(BF16)]
    k2_all = [jnp.concatenate([k2p_ref[c], k2_cur[c]], axis=0) for c in kvs]
    v_t = v_ref[...].astype(F32).T
    ones_rows = (lax.broadcasted_iota(jnp.int32, (SWA_VT_ROWS - hd, rows), 0) == 0).astype(F32)
    vt_cur = [jnp.concatenate([v_t[c * hd:(c + 1) * hd], ones_rows], axis=0).astype(BF16) for c in kvs]
    vt_all = [jnp.concatenate([vtp_ref[c], vt_cur[c]], axis=1) for c in kvs]

    ki = lax.broadcasted_iota(jnp.int32, (2 * wb, wb), 0)
    qi = lax.broadcasted_iota(jnp.int32, (2 * wb, wb), 1) + wb
    band = (ki <= qi) & (qi - ki < WINDOW)
    bias = jnp.where(band, 0.0, NEG_INF).astype(BF16)
    bias_first = jnp.where(band & ((step > 0) | (ki >= wb)), 0.0, NEG_INF).astype(BF16)

    qscale = (hd ** -0.5) * LOG2E
    cos_q = cos * qscale
    sin_q = sin * qscale

    n_tiles = SWA_Q_W // LANES
    lhs = []
    for t in range(n_tiles):
        q_r = rope(q_ref[:, t * LANES:(t + 1) * LANES].astype(F32), cos_q, sin_q)
        lhs.append([jnp.where(head_a, q_r, 0.0).astype(BF16), jnp.where(head_a, 0.0, q_r).astype(BF16)])
    st = []
    for x in range(nblk):
        for c in kvs:
            rows_x = slice(x * wb, (x + 1) * wb)
            lhs_xc = jnp.concatenate([lhs[t][i][rows_x] for t in range(c * pairs_per_kv, (c + 1) * pairs_per_kv)
                                      for i in range(2)], axis=0)
            lhs_xc = jnp.concatenate([lhs_xc, eye_ref[...]], axis=1)
            keys = jnp.concatenate([k2_all[c][x * wb:(x + 2) * wb], bias_first if x == 0 else bias], axis=1)
            st.append(lax.dot_general(keys, lhs_xc, _NT, preferred_element_type=F32))
    st = jnp.concatenate(st, axis=1)
    m = jnp.max(st, axis=0, keepdims=True)
    e = jnp.exp2(st - m).astype(BF16)
    sink = jnp.concatenate([jnp.full((1, wb), sinks_ref[i], F32) for i in range(SWA_HEADS)] * nblk, axis=1) * LOG2E
    sink_term = jnp.exp2(sink - m)
    cols_per_kv = 2 * pairs_per_kv * wb
    for x in range(nblk):
        for c in kvs:
            cols = slice((x * SWA_KV_HEADS + c) * cols_per_kv, (x * SWA_KV_HEADS + c + 1) * cols_per_kv)
            ot = jnp.dot(vt_all[c][:, x * wb:(x + 2) * wb], e[:, cols], preferred_element_type=F32)
            o_n = ot[:hd] * (1.0 / (ot[hd:hd + 1] + sink_term[:, cols]))
            for n in range(pairs_per_kv):
                t = c * pairs_per_kv + n
                pair = jnp.concatenate([o_n[:, 2 * n * wb:(2 * n + 1) * wb], o_n[:, (2 * n + 1) * wb:(2 * n + 2) * wb]],
                                       axis=0)
                o_ref[x * wb:(x + 1) * wb, t * LANES:(t + 1) * LANES] = pair.T.astype(BF16)

    for c in kvs:
        k2p_ref[c] = k2_cur[c][rows - wb:]
        vtp_ref[c] = vt_cur[c][:, rows - wb:]


def _swa(proj, sinks, cos_tab, sin_tab, batch, seq):
    rows = SWA_BLOCKS * WINDOW
    ns = seq // rows
    eye = jnp.tile(jnp.eye(WINDOW, dtype=BF16), (SWA_HEADS // SWA_KV_HEADS, 1))
    tile = lambda b, n: b * ns + n
    return pl.pallas_call(
        _swa_kernel,
        out_shape=jax.ShapeDtypeStruct((batch * seq, SWA_Q_W), BF16),
        grid=(batch, ns),
        in_specs=[
            pl.BlockSpec(memory_space=pltpu.SMEM),
            pl.BlockSpec((rows, SWA_Q_W), lambda b, n: (tile(b, n), COL_SQ // SWA_Q_W)),
            pl.BlockSpec((rows, SWA_KV_W), lambda b, n: (tile(b, n), COL_SK // SWA_KV_W)),
            pl.BlockSpec((rows, SWA_KV_W), lambda b, n: (tile(b, n), COL_SV // SWA_KV_W)),
            pl.BlockSpec((rows, LANES), lambda b, n: (tile(b, n), 0)),
            pl.BlockSpec((rows, LANES), lambda b, n: (tile(b, n), 0)),
            pl.BlockSpec((SWA_HEADS // SWA_KV_HEADS * WINDOW, WINDOW), lambda b, n: (0, 0)),
        ],
        out_specs=pl.BlockSpec((rows, SWA_Q_W), lambda b, n: (tile(b, n), 0)),
        scratch_shapes=[pltpu.VMEM((SWA_KV_HEADS, WINDOW, LANES), BF16),
                        pltpu.VMEM((SWA_KV_HEADS, SWA_VT_ROWS, WINDOW), BF16)],
        compiler_params=pltpu.CompilerParams(
            dimension_semantics=("arbitrary", "arbitrary"),
            vmem_limit_bytes=VMEM_LIMIT),
        name="swa",
    )(sinks, proj, proj, proj, cos_tab, sin_tab, eye)


def _outproj_kernel(x_ref, a_ref, b_ref, wa_ref, wb_ref, h_ref, *, tm, rc):
    for r in range(tm // rc):
        rs = pl.ds(r * rc, rc)
        acc = jnp.dot(a_ref[rs, :], wa_ref[...], preferred_element_type=F32)
        acc = acc + jnp.dot(b_ref[rs, :], wb_ref[...], preferred_element_type=F32)
        h_ref[rs, :] = x_ref[rs, :] + acc


def _outproj(x2, o_gla, o_swa, w, tm, rc):
    t = x2.shape[0]
    return pl.pallas_call(
        functools.partial(_outproj_kernel, tm=tm, rc=rc),
        out_shape=jax.ShapeDtypeStruct((t, D_MODEL), F32),
        grid=(t // tm,),
        in_specs=[
            pl.BlockSpec((tm, D_MODEL), lambda i: (i, 0)),
            pl.BlockSpec((tm, GLA_V_W), lambda i: (i, 0)),
            pl.BlockSpec((tm, SWA_Q_W), lambda i: (i, 0)),
            pl.BlockSpec((GLA_V_W, D_MODEL), lambda i: (0, 0), pipeline_mode=pl.Buffered(1)),
            pl.BlockSpec((SWA_Q_W, D_MODEL), lambda i: (GLA_V_W // SWA_Q_W, 0), pipeline_mode=pl.Buffered(1)),
        ],
        out_specs=pl.BlockSpec((tm, D_MODEL), lambda i: (i, 0)),
        compiler_params=pltpu.CompilerParams(
            dimension_semantics=("arbitrary",),
            vmem_limit_bytes=VMEM_LIMIT),
        name="outproj",
    )(x2, o_gla, o_swa, w, w)


def _ffn_kernel(h_ref, gf_ref, wg_ref, wu_ref, cw_ref, cb_ref, wd_ref, gl_ref, o_ref,
                hn_ref, carry_ref, *, tm, rc, seq):
    i = pl.program_id(0)
    j = pl.program_id(1)
    nj = pl.num_programs(1)

    def step(first, last):
        seq_start = (i * tm) % seq == 0
        prev = jnp.where(seq_start, 0.0, carry_ref[j])
        rows = lax.broadcasted_iota(jnp.int32, prev.shape, 0)
        cb = cb_ref[...]
        cw0, cw1, cw2 = cw_ref[0:1, :], cw_ref[1:2, :], cw_ref[2:3, :]

        for r in range(tm // rc):
            rs = pl.ds(r * rc, rc)
            if first:
                res = h_ref[rs, :]
                hn = _rms(res, gf_ref[...]).astype(BF16)
                hn_ref[rs, :] = hn
            else:
                res = o_ref[rs, :]
                hn = hn_ref[rs, :]
            gate = jnp.dot(hn, wg_ref[...], preferred_element_type=F32)
            up = jnp.dot(hn, wu_ref[...], preferred_element_type=F32)

            def shifted(d):
                rolled = pltpu.roll(gate, d, 0)
                top = jnp.where(rows < d, pltpu.roll(prev, d, 0), rolled[:SUBLANES])
                return jnp.concatenate([top, rolled[SUBLANES:]], axis=0)

            conv = cb + cw0 * shifted(2) + cw1 * shifted(1) + cw2 * gate
            act = (conv * jax.nn.sigmoid(conv) * up).astype(BF16)
            acc = res + jnp.dot(act, wd_ref[...], preferred_element_type=F32)
            o_ref[rs, :] = _rms(acc, gl_ref[...]) if last else acc
            prev = gate[rc - SUBLANES:, :]
        carry_ref[j] = prev

    pl.when(j == 0)(functools.partial(step, True, False))
    pl.when((j > 0) & (j < nj - 1))(functools.partial(step, False, False))
    pl.when(j == nj - 1)(functools.partial(step, False, True))


def _ffn(h, gf, wg, wu, cw, cb, wd, gl, seq, tm, tf, rc):
    t = h.shape[0]
    nj = D_FF // tf
    kern = functools.partial(_ffn_kernel, tm=tm, rc=rc, seq=seq)
    return pl.pallas_call(
        kern,
        out_shape=jax.ShapeDtypeStruct((t, D_MODEL), F32),
        grid=(t // tm, nj),
        in_specs=[
            pl.BlockSpec((tm, D_MODEL), lambda i, j: (i, 0)),
            pl.BlockSpec((1, D_MODEL), lambda i, j: (0, 0)),
            pl.BlockSpec((D_MODEL, tf), lambda i, j: (0, j)),
            pl.BlockSpec((D_MODEL, tf), lambda i, j: (0, j)),
            pl.BlockSpec((CONV_WIDTH, tf), lambda i, j: (0, j)),
            pl.BlockSpec((1, tf), lambda i, j: (0, j)),
            pl.BlockSpec((tf, D_MODEL), lambda i, j: (j, 0)),
            pl.BlockSpec((1, D_MODEL), lambda i, j: (0, 0)),
        ],
        out_specs=pl.BlockSpec((tm, D_MODEL), lambda i, j: (i, 0)),
        scratch_shapes=[pltpu.VMEM((tm, D_MODEL), BF16), pltpu.VMEM((nj, SUBLANES, tf), F32)],
        compiler_params=pltpu.CompilerParams(
            dimension_semantics=("arbitrary", "arbitrary"),
            vmem_limit_bytes=VMEM_LIMIT),
        name="convffn",
    )(h, gf, wg, wu, cw, cb, wd, gl)


def kernel(x, positions, attn_norm, w_in, w_a_up, b_a_up, gla_norm, sinks, w_out, ffn_norm,
           w_gate, w_up, conv_w, conv_b, w_down, final_norm):
    batch, seq, _ = x.shape
    t = batch * seq
    assert w_in.shape[0] == 1, "the final norm is fused into the single layer's FFN kernel"
    x2 = x.reshape(t, D_MODEL)
    pos3 = positions.reshape(t // WINDOW, 1, WINDOW)
    half = SWA_HEAD_DIM // 2
    freq_col = (ROPE_THETA ** (-jnp.arange(half, dtype=F32) / half)).reshape(half, 1)

    w_proj = _wprep(jnp.swapaxes(w_in[0], 0, 1), tk=512)
    wa = jnp.concatenate(
        [w_a_up[0], jnp.zeros((LANES - GLA_RANK, GLA_QK_W), F32)], axis=0).astype(BF16)

    proj, cos_tab, sin_tab, wg_bf, wu_bf = _inproj(x2, attn_norm[0].reshape(1, D_MODEL), w_proj, pos3, freq_col,
                                                   tm=512, rc=128, to_cast=[w_gate[0], w_up[0]])
    o_gla, wd_bf, wo_bf = _gla(proj, wa, b_a_up[0].reshape(1, GLA_QK_W), gla_norm[0].reshape(1, GLA_DV),
                               batch, seq, ts=1024, to_cast=[w_down[0], w_out[0]])
    o_swa = _swa(proj, sinks[0], cos_tab, sin_tab, batch, seq)
    h = _outproj(x2, o_gla, o_swa, wo_bf, tm=1024, rc=512)
    y = _ffn(h, ffn_norm[0].reshape(1, D_MODEL), wg_bf, wu_bf, conv_w[0], conv_b[0].reshape(1, D_FF), wd_bf,
             final_norm.reshape(1, D_MODEL), seq, tm=1024, tf=512, rc=512)
    return y.reshape(batch, seq, D_MODEL)
```

```python
import functools

import jax
import jax.numpy as jnp
from jax import lax
from jax.experimental import pallas as pl
from jax.experimental.pallas import tpu as pltpu

D_MODEL = 2048
GLA_HEADS = 4
GLA_DK = 128
GLA_DV = 256
GLA_RANK = 16
GLA_GATE_NORM = 16.0
GLA_CHUNK = 64
GLA_OUT_RC = 512
GLA_SUB = 128
SWA_HEADS = 16
SWA_KV_HEADS = 2
SWA_HEAD_DIM = 64
WINDOW = 128
ROPE_THETA = 10000.0
D_FF = 5632
CONV_WIDTH = 3
EPS = 1e-6
NEG_INF = -1e30
LOG2E = 1.4426950408889634
SWA_VT_ROWS = SWA_HEAD_DIM + 16
SWA_BLOCKS = 4

GLA_QK_W = GLA_HEADS * GLA_DK
GLA_V_W = GLA_HEADS * GLA_DV
SWA_Q_W = SWA_HEADS * SWA_HEAD_DIM
SWA_KV_W = SWA_KV_HEADS * SWA_HEAD_DIM

LANES = 128
SUBLANES = 8

COL_GQ = 0
COL_GK = COL_GQ + GLA_QK_W
COL_GV = COL_GK + GLA_QK_W
COL_GG = COL_GV + GLA_V_W
COL_SQ = COL_GG + GLA_V_W
COL_SK = COL_SQ + SWA_Q_W
COL_SV = COL_SK + SWA_KV_W
COL_GA = COL_SV + SWA_KV_W
PROJ_TN = 1536
PROJ_W = 3 * PROJ_TN

VMEM_LIMIT = 58 * 1024 * 1024

F32 = jnp.float32
BF16 = jnp.bfloat16

_NT = (((1,), (1,)), ((), ()))
_TN = (((0,), (0,)), ((), ()))


def _rms(x, gain):
    return x * lax.rsqrt(jnp.mean(x * x, axis=-1, keepdims=True) + EPS) * gain


def _cast_specs(weights, steps, step_index):
    in_specs, out_specs, out_shapes = [], [], []
    for w in weights:
        rows, cols = w.shape
        slab = rows // steps
        assert slab * steps == rows and slab % (2 * SUBLANES) == 0, (w.shape, steps)
        spec = pl.BlockSpec((slab, cols), lambda *g: (step_index(*g), 0))
        in_specs.append(spec)
        out_specs.append(spec)
        out_shapes.append(jax.ShapeDtypeStruct(w.shape, BF16))
    return in_specs, out_specs, out_shapes


def _cast_slabs(src_refs, dst_refs):
    for src, dst in zip(src_refs, dst_refs):
        dst[...] = src[...].astype(BF16)


def _wprep_kernel(wt_ref, o_ref):
    tk = wt_ref.shape[1]
    half = SWA_HEAD_DIM // 2
    o_ga = 2 * GLA_QK_W + 2 * GLA_V_W
    o_sq = o_ga + GLA_RANK

    def put(col, rows):
        o_ref[:, col:col + rows.shape[0]] = rows.T.astype(BF16)

    put(0, wt_ref[:o_ga, :])
    for i in range((SWA_Q_W + SWA_KV_W) // LANES):
        r0 = o_sq + i * LANES
        blk = [wt_ref[r0 + j * half:r0 + (j + 1) * half, :] for j in range(4)]
        put(COL_SQ + i * LANES, jnp.concatenate([blk[0], blk[2], blk[1], blk[3]], axis=0))
    put(COL_SV, wt_ref[o_sq + SWA_Q_W + SWA_KV_W:, :])
    keep = lax.broadcasted_iota(jnp.int32, (LANES, tk), 0) < GLA_RANK
    put(COL_GA, jnp.where(keep, wt_ref[o_ga:o_ga + LANES, :], 0.0))
    o_ref[:, COL_GA + LANES:] = jnp.zeros((tk, PROJ_W - COL_GA - LANES), BF16)


def _wprep(wt, tk):
    n, k = wt.shape
    return pl.pallas_call(
        _wprep_kernel,
        out_shape=jax.ShapeDtypeStruct((k, PROJ_W), BF16),
        grid=(k // tk,),
        in_specs=[pl.BlockSpec((n, tk), lambda i: (0, i))],
        out_specs=pl.BlockSpec((tk, PROJ_W), lambda i: (i, 0)),
        compiler_params=pltpu.CompilerParams(
            dimension_semantics=("arbitrary",),
            vmem_limit_bytes=VMEM_LIMIT),
        name="wprep",
    )(wt)


def _inproj_kernel(x_ref, g_ref, w_ref, pos_ref, freq_ref, *rest, tm, rc):
    n_cast = (len(rest) - 3) // 2
    o_ref, cos_ref, sin_ref = rest[n_cast:n_cast + 3]
    _cast_slabs(rest[:n_cast], rest[n_cast + 3:])

    half = SWA_HEAD_DIM // 2
    low = lax.broadcasted_iota(jnp.int32, (WINDOW, LANES), 1) < SWA_HEAD_DIM
    for b in range(tm // WINDOW):
        ang_t = freq_ref[...] * pos_ref[b].astype(F32)
        cs = jnp.concatenate([jnp.cos(ang_t)] * (LANES // half), axis=0).T
        sn = jnp.concatenate([jnp.sin(ang_t)] * (LANES // half), axis=0).T
        cos_ref[b * WINDOW:(b + 1) * WINDOW, :] = cs
        sin_ref[b * WINDOW:(b + 1) * WINDOW, :] = jnp.where(low, -sn, sn)

    for r in range(tm // rc):
        rs = pl.ds(r * rc, rc)
        u = _rms(x_ref[rs, :], g_ref[...]).astype(BF16)
        o_ref[rs, :] = jnp.dot(u, w_ref[...], preferred_element_type=F32).astype(BF16)


def _inproj(x2, gain, w, pos3, freq_col, tm, rc, to_cast):
    t = x2.shape[0]
    cast_in, cast_out, cast_shapes = _cast_specs(to_cast, t // tm, lambda i: i)
    return pl.pallas_call(
        functools.partial(_inproj_kernel, tm=tm, rc=rc),
        out_shape=[jax.ShapeDtypeStruct((t, PROJ_W), BF16), jax.ShapeDtypeStruct((t, LANES), F32),
                   jax.ShapeDtypeStruct((t, LANES), F32)] + cast_shapes,
        grid=(t // tm,),
        in_specs=[
            pl.BlockSpec((tm, D_MODEL), lambda i: (i, 0)),
            pl.BlockSpec((1, D_MODEL), lambda i: (0, 0)),
            pl.BlockSpec((D_MODEL, PROJ_W), lambda i: (0, 0), pipeline_mode=pl.Buffered(1)),
            pl.BlockSpec((tm // WINDOW, 1, WINDOW), lambda i: (i, 0, 0)),
            pl.BlockSpec((SWA_HEAD_DIM // 2, 1), lambda i: (0, 0)),
        ] + cast_in,
        out_specs=[pl.BlockSpec((tm, PROJ_W), lambda i: (i, 0)), pl.BlockSpec((tm, LANES), lambda i: (i, 0)),
                   pl.BlockSpec((tm, LANES), lambda i: (i, 0))] + cast_out,
        compiler_params=pltpu.CompilerParams(
            dimension_semantics=("arbitrary",),
            vmem_limit_bytes=VMEM_LIMIT),
        name="inproj",
    )(x2, gain, w, pos3, freq_col, *to_cast)


def _gla_kernel(q_ref, k_ref, v_ref, g_ref, a_ref, wa_ref, ba_ref, gn_ref, tril_ref, mask_ref,
                x_ref, osw_ref, wo_ref, *rest, ts):
    c = GLA_CHUNK
    nc = ts // c
    n_cast = (len(rest) - 3) // 2
    h_ref, s_ref, og_ref = rest[n_cast], rest[-2], rest[-1]
    _cast_slabs(rest[:n_cast], rest[n_cast + 1:-2])
    heads = range(GLA_HEADS)
    ks = [slice(h * GLA_DK, (h + 1) * GLA_DK) for h in heads]
    vs = [slice(h * GLA_DV, (h + 1) * GLA_DV) for h in heads]

    @pl.when(pl.program_id(1) == 0)
    def _():
        s_ref[...] = jnp.zeros_like(s_ref)

    z = jnp.dot(a_ref[...], wa_ref[...], preferred_element_type=F32) + ba_ref[...]
    log_a = (jnp.minimum(z, 0.0) - jnp.log(1.0 + jnp.exp(-jnp.abs(z)))) * (1.0 / GLA_GATE_NORM)

    la_hi = log_a.astype(BF16)
    la_lo = (log_a - la_hi.astype(F32)).astype(BF16)
    la_hl = jnp.concatenate([la_hi, la_lo], axis=1)
    subs = [slice(s * GLA_SUB, (s + 1) * GLA_SUB) for s in range(ts // GLA_SUB)]
    cum2 = jnp.concatenate([jnp.dot(tril_ref[...], la_hl[rs], preferred_element_type=F32) for rs in subs], axis=0)
    bcum = cum2[:, :GLA_QK_W] + cum2[:, GLA_QK_W:]
    bcum3 = bcum.reshape(nc, c, GLA_QK_W)
    b_last = bcum3[:, c - 1:c, :]

    q = q_ref[...].astype(F32) * (GLA_DK ** -0.5)
    k = k_ref[...].astype(F32)
    q_e = (q * jnp.exp(bcum)).astype(BF16)
    k_e = (k * jnp.exp(-bcum)).astype(BF16)
    k_d = (k * jnp.exp(b_last - bcum3).reshape(ts, GLA_QK_W)).astype(BF16)
    decay = jnp.exp(b_last)
    v = [v_ref[:, vs[h]] for h in heads]

    keep = mask_ref[...] != 0.0

    for h in heads:
        o_intra = []
        for rs in subs:
            scores = lax.dot_general(q_e[rs, ks[h]], k_e[rs, ks[h]], _NT, preferred_element_type=F32)
            scores = jnp.where(keep, scores, 0.0).astype(BF16)
            o_intra.append(jnp.dot(scores, v[h][rs], preferred_element_type=F32))
        o_intra = jnp.concatenate(o_intra, axis=0)

        state_t = s_ref[h]
        o_inter = []
        for n in range(nc):
            rows = slice(n * c, (n + 1) * c)
            o_inter.append(lax.dot_general(q_e[rows, ks[h]], state_t.astype(BF16), _NT,
                                           preferred_element_type=F32))
            kv_t = lax.dot_general(v[h][rows], k_d[rows, ks[h]], _TN, preferred_element_type=F32)
            state_t = state_t * decay[n][:, ks[h]] + kv_t
        s_ref[h] = state_t

        o = _rms(o_intra + jnp.concatenate(o_inter, axis=0), gn_ref[...])
        gate = g_ref[:, vs[h]].astype(F32)
        og_ref[:, vs[h]] = (o * (gate * jax.nn.sigmoid(gate))).astype(BF16)

    for r in range(ts // GLA_OUT_RC):
        rs = pl.ds(r * GLA_OUT_RC, GLA_OUT_RC)
        acc = jnp.dot(og_ref[rs, :], wo_ref[:GLA_V_W, :], preferred_element_type=F32)
        acc = acc + jnp.dot(osw_ref[rs, :], wo_ref[GLA_V_W:, :], preferred_element_type=F32)
        h_ref[rs, :] = x_ref[rs, :] + acc


def _gla(proj, wa, ba, gn, x2, o_swa, w_out, batch, seq, ts, to_cast):
    nt = seq // ts
    cast_in, cast_out, cast_shapes = _cast_specs(to_cast, batch * nt, lambda b, i: b * nt + i)
    kern = functools.partial(_gla_kernel, ts=ts)
    rows = lambda b, i: b * nt + i
    idx = jnp.arange(GLA_SUB)
    causal = (idx[:, None] >= idx[None, :]) & (idx[:, None] // GLA_CHUNK == idx[None, :] // GLA_CHUNK)
    return pl.pallas_call(
        kern,
        out_shape=[jax.ShapeDtypeStruct((batch * seq, D_MODEL), F32)] + cast_shapes,
        grid=(batch, nt),
        in_specs=[
            pl.BlockSpec((ts, GLA_QK_W), lambda b, i: (rows(b, i), COL_GQ // GLA_QK_W)),
            pl.BlockSpec((ts, GLA_QK_W), lambda b, i: (rows(b, i), COL_GK // GLA_QK_W)),
            pl.BlockSpec((ts, GLA_V_W), lambda b, i: (rows(b, i), COL_GV // GLA_V_W)),
            pl.BlockSpec((ts, GLA_V_W), lambda b, i: (rows(b, i), COL_GG // GLA_V_W)),
            pl.BlockSpec((ts, LANES), lambda b, i: (rows(b, i), COL_GA // LANES)),
            pl.BlockSpec((LANES, GLA_QK_W), lambda b, i: (0, 0)),
            pl.BlockSpec((1, GLA_QK_W), lambda b, i: (0, 0)),
            pl.BlockSpec((1, GLA_DV), lambda b, i: (0, 0)),
            pl.BlockSpec((GLA_SUB, GLA_SUB), lambda b, i: (0, 0)),
            pl.BlockSpec((GLA_SUB, GLA_SUB), lambda b, i: (0, 0)),
            pl.BlockSpec((ts, D_MODEL), lambda b, i: (rows(b, i), 0)),
            pl.BlockSpec((ts, SWA_Q_W), lambda b, i: (rows(b, i), 0)),
            pl.BlockSpec((GLA_V_W + SWA_Q_W, D_MODEL), lambda b, i: (0, 0), pipeline_mode=pl.Buffered(1)),
        ] + cast_in,
        out_specs=[pl.BlockSpec((ts, D_MODEL), lambda b, i: (rows(b, i), 0))] + cast_out,
        scratch_shapes=[pltpu.VMEM((GLA_HEADS, GLA_DV, GLA_DK), F32), pltpu.VMEM((ts, GLA_V_W), BF16)],
        compiler_params=pltpu.CompilerParams(
            dimension_semantics=("arbitrary", "arbitrary"),
            vmem_limit_bytes=VMEM_LIMIT),
        name="gla",
    )(proj, proj, proj, proj, proj, wa, ba, gn, causal.astype(BF16), causal.astype(F32), x2, o_swa, w_out, *to_cast)


def _swa_kernel(sinks_ref, q_ref, k_ref, v_ref, cos_ref, sin_ref, eye_ref, o_ref, k2p_ref, vtp_ref):
    step = pl.program_id(1)
    wb = WINDOW
    hd = SWA_HEAD_DIM
    half = hd // 2
    nblk = SWA_BLOCKS
    rows = nblk * wb
    pairs_per_kv = SWA_Q_W // LANES // SWA_KV_HEADS
    kvs = range(SWA_KV_HEADS)
    lane = lax.broadcasted_iota(jnp.int32, (rows, LANES), 1)
    head_a = (lane % hd) < half

    @pl.when(step == 0)
    def _():
        k2p_ref[...] = jnp.zeros_like(k2p_ref)
        vtp_ref[...] = jnp.zeros_like(vtp_ref)

    cos = cos_ref[...]
    sin = sin_ref[...]

    def rope(t, cs, sn):
        return t * cs + pltpu.roll(t, hd, 1) * sn

    k_r = rope(k_ref[...].astype(F32), cos, sin)
    k2_cur = [jnp.where(head_a, k_r, pltpu.roll(k_r, half, 1)).astype(BF16),
              jnp.where(head_a, pltpu.roll(k_r, LANES - half, 1), k_r).astype(BF16)]
    k2_all = [jnp.concatenate([k2p_ref[c], k2_cur[c]], axis=0) for c in kvs]
    v_t = v_ref[...].astype(F32).T
    ones_rows = (lax.broadcasted_iota(jnp.int32, (SWA_VT_ROWS - hd, rows), 0) == 0).astype(F32)
    vt_cur = [jnp.concatenate([v_t[c * hd:(c + 1) * hd], ones_rows], axis=0).astype(BF16) for c in kvs]
    vt_all = [jnp.concatenate([vtp_ref[c], vt_cur[c]], axis=1) for c in kvs]

    ki = lax.broadcasted_iota(jnp.int32, (2 * wb, wb), 0)
    qi = lax.broadcasted_iota(jnp.int32, (2 * wb, wb), 1) + wb
    band = (ki <= qi) & (qi - ki < WINDOW)
    bias = jnp.where(band, 0.0, NEG_INF).astype(BF16)
    bias_first = jnp.where(band & ((step > 0) | (ki >= wb)), 0.0, NEG_INF).astype(BF16)

    qscale = (hd ** -0.5) * LOG2E
    cos_q = cos * qscale
    sin_q = sin * qscale

    n_tiles = SWA_Q_W // LANES
    lhs = []
    for t in range(n_tiles):
        q_r = rope(q_ref[:, t * LANES:(t + 1) * LANES].astype(F32), cos_q, sin_q)
        lhs.append([jnp.where(head_a, q_r, 0.0).astype(BF16), jnp.where(head_a, 0.0, q_r).astype(BF16)])
    st = []
    for x in range(nblk):
        for c in kvs:
            rows_x = slice(x * wb, (x + 1) * wb)
            lhs_xc = jnp.concatenate([lhs[t][i][rows_x] for t in range(c * pairs_per_kv, (c + 1) * pairs_per_kv)
                                      for i in range(2)], axis=0)
            lhs_xc = jnp.concatenate([lhs_xc, eye_ref[...]], axis=1)
            keys = jnp.concatenate([k2_all[c][x * wb:(x + 2) * wb], bias_first if x == 0 else bias], axis=1)
            st.append(lax.dot_general(keys, lhs_xc, _NT, preferred_element_type=F32))
    st = jnp.concatenate(st, axis=1)
    m = jnp.max(st, axis=0, keepdims=True)
    e = jnp.exp2(st - m).astype(BF16)
    sink = jnp.concatenate([jnp.full((1, wb), sinks_ref[i], F32) for i in range(SWA_HEADS)] * nblk, axis=1) * LOG2E
    sink_term = jnp.exp2(sink - m)
    cols_per_kv = 2 * pairs_per_kv * wb
    for x in range(nblk):
        for c in kvs:
            cols = slice((x * SWA_KV_HEADS + c) * cols_per_kv, (x * SWA_KV_HEADS + c + 1) * cols_per_kv)
            ot = jnp.dot(vt_all[c][:, x * wb:(x + 2) * wb], e[:, cols], preferred_element_type=F32)
            o_n = ot[:hd] * (1.0 / (ot[hd:hd + 1] + sink_term[:, cols]))
            for n in range(pairs_per_kv):
                t = c * pairs_per_kv + n
                pair = jnp.concatenate([o_n[:, 2 * n * wb:(2 * n + 1) * wb], o_n[:, (2 * n + 1) * wb:(2 * n + 2) * wb]],
                                       axis=0)
                o_ref[x * wb:(x + 1) * wb, t * LANES:(t + 1) * LANES] = pair.T.astype(BF16)

    for c in kvs:
        k2p_ref[c] = k2_cur[c][rows - wb:]
        vtp_ref[c] = vt_cur[c][:, rows - wb:]


def _swa(proj, sinks, cos_tab, sin_tab, batch, seq):
    rows = SWA_BLOCKS * WINDOW
    ns = seq // rows
    eye = jnp.tile(jnp.eye(WINDOW, dtype=BF16), (SWA_HEADS // SWA_KV_HEADS, 1))
    tile = lambda b, n: b * ns + n
    return pl.pallas_call(
        _swa_kernel,
        out_shape=jax.ShapeDtypeStruct((batch * seq, SWA_Q_W), BF16),
        grid=(batch, ns),
        in_specs=[
            pl.BlockSpec(memory_space=pltpu.SMEM),
            pl.BlockSpec((rows, SWA_Q_W), lambda b, n: (tile(b, n), COL_SQ // SWA_Q_W)),
            pl.BlockSpec((rows, SWA_KV_W), lambda b, n: (tile(b, n), COL_SK // SWA_KV_W)),
            pl.BlockSpec((rows, SWA_KV_W), lambda b, n: (tile(b, n), COL_SV // SWA_KV_W)),
            pl.BlockSpec((rows, LANES), lambda b, n: (tile(b, n), 0)),
            pl.BlockSpec((rows, LANES), lambda b, n: (tile(b, n), 0)),
            pl.BlockSpec((SWA_HEADS // SWA_KV_HEADS * WINDOW, WINDOW), lambda b, n: (0, 0)),
        ],
        out_specs=pl.BlockSpec((rows, SWA_Q_W), lambda b, n: (tile(b, n), 0)),
        scratch_shapes=[pltpu.VMEM((SWA_KV_HEADS, WINDOW, LANES), BF16),
                        pltpu.VMEM((SWA_KV_HEADS, SWA_VT_ROWS, WINDOW), BF16)],
        compiler_params=pltpu.CompilerParams(
            dimension_semantics=("arbitrary", "arbitrary"),
            vmem_limit_bytes=VMEM_LIMIT),
        name="swa",
    )(sinks, proj, proj, proj, cos_tab, sin_tab, eye)


def _outproj_kernel(x_ref, a_ref, b_ref, wa_ref, wb_ref, h_ref, *, tm, rc):
    for r in range(tm // rc):
        rs = pl.ds(r * rc, rc)
        acc = jnp.dot(a_ref[rs, :], wa_ref[...], preferred_element_type=F32)
        acc = acc + jnp.dot(b_ref[rs, :], wb_ref[...], preferred_element_type=F32)
        h_ref[rs, :] = x_ref[rs, :] + acc


def _outproj(x2, o_gla, o_swa, w, tm, rc):
    t = x2.shape[0]
    return pl.pallas_call(
        functools.partial(_outproj_kernel, tm=tm, rc=rc),
        out_shape=jax.ShapeDtypeStruct((t, D_MODEL), F32),
        grid=(t // tm,),
        in_specs=[
            pl.BlockSpec((tm, D_MODEL), lambda i: (i, 0)),
            pl.BlockSpec((tm, GLA_V_W), lambda i: (i, 0)),
            pl.BlockSpec((tm, SWA_Q_W), lambda i: (i, 0)),
            pl.BlockSpec((GLA_V_W, D_MODEL), lambda i: (0, 0), pipeline_mode=pl.Buffered(1)),
            pl.BlockSpec((SWA_Q_W, D_MODEL), lambda i: (GLA_V_W // SWA_Q_W, 0), pipeline_mode=pl.Buffered(1)),
        ],
        out_specs=pl.BlockSpec((tm, D_MODEL), lambda i: (i, 0)),
        compiler_params=pltpu.CompilerParams(
            dimension_semantics=("arbitrary",),
            vmem_limit_bytes=VMEM_LIMIT),
        name="outproj",
    )(x2, o_gla, o_swa, w, w)


def _ffn_kernel(h_ref, gf_ref, wg_ref, wu_ref, cw_ref, cb_ref, wd_ref, gl_ref, o_ref,
                hn_ref, carry_ref, *, tm, rc, seq):
    i = pl.program_id(0)
    j = pl.program_id(1)
    nj = pl.num_programs(1)

    def step(first, last):
        seq_start = (i * tm) % seq == 0
        prev = jnp.where(seq_start, 0.0, carry_ref[j])
        rows = lax.broadcasted_iota(jnp.int32, prev.shape, 0)
        cb = cb_ref[...]
        cw0, cw1, cw2 = cw_ref[0:1, :], cw_ref[1:2, :], cw_ref[2:3, :]

        for r in range(tm // rc):
            rs = pl.ds(r * rc, rc)
            if first:
                res = h_ref[rs, :]
                hn = _rms(res, gf_ref[...]).astype(BF16)
                hn_ref[rs, :] = hn
            else:
                res = o_ref[rs, :]
                hn = hn_ref[rs, :]
            gate = jnp.dot(hn, wg_ref[...], preferred_element_type=F32)
            up = jnp.dot(hn, wu_ref[...], preferred_element_type=F32)

            def shifted(d):
                rolled = pltpu.roll(gate, d, 0)
                top = jnp.where(rows < d, pltpu.roll(prev, d, 0), rolled[:SUBLANES])
                return jnp.concatenate([top, rolled[SUBLANES:]], axis=0)

            conv = cb + cw0 * shifted(2) + cw1 * shifted(1) + cw2 * gate
            act = (conv * jax.nn.sigmoid(conv) * up).astype(BF16)
            acc = res + jnp.dot(act, wd_ref[...], preferred_element_type=F32)
            o_ref[rs, :] = _rms(acc, gl_ref[...]) if last else acc
            prev = gate[rc - SUBLANES:, :]
        carry_ref[j] = prev

    pl.when(j == 0)(functools.partial(step, True, False))
    pl.when((j > 0) & (j < nj - 1))(functools.partial(step, False, False))
    pl.when(j == nj - 1)(functools.partial(step, False, True))


def _ffn(h, gf, wg, wu, cw, cb, wd, gl, seq, tm, tf, rc):
    t = h.shape[0]
    nj = D_FF // tf
    kern = functools.partial(_ffn_kernel, tm=tm, rc=rc, seq=seq)
    return pl.pallas_call(
        kern,
        out_shape=jax.ShapeDtypeStruct((t, D_MODEL), F32),
        grid=(t // tm, nj),
        in_specs=[
            pl.BlockSpec((tm, D_MODEL), lambda i, j: (i, 0)),
            pl.BlockSpec((1, D_MODEL), lambda i, j: (0, 0)),
            pl.BlockSpec((D_MODEL, tf), lambda i, j: (0, j)),
            pl.BlockSpec((D_MODEL, tf), lambda i, j: (0, j)),
            pl.BlockSpec((CONV_WIDTH, tf), lambda i, j: (0, j)),
            pl.BlockSpec((1, tf), lambda i, j: (0, j)),
            pl.BlockSpec((tf, D_MODEL), lambda i, j: (j, 0)),
            pl.BlockSpec((1, D_MODEL), lambda i, j: (0, 0)),
        ],
        out_specs=pl.BlockSpec((tm, D_MODEL), lambda i, j: (i, 0)),
        scratch_shapes=[pltpu.VMEM((tm, D_MODEL), BF16), pltpu.VMEM((nj, SUBLANES, tf), F32)],
        compiler_params=pltpu.CompilerParams(
            dimension_semantics=("arbitrary", "arbitrary"),
            vmem_limit_bytes=VMEM_LIMIT),
        name="convffn",
    )(h, gf, wg, wu, cw, cb, wd, gl)


def kernel(x, positions, attn_norm, w_in, w_a_up, b_a_up, gla_norm, sinks, w_out, ffn_norm,
           w_gate, w_up, conv_w, conv_b, w_down, final_norm):
    batch, seq, _ = x.shape
    t = batch * seq
    assert w_in.shape[0] == 1, "the final norm is fused into the single layer's FFN kernel"
    x2 = x.reshape(t, D_MODEL)
    pos3 = positions.reshape(t // WINDOW, 1, WINDOW)
    half = SWA_HEAD_DIM // 2
    freq_col = (ROPE_THETA ** (-jnp.arange(half, dtype=F32) / half)).reshape(half, 1)

    w_proj = _wprep(jnp.swapaxes(w_in[0], 0, 1), tk=512)
    wa = jnp.concatenate(
        [w_a_up[0], jnp.zeros((LANES - GLA_RANK, GLA_QK_W), F32)], axis=0).astype(BF16)

    proj, cos_tab, sin_tab, wg_bf, wu_bf, wo_bf = _inproj(x2, attn_norm[0].reshape(1, D_MODEL), w_proj, pos3, freq_col,
                                                          tm=512, rc=128, to_cast=[w_gate[0], w_up[0], w_out[0]])
    o_swa = _swa(proj, sinks[0], cos_tab, sin_tab, batch, seq)
    h, wd_bf = _gla(proj, wa, b_a_up[0].reshape(1, GLA_QK_W), gla_norm[0].reshape(1, GLA_DV), x2, o_swa, wo_bf,
                    batch, seq, ts=512, to_cast=[w_down[0]])
    y = _ffn(h, ffn_norm[0].reshape(1, D_MODEL), wg_bf, wu_bf, conv_w[0], conv_b[0].reshape(1, D_FF), wd_bf,
             final_norm.reshape(1, D_MODEL), seq, tm=1024, tf=512, rc=512)
    return y.reshape(batch, seq, D_MODEL)
```

```python
import functools

import jax
import jax.numpy as jnp
from jax import lax
from jax.experimental import pallas as pl
from jax.experimental.pallas import tpu as pltpu

D_MODEL = 2048
GLA_HEADS = 4
GLA_DK = 128
GLA_DV = 256
GLA_RANK = 16
GLA_GATE_NORM = 16.0
GLA_CHUNK = 64
GLA_OUT_RC = 512
GLA_SUB = 128
SWA_HEADS = 16
SWA_KV_HEADS = 2
SWA_HEAD_DIM = 64
WINDOW = 128
ROPE_THETA = 10000.0
D_FF = 5632
CONV_WIDTH = 3
EPS = 1e-6
NEG_INF = -1e30
LOG2E = 1.4426950408889634
SWA_VT_ROWS = SWA_HEAD_DIM + 16
SWA_BLOCKS = 4

GLA_QK_W = GLA_HEADS * GLA_DK
GLA_V_W = GLA_HEADS * GLA_DV
SWA_Q_W = SWA_HEADS * SWA_HEAD_DIM
SWA_KV_W = SWA_KV_HEADS * SWA_HEAD_DIM

LANES = 128
SUBLANES = 8

COL_GQ = 0
COL_GK = COL_GQ + GLA_QK_W
COL_GV = COL_GK + GLA_QK_W
COL_GG = COL_GV + GLA_V_W
COL_SQ = COL_GG + GLA_V_W
COL_SK = COL_SQ + SWA_Q_W
COL_SV = COL_SK + SWA_KV_W
COL_GA = COL_SV + SWA_KV_W
PROJ_TN = 1536
PROJ_W = 3 * PROJ_TN

VMEM_LIMIT = 58 * 1024 * 1024

F32 = jnp.float32
BF16 = jnp.bfloat16

_NT = (((1,), (1,)), ((), ()))
_TN = (((0,), (0,)), ((), ()))


def _rms(x, gain):
    return x * lax.rsqrt(jnp.mean(x * x, axis=-1, keepdims=True) + EPS) * gain


def _cast_specs(weights, steps, step_index):
    in_specs, out_specs, out_shapes = [], [], []
    for w in weights:
        rows, cols = w.shape
        slab = rows // steps
        assert slab * steps == rows and slab % (2 * SUBLANES) == 0, (w.shape, steps)
        spec = pl.BlockSpec((slab, cols), lambda *g: (step_index(*g), 0))
        in_specs.append(spec)
        out_specs.append(spec)
        out_shapes.append(jax.ShapeDtypeStruct(w.shape, BF16))
    return in_specs, out_specs, out_shapes


def _cast_slabs(src_refs, dst_refs):
    for src, dst in zip(src_refs, dst_refs):
        dst[...] = src[...].astype(BF16)


def _wprep_kernel(wt_ref, o_ref):
    tk = wt_ref.shape[1]
    half = SWA_HEAD_DIM // 2
    o_ga = 2 * GLA_QK_W + 2 * GLA_V_W
    o_sq = o_ga + GLA_RANK

    def put(col, rows):
        o_ref[:, col:col + rows.shape[0]] = rows.T.astype(BF16)

    put(0, wt_ref[:o_ga, :])
    for i in range((SWA_Q_W + SWA_KV_W) // LANES):
        r0 = o_sq + i * LANES
        blk = [wt_ref[r0 + j * half:r0 + (j + 1) * half, :] for j in range(4)]
        put(COL_SQ + i * LANES, jnp.concatenate([blk[0], blk[2], blk[1], blk[3]], axis=0))
    put(COL_SV, wt_ref[o_sq + SWA_Q_W + SWA_KV_W:, :])
    keep = lax.broadcasted_iota(jnp.int32, (LANES, tk), 0) < GLA_RANK
    put(COL_GA, jnp.where(keep, wt_ref[o_ga:o_ga + LANES, :], 0.0))
    o_ref[:, COL_GA + LANES:] = jnp.zeros((tk, PROJ_W - COL_GA - LANES), BF16)


def _wprep(wt, tk):
    n, k = wt.shape
    return pl.pallas_call(
        _wprep_kernel,
        out_shape=jax.ShapeDtypeStruct((k, PROJ_W), BF16),
        grid=(k // tk,),
        in_specs=[pl.BlockSpec((n, tk), lambda i: (0, i))],
        out_specs=pl.BlockSpec((tk, PROJ_W), lambda i: (i, 0)),
        compiler_params=pltpu.CompilerParams(
            dimension_semantics=("arbitrary",),
            vmem_limit_bytes=VMEM_LIMIT),
        name="wprep",
    )(wt)


def _inproj_kernel(x_ref, g_ref, w_ref, pos_ref, freq_ref, *rest, tm, rc):
    n_cast = (len(rest) - 3) // 2
    o_ref, cos_ref, sin_ref = rest[n_cast:n_cast + 3]
    _cast_slabs(rest[:n_cast], rest[n_cast + 3:])

    half = SWA_HEAD_DIM // 2
    low = lax.broadcasted_iota(jnp.int32, (WINDOW, LANES), 1) < SWA_HEAD_DIM
    for b in range(tm // WINDOW):
        ang_t = freq_ref[...] * pos_ref[b].astype(F32)
        cs = jnp.concatenate([jnp.cos(ang_t)] * (LANES // half), axis=0).T
        sn = jnp.concatenate([jnp.sin(ang_t)] * (LANES // half), axis=0).T
        cos_ref[b * WINDOW:(b + 1) * WINDOW, :] = cs
        sin_ref[b * WINDOW:(b + 1) * WINDOW, :] = jnp.where(low, -sn, sn)

    for r in range(tm // rc):
        rs = pl.ds(r * rc, rc)
        u = _rms(x_ref[rs, :], g_ref[...]).astype(BF16)
        o_ref[rs, :] = jnp.dot(u, w_ref[...], preferred_element_type=F32).astype(BF16)


def _inproj(x2, gain, w, pos3, freq_col, tm, rc, to_cast):
    t = x2.shape[0]
    cast_in, cast_out, cast_shapes = _cast_specs(to_cast, t // tm, lambda i: i)
    return pl.pallas_call(
        functools.partial(_inproj_kernel, tm=tm, rc=rc),
        out_shape=[jax.ShapeDtypeStruct((t, PROJ_W), BF16), jax.ShapeDtypeStruct((t, LANES), F32),
                   jax.ShapeDtypeStruct((t, LANES), F32)] + cast_shapes,
        grid=(t // tm,),
        in_specs=[
            pl.BlockSpec((tm, D_MODEL), lambda i: (i, 0)),
            pl.BlockSpec((1, D_MODEL), lambda i: (0, 0)),
            pl.BlockSpec((D_MODEL, PROJ_W), lambda i: (0, 0), pipeline_mode=pl.Buffered(1)),
            pl.BlockSpec((tm // WINDOW, 1, WINDOW), lambda i: (i, 0, 0)),
            pl.BlockSpec((SWA_HEAD_DIM // 2, 1), lambda i: (0, 0)),
        ] + cast_in,
        out_specs=[pl.BlockSpec((tm, PROJ_W), lambda i: (i, 0)), pl.BlockSpec((tm, LANES), lambda i: (i, 0)),
                   pl.BlockSpec((tm, LANES), lambda i: (i, 0))] + cast_out,
        compiler_params=pltpu.CompilerParams(
            dimension_semantics=("arbitrary",),
            vmem_limit_bytes=VMEM_LIMIT),
        name="inproj",
    )(x2, gain, w, pos3, freq_col, *to_cast)


def _gla_kernel(q_ref, k_ref, v_ref, g_ref, a_ref, wa_ref, ba_ref, gn_ref, tril_ref, mask_ref,
                x_ref, osw_ref, wo_ref, *rest, ts):
    c = GLA_CHUNK
    nc = ts // c
    n_cast = (len(rest) - 3) // 2
    h_ref, s_ref, og_ref = rest[n_cast], rest[-2], rest[-1]
    _cast_slabs(rest[:n_cast], rest[n_cast + 1:-2])
    heads = range(GLA_HEADS)
    ks = [slice(h * GLA_DK, (h + 1) * GLA_DK) for h in heads]
    vs = [slice(h * GLA_DV, (h + 1) * GLA_DV) for h in heads]

    @pl.when(pl.program_id(1) == 0)
    def _():
        s_ref[...] = jnp.zeros_like(s_ref)

    z = jnp.dot(a_ref[...], wa_ref[...], preferred_element_type=F32) + ba_ref[...]
    log_a = (jnp.minimum(z, 0.0) - jnp.log(1.0 + jnp.exp(-jnp.abs(z)))) * (1.0 / GLA_GATE_NORM)

    la_hi = log_a.astype(BF16)
    la_lo = (log_a - la_hi.astype(F32)).astype(BF16)
    la_hl = jnp.concatenate([la_hi, la_lo], axis=1)
    subs = [slice(s * GLA_SUB, (s + 1) * GLA_SUB) for s in range(ts // GLA_SUB)]
    cum2 = jnp.concatenate([jnp.dot(tril_ref[...], la_hl[rs], preferred_element_type=F32) for rs in subs], axis=0)
    bcum = cum2[:, :GLA_QK_W] + cum2[:, GLA_QK_W:]
    bcum3 = bcum.reshape(nc, c, GLA_QK_W)
    b_last = bcum3[:, c - 1:c, :]

    q = q_ref[...].astype(F32) * (GLA_DK ** -0.5)
    k = k_ref[...].astype(F32)
    q_e = (q * jnp.exp(bcum)).astype(BF16)
    k_e = (k * jnp.exp(-bcum)).astype(BF16)
    k_d = (k * jnp.exp(b_last - bcum3).reshape(ts, GLA_QK_W)).astype(BF16)
    decay = jnp.exp(b_last)
    v = [v_ref[:, vs[h]] for h in heads]

    keep = mask_ref[...] != 0.0

    for h in heads:
        o_intra = []
        for rs in subs:
            scores = lax.dot_general(q_e[rs, ks[h]], k_e[rs, ks[h]], _NT, preferred_element_type=F32)
            scores = jnp.where(keep, scores, 0.0).astype(BF16)
            o_intra.append(jnp.dot(scores, v[h][rs], preferred_element_type=F32))
        o_intra = jnp.concatenate(o_intra, axis=0)

        state_t = s_ref[h]
        o_inter = []
        for n in range(nc):
            rows = slice(n * c, (n + 1) * c)
            o_inter.append(lax.dot_general(q_e[rows, ks[h]], state_t.astype(BF16), _NT,
                                           preferred_element_type=F32))
            kv_t = lax.dot_general(v[h][rows], k_d[rows, ks[h]], _TN, preferred_element_type=F32)
            state_t = state_t * decay[n][:, ks[h]] + kv_t
        s_ref[h] = state_t

        o = _rms(o_intra + jnp.concatenate(o_inter, axis=0), gn_ref[...])
        gate = g_ref[:, vs[h]].astype(F32)
        og_ref[:, vs[h]] = (o * (gate * jax.nn.sigmoid(gate))).astype(BF16)

    for r in range(ts // GLA_OUT_RC):
        rs = pl.ds(r * GLA_OUT_RC, GLA_OUT_RC)
        acc = jnp.dot(og_ref[rs, :], wo_ref[:GLA_V_W, :], preferred_element_type=F32)
        acc = acc + jnp.dot(osw_ref[rs, :], wo_ref[GLA_V_W:, :], preferred_element_type=F32)
        h_ref[rs, :] = x_ref[rs, :] + acc


def _gla(proj, wa, ba, gn, x2, o_swa, w_out, batch, seq, ts, to_cast):
    nt = seq // ts
    cast_in, cast_out, cast_shapes = _cast_specs(to_cast, batch * nt, lambda b, i: b * nt + i)
    kern = functools.partial(_gla_kernel, ts=ts)
    rows = lambda b, i: b * nt + i
    idx = jnp.arange(GLA_SUB)
    causal = (idx[:, None] >= idx[None, :]) & (idx[:, None] // GLA_CHUNK == idx[None, :] // GLA_CHUNK)
    return pl.pallas_call(
        kern,
        out_shape=[jax.ShapeDtypeStruct((batch * seq, D_MODEL), F32)] + cast_shapes,
        grid=(batch, nt),
        in_specs=[
            pl.BlockSpec((ts, GLA_QK_W), lambda b, i: (rows(b, i), COL_GQ // GLA_QK_W)),
            pl.BlockSpec((ts, GLA_QK_W), lambda b, i: (rows(b, i), COL_GK // GLA_QK_W)),
            pl.BlockSpec((ts, GLA_V_W), lambda b, i: (rows(b, i), COL_GV // GLA_V_W)),
            pl.BlockSpec((ts, GLA_V_W), lambda b, i: (rows(b, i), COL_GG // GLA_V_W)),
            pl.BlockSpec((ts, LANES), lambda b, i: (rows(b, i), COL_GA // LANES)),
            pl.BlockSpec((LANES, GLA_QK_W), lambda b, i: (0, 0)),
            pl.BlockSpec((1, GLA_QK_W), lambda b, i: (0, 0)),
            pl.BlockSpec((1, GLA_DV), lambda b, i: (0, 0)),
            pl.BlockSpec((GLA_SUB, GLA_SUB), lambda b, i: (0, 0)),
            pl.BlockSpec((GLA_SUB, GLA_SUB), lambda b, i: (0, 0)),
            pl.BlockSpec((ts, D_MODEL), lambda b, i: (rows(b, i), 0)),
            pl.BlockSpec((ts, SWA_Q_W), lambda b, i: (rows(b, i), 0)),
            pl.BlockSpec((GLA_V_W + SWA_Q_W, D_MODEL), lambda b, i: (0, 0), pipeline_mode=pl.Buffered(1)),
        ] + cast_in,
        out_specs=[pl.BlockSpec((ts, D_MODEL), lambda b, i: (rows(b, i), 0))] + cast_out,
        scratch_shapes=[pltpu.VMEM((GLA_HEADS, GLA_DV, GLA_DK), F32), pltpu.VMEM((ts, GLA_V_W), BF16)],
        compiler_params=pltpu.CompilerParams(
            dimension_semantics=("arbitrary", "arbitrary"),
            vmem_limit_bytes=VMEM_LIMIT),
        name="gla",
    )(proj, proj, proj, proj, proj, wa, ba, gn, causal.astype(BF16), causal.astype(F32), x2, o_swa, w_out, *to_cast)


def _swa_kernel(sinks_ref, q_ref, k_ref, v_ref, cos_ref, sin_ref, eye_ref, o_ref, k2p_ref, vtp_ref):
    step = pl.program_id(1)
    wb = WINDOW
    hd = SWA_HEAD_DIM
    half = hd // 2
    nblk = SWA_BLOCKS
    rows = nblk * wb
    pairs_per_kv = SWA_Q_W // LANES // SWA_KV_HEADS
    kvs = range(SWA_KV_HEADS)
    lane = lax.broadcasted_iota(jnp.int32, (rows, LANES), 1)
    head_a = (lane % hd) < half

    @pl.when(step == 0)
    def _():
        k2p_ref[...] = jnp.zeros_like(k2p_ref)
        vtp_ref[...] = jnp.zeros_like(vtp_ref)

    cos = cos_ref[...]
    sin = sin_ref[...]

    def rope(t, cs, sn):
        return t * cs + pltpu.roll(t, hd, 1) * sn

    k_r = rope(k_ref[...].astype(F32), cos, sin)
    k2_cur = [jnp.where(head_a, k_r, pltpu.roll(k_r, half, 1)).astype(BF16),
              jnp.where(head_a, pltpu.roll(k_r, LANES - half, 1), k_r).astype(BF16)]
    k2_all = [jnp.concatenate([k2p_ref[c], k2_cur[c]], axis=0) for c in kvs]
    v_t = v_ref[...].astype(F32).T
    ones_rows = (lax.broadcasted_iota(jnp.int32, (SWA_VT_ROWS - hd, rows), 0) == 0).astype(F32)
    vt_cur = [jnp.concatenate([v_t[c * hd:(c + 1) * hd], ones_rows], axis=0).astype(BF16) for c in kvs]
    vt_all = [jnp.concatenate([vtp_ref[c], vt_cur[c]], axis=1) for c in kvs]

    ki = lax.broadcasted_iota(jnp.int32, (2 * wb, wb), 0)
    qi = lax.broadcasted_iota(jnp.int32, (2 * wb, wb), 1) + wb
    band = (ki <= qi) & (qi - ki < WINDOW)
    bias = jnp.where(band, 0.0, NEG_INF).astype(BF16)
    bias_first = jnp.where(band & ((step > 0) | (ki >= wb)), 0.0, NEG_INF).astype(BF16)

    qscale = (hd ** -0.5) * LOG2E
    cos_q = cos * qscale
    sin_q = sin * qscale

    n_tiles = SWA_Q_W // LANES
    lhs = []
    for t in range(n_tiles):
        q_r = rope(q_ref[:, t * LANES:(t + 1) * LANES].astype(F32), cos_q, sin_q)
        lhs.append([jnp.where(head_a, q_r, 0.0).astype(BF16), jnp.where(head_a, 0.0, q_r).astype(BF16)])
    st = []
    for x in range(nblk):
        for c in kvs:
            rows_x = slice(x * wb, (x + 1) * wb)
            lhs_xc = jnp.concatenate([lhs[t][i][rows_x] for t in range(c * pairs_per_kv, (c + 1) * pairs_per_kv)
                                      for i in range(2)], axis=0)
            lhs_xc = jnp.concatenate([lhs_xc, eye_ref[...]], axis=1)
            keys = jnp.concatenate([k2_all[c][x * wb:(x + 2) * wb], bias_first if x == 0 else bias], axis=1)
            st.append(lax.dot_general(keys, lhs_xc, _NT, preferred_element_type=F32))
    st = jnp.concatenate(st, axis=1)
    m = jnp.max(st, axis=0, keepdims=True)
    e = jnp.exp2(st - m).astype(BF16)
    sink = jnp.concatenate([jnp.full((1, wb), sinks_ref[i], F32) for i in range(SWA_HEADS)] * nblk, axis=1) * LOG2E
    sink_term = jnp.exp2(sink - m)
    cols_per_kv = 2 * pairs_per_kv * wb
    for x in range(nblk):
        for c in kvs:
            cols = slice((x * SWA_KV_HEADS + c) * cols_per_kv, (x * SWA_KV_HEADS + c + 1) * cols_per_kv)
            ot = jnp.dot(vt_all[c][:, x * wb:(x + 2) * wb], e[:, cols], preferred_element_type=F32)
            o_n = ot[:hd] * (1.0 / (ot[hd:hd + 1] + sink_term[:, cols]))
            for n in range(pairs_per_kv):
                t = c * pairs_per_kv + n
                pair = jnp.concatenate([o_n[:, 2 * n * wb:(2 * n + 1) * wb], o_n[:, (2 * n + 1) * wb:(2 * n + 2) * wb]],
                                       axis=0)
                o_ref[x * wb:(x + 1) * wb, t * LANES:(t + 1) * LANES] = pair.T.astype(BF16)

    for c in kvs:
        k2p_ref[c] = k2_cur[c][rows - wb:]
        vtp_ref[c] = vt_cur[c][:, rows - wb:]


def _swa(proj, sinks, cos_tab, sin_tab, batch, seq):
    rows = SWA_BLOCKS * WINDOW
    ns = seq // rows
    eye = jnp.tile(jnp.eye(WINDOW, dtype=BF16), (SWA_HEADS // SWA_KV_HEADS, 1))
    tile = lambda b, n: b * ns + n
    return pl.pallas_call(
        _swa_kernel,
        out_shape=jax.ShapeDtypeStruct((batch * seq, SWA_Q_W), BF16),
        grid=(batch, ns),
        in_specs=[
            pl.BlockSpec(memory_space=pltpu.SMEM),
            pl.BlockSpec((rows, SWA_Q_W), lambda b, n: (tile(b, n), COL_SQ // SWA_Q_W)),
            pl.BlockSpec((rows, SWA_KV_W), lambda b, n: (tile(b, n), COL_SK // SWA_KV_W)),
            pl.BlockSpec((rows, SWA_KV_W), lambda b, n: (tile(b, n), COL_SV // SWA_KV_W)),
            pl.BlockSpec((rows, LANES), lambda b, n: (tile(b, n), 0)),
            pl.BlockSpec((rows, LANES), lambda b, n: (tile(b, n), 0)),
            pl.BlockSpec((SWA_HEADS // SWA_KV_HEADS * WINDOW, WINDOW), lambda b, n: (0, 0)),
        ],
        out_specs=pl.BlockSpec((rows, SWA_Q_W), lambda b, n: (tile(b, n), 0)),
        scratch_shapes=[pltpu.VMEM((SWA_KV_HEADS, WINDOW, LANES), BF16),
                        pltpu.VMEM((SWA_KV_HEADS, SWA_VT_ROWS, WINDOW), BF16)],
        compiler_params=pltpu.CompilerParams(
            dimension_semantics=("arbitrary", "arbitrary"),
            vmem_limit_bytes=VMEM_LIMIT),
        name="swa",
    )(sinks, proj, proj, proj, cos_tab, sin_tab, eye)


def _ffn_kernel(h_ref, gf_ref, wg_ref, wu_ref, cw_ref, cb_ref, wd_ref, gl_ref, o_ref,
                hn_ref, carry_ref, *, tm, rc, seq):
    i = pl.program_id(0)
    j = pl.program_id(1)
    nj = pl.num_programs(1)

    def step(first, last):
        seq_start = (i * tm) % seq == 0
        prev = jnp.where(seq_start, 0.0, carry_ref[j])
        rows = lax.broadcasted_iota(jnp.int32, prev.shape, 0)
        cb = cb_ref[...]
        cw0, cw1, cw2 = cw_ref[0:1, :], cw_ref[1:2, :], cw_ref[2:3, :]

        for r in range(tm // rc):
            rs = pl.ds(r * rc, rc)
            if first:
                res = h_ref[rs, :]
                hn = _rms(res, gf_ref[...]).astype(BF16)
                hn_ref[rs, :] = hn
            else:
                res = o_ref[rs, :]
                hn = hn_ref[rs, :]
            gate = jnp.dot(hn, wg_ref[...], preferred_element_type=F32)
            up = jnp.dot(hn, wu_ref[...], preferred_element_type=F32)

            def shifted(d):
                rolled = pltpu.roll(gate, d, 0)
                top = jnp.where(rows < d, pltpu.roll(prev, d, 0), rolled[:SUBLANES])
                return jnp.concatenate([top, rolled[SUBLANES:]], axis=0)

            conv = cb + cw0 * shifted(2) + cw1 * shifted(1) + cw2 * gate
            act = (conv * jax.nn.sigmoid(conv) * up).astype(BF16)
            acc = res + jnp.dot(act, wd_ref[...], preferred_element_type=F32)
            o_ref[rs, :] = _rms(acc, gl_ref[...]) if last else acc
            prev = gate[rc - SUBLANES:, :]
        carry_ref[j] = prev

    pl.when(j == 0)(functools.partial(step, True, False))
    pl.when((j > 0) & (j < nj - 1))(functools.partial(step, False, False))
    pl.when(j == nj - 1)(functools.partial(step, False, True))


def _ffn(h, gf, wg, wu, cw, cb, wd, gl, seq, tm, tf, rc):
    t = h.shape[0]
    nj = D_FF // tf
    kern = functools.partial(_ffn_kernel, tm=tm, rc=rc, seq=seq)
    return pl.pallas_call(
        kern,
        out_shape=jax.ShapeDtypeStruct((t, D_MODEL), F32),
        grid=(t // tm, nj),
        in_specs=[
            pl.BlockSpec((tm, D_MODEL), lambda i, j: (i, 0)),
            pl.BlockSpec((1, D_MODEL), lambda i, j: (0, 0)),
            pl.BlockSpec((D_MODEL, tf), lambda i, j: (0, j)),
            pl.BlockSpec((D_MODEL, tf), lambda i, j: (0, j)),
            pl.BlockSpec((CONV_WIDTH, tf), lambda i, j: (0, j)),
            pl.BlockSpec((1, tf), lambda i, j: (0, j)),
            pl.BlockSpec((tf, D_MODEL), lambda i, j: (j, 0)),
            pl.BlockSpec((1, D_MODEL), lambda i, j: (0, 0)),
        ],
        out_specs=pl.BlockSpec((tm, D_MODEL), lambda i, j: (i, 0)),
        scratch_shapes=[pltpu.VMEM((tm, D_MODEL), BF16), pltpu.VMEM((nj, SUBLANES, tf), F32)],
        compiler_params=pltpu.CompilerParams(
            dimension_semantics=("arbitrary", "arbitrary"),
            vmem_limit_bytes=VMEM_LIMIT),
        name="convffn",
    )(h, gf, wg, wu, cw, cb, wd, gl)


def kernel(x, positions, attn_norm, w_in, w_a_up, b_a_up, gla_norm, sinks, w_out, ffn_norm,
           w_gate, w_up, conv_w, conv_b, w_down, final_norm):
    batch, seq, _ = x.shape
    t = batch * seq
    assert w_in.shape[0] == 1, "the final norm is fused into the single layer's FFN kernel"
    x2 = x.reshape(t, D_MODEL)
    pos3 = positions.reshape(t // WINDOW, 1, WINDOW)
    half = SWA_HEAD_DIM // 2
    freq_col = (ROPE_THETA ** (-jnp.arange(half, dtype=F32) / half)).reshape(half, 1)

    w_proj = _wprep(jnp.swapaxes(w_in[0], 0, 1), tk=512)
    wa = jnp.concatenate(
        [w_a_up[0], jnp.zeros((LANES - GLA_RANK, GLA_QK_W), F32)], axis=0).astype(BF16)

    proj, cos_tab, sin_tab, wg_bf, wu_bf, wo_bf = _inproj(x2, attn_norm[0].reshape(1, D_MODEL), w_proj, pos3, freq_col,
                                                          tm=512, rc=128, to_cast=[w_gate[0], w_up[0], w_out[0]])
    o_swa = _swa(proj, sinks[0], cos_tab, sin_tab, batch, seq)
    h, wd_bf = _gla(proj, wa, b_a_up[0].reshape(1, GLA_QK_W), gla_norm[0].reshape(1, GLA_DV), x2, o_swa, wo_bf,
                    batch, seq, ts=512, to_cast=[w_down[0]])
    y = _ffn(h, ffn_norm[0].reshape(1, D_MODEL), wg_bf, wu_bf, conv_w[0], conv_b[0].reshape(1, D_FF), wd_bf,
             final_norm.reshape(1, D_MODEL), seq, tm=1024, tf=512, rc=512)
    return y.reshape(batch, seq, D_MODEL)
```
